```python
import jax, jax.numpy as jnp
from jax import lax
import numpy as np

D_MODEL = 1024
BATCH = 8
SEQ = 2048
DEPTH = 1
DEC_BATCH = 128
DEC_SEQ = 1
PAST_LEN = 16384
PAGE_SIZE = 128

N_MEM = 256
CHUNK = 64
CONV_W = 4
EPS = 1e-6
GDN_HEADS = 8
GDN_DK = 128
GDN_DV = 128
GDN_QK = GDN_HEADS * GDN_DK
GDN_V = GDN_HEADS * GDN_DV
GDN_CONV_CH = 2 * GDN_QK + GDN_V
SSM_DINNER = 2 * D_MODEL
SSM_HEADDIM = 64
SSM_HEADS = SSM_DINNER // SSM_HEADDIM
SSM_GROUPS = 8
SSM_DSTATE = 128
SSM_BC = SSM_GROUPS * SSM_DSTATE
SSM_CONV_CH = SSM_DINNER + 2 * SSM_BC
X_HEADS = 4
X_HEADDIM = D_MODEL // X_HEADS
D_FF = 4 * D_MODEL
IN_SPLITS = (GDN_CONV_CH, GDN_V, GDN_HEADS, GDN_HEADS, SSM_DINNER, SSM_CONV_CH, SSM_HEADS, D_MODEL, D_MODEL)
IN_COLS = GDN_CONV_CH + GDN_V + 2 * GDN_HEADS + SSM_DINNER + SSM_CONV_CH + SSM_HEADS + 2 * D_MODEL

kernel_name = "hybrid_gdn_ssd_memxattn_step"


def split_cols(t, sizes):
    out, start = [], 0
    for s in sizes:
        out.append(t[..., start:start + s])
        start += s
    return out


def rmsnorm(x, g):
    xf = x.astype(jnp.float32)
    xf = xf * lax.rsqrt(jnp.mean(xf * xf, axis=-1, keepdims=True) + EPS)
    return (xf * g.astype(jnp.float32)).astype(x.dtype)


def l2norm(x):
    return x * lax.rsqrt(jnp.sum(x * x, axis=-1, keepdims=True) + EPS)


def chunk_len(L):
    return CHUNK if L % CHUNK == 0 else L


def causal_dwconv(x, buf, w):
    xc = jnp.concatenate([buf.astype(x.dtype), x], axis=1)
    y = lax.conv_general_dilated(xc, w[:, None, :].astype(x.dtype), window_strides=(1,), padding='VALID',
                                 dimension_numbers=('NWC', 'WIO', 'NWC'), feature_group_count=x.shape[-1])
    return y, xc[:, -(CONV_W - 1):]


def gated_delta_chunked(q, k, v, g, beta, S0):
    Bsz, L, H, dk = q.shape
    dv = v.shape[-1]
    C = chunk_len(L)
    n = L // C

    def blk(t):
        return jnp.moveaxis(t.reshape((Bsz, n, C, H) + t.shape[3:]), 3, 2)

    q, k, v, g, beta = blk(q), blk(k), blk(v), blk(g), blk(beta)
    gc = jnp.cumsum(g, axis=-1)
    idx = jnp.arange(C)
    tril = idx[:, None] >= idx[None, :]
    strict = idx[:, None] > idx[None, :]
    diff = gc[..., :, None] - gc[..., None, :]
    decay = jnp.where(tril, jnp.exp(jnp.where(tril, diff, 0.0)), 0.0)
    kb = k * beta[..., None]
    A = jnp.where(strict, jnp.einsum('bnhid,bnhjd->bnhij', kb, k) * decay, 0.0)
    rhs = jnp.concatenate([v * beta[..., None], kb * jnp.exp(gc)[..., None]], axis=-1)
    sol = lax.linalg.triangular_solve(A + jnp.eye(C, dtype=A.dtype), rhs, left_side=True, lower=True)
    u, w = sol[..., :dv], sol[..., dv:]
    aqk = jnp.einsum('bnhid,bnhjd->bnhij', q, k) * decay
    qg = q * jnp.exp(gc)[..., None]
    kd = k * jnp.exp(gc[..., -1:] - gc)[..., None]
    dch = jnp.exp(gc[..., -1])

    def step(S, inp):
        qg_i, kd_i, u_i, w_i, aqk_i, dch_i = inp
        v_new = u_i - jnp.einsum('bhcd,bhde->bhce', w_i, S)
        o_i = jnp.einsum('bhcd,bhde->bhce', qg_i, S) + jnp.einsum('bhij,bhje->bhie', aqk_i, v_new)
        S = S * dch_i[..., None, None] + jnp.einsum('bhcd,bhce->bhde', kd_i, v_new)
        return S, o_i

    xs = tuple(jnp.moveaxis(t, 1, 0) for t in (qg, kd, u, w, aqk, dch))
    S, o = lax.scan(step, S0, xs)
    o = jnp.moveaxis(jnp.moveaxis(o, 0, 1), 2, 3).reshape(Bsz, L, H, dv)
    return o, S


def ssd_chunked(x, dt, A, Bm, Cm, S0):
    Bsz, L, H, P = x.shape
    G, N = Bm.shape[2], Bm.shape[3]
    R = H // G
    C = chunk_len(L)
    n = L // C
    xdt = (x * dt[..., None]).reshape(Bsz, n, C, G, R, P)
    dA = (dt * A).reshape(Bsz, n, C, G, R).transpose(0, 1, 3, 4, 2)
    cs = jnp.cumsum(dA, axis=-1)
    idx = jnp.arange(C)
    tril = idx[:, None] >= idx[None, :]
    seg = cs[..., :, None] - cs[..., None, :]
    Lmat = jnp.where(tril, jnp.exp(jnp.where(tril, seg, 0.0)), 0.0)
    Bc = Bm.reshape(Bsz, n, C, G, N)
    Cc = Cm.reshape(Bsz, n, C, G, N)
    CB = jnp.einsum('bclgs,bcmgs->bcglm', Cc, Bc)
    y_diag = jnp.einsum('bcglm,bcgrlm,bcmgrp->bclgrp', CB, Lmat, xdt)
    decay_in = jnp.exp(cs)
    decay_out = jnp.exp(cs[..., -1:] - cs)
    chunk_decay = jnp.exp(cs[..., -1])

    def step(S, inp):
        Cc_i, Bc_i, xdt_i, din, dout, dch = inp
        y_off = jnp.einsum('blgs,bgrps,bgrl->blgrp', Cc_i, S, din)
        S = S * dch[..., None, None] + jnp.einsum('blgs,bgrl,blgrp->bgrps', Bc_i, dout, xdt_i)
        return S, y_off

    xs = tuple(jnp.moveaxis(t, 1, 0) for t in (Cc, Bc, xdt, decay_in, decay_out, chunk_decay))
    S, y_off = lax.scan(step, S0.reshape(Bsz, G, R, P, N), xs)
    y = y_diag + jnp.moveaxis(y_off, 0, 1)
    return y.reshape(Bsz, L, H, P), S.reshape(Bsz, H, P, N)


def mem_kv(mem, g_mem, w_mk, w_mv):
    Bsz = mem.shape[0]
    m = rmsnorm(mem, g_mem)
    k = (m @ w_mk).reshape(Bsz, N_MEM, X_HEADS, X_HEADDIM)
    v = (m @ w_mv).reshape(Bsz, N_MEM, X_HEADS, X_HEADDIM)
    return k, v


def cross_attn(h, mk, mv, w_cq, w_co):
    Bsz, L, _ = h.shape
    q = (h @ w_cq).reshape(Bsz, L, X_HEADS, X_HEADDIM)
    s = jnp.einsum('blhd,bmhd->bhlm', q, mk.astype(h.dtype)).astype(jnp.float32) * (X_HEADDIM ** -0.5)
    p = jax.nn.softmax(s, axis=-1).astype(h.dtype)
    o = jnp.einsum('bhlm,bmhd->blhd', p, mv.astype(h.dtype)).reshape(Bsz, L, D_MODEL)
    return o @ w_co


def block(x, mk, mv, gdn_buf, gdn_S, ssm_buf, ssm_S,
          g_mix, w_in, gdn_conv_w, gdn_A_log, gdn_dt_bias, gdn_norm_w, w_gdn_up,
          ssm_conv_w, ssm_conv_b, ssm_dt_bias, ssm_A_log, ssm_D, ssm_norm_w, w_ssm_up, w_out,
          g_x, w_cq, w_co, g_ff, w_ff_up, w_ff_down):
    f32 = jnp.float32
    Bsz, L, _ = x.shape
    h = rmsnorm(x, g_mix)
    proj = h @ w_in
    qkv, gdn_gate, b_raw, a_raw, z, xbc, dt_raw, gate_a, gate_b = split_cols(proj, IN_SPLITS)
    qkv, gdn_buf_new = causal_dwconv(qkv, gdn_buf, gdn_conv_w)
    qkv = jax.nn.silu(qkv).astype(f32)
    q, k, v = split_cols(qkv, (GDN_QK, GDN_QK, GDN_V))
    q = l2norm(q.reshape(Bsz, L, GDN_HEADS, GDN_DK)) * (GDN_DK ** -0.5)
    k = l2norm(k.reshape(Bsz, L, GDN_HEADS, GDN_DK))
    v = v.reshape(Bsz, L, GDN_HEADS, GDN_DV)
    beta = jax.nn.sigmoid(b_raw.astype(f32))
    g = -jnp.exp(gdn_A_log.astype(f32)) * jax.nn.softplus(a_raw.astype(f32) + gdn_dt_bias.astype(f32))
    o_a, gdn_S_new = gated_delta_chunked(q, k, v, g, beta, gdn_S.astype(f32))
    o_a = rmsnorm(o_a, gdn_norm_w) * jax.nn.silu(gdn_gate.astype(f32).reshape(Bsz, L, GDN_HEADS, GDN_DV))
    o_a = o_a.reshape(Bsz, L, GDN_V).astype(x.dtype)
    xbc, ssm_buf_new = causal_dwconv(xbc, ssm_buf, ssm_conv_w)
    xbc = jax.nn.silu(xbc + ssm_conv_b).astype(f32)
    xs_, Bm, Cm = split_cols(xbc, (SSM_DINNER, SSM_BC, SSM_BC))
    xs_ = xs_.reshape(Bsz, L, SSM_HEADS, SSM_HEADDIM)
    dt = jax.nn.softplus(dt_raw.astype(f32) + ssm_dt_bias.astype(f32))
    A = -jnp.exp(ssm_A_log.astype(f32))
    y_b, ssm_S_new = ssd_chunked(xs_, dt, A, Bm.reshape(Bsz, L, SSM_GROUPS, SSM_DSTATE),
                                 Cm.reshape(Bsz, L, SSM_GROUPS, SSM_DSTATE), ssm_S.astype(f32))
    y_b = y_b + ssm_D.astype(f32)[:, None] * xs_
    y_b = y_b.reshape(Bsz, L, SSM_DINNER) * jax.nn.silu(z.astype(f32))
    y_b = rmsnorm(y_b.reshape(Bsz, L, SSM_GROUPS, SSM_DINNER // SSM_GROUPS),
                  ssm_norm_w.reshape(SSM_GROUPS, SSM_DINNER // SSM_GROUPS))
    y_b = y_b.reshape(Bsz, L, SSM_DINNER).astype(x.dtype)
    merged = jax.nn.sigmoid(gate_a) * (o_a @ w_gdn_up) + jax.nn.sigmoid(gate_b) * (y_b @ w_ssm_up)
    x = x + merged @ w_out
    x = x + cross_attn(rmsnorm(x, g_x), mk, mv, w_cq, w_co)
    hf = rmsnorm(x, g_ff) @ w_ff_up
    x = x + jnp.square(jax.nn.relu(hf)) @ w_ff_down
    return x, gdn_buf_new, gdn_S_new.astype(x.dtype), ssm_buf_new, ssm_S_new.astype(x.dtype)


def setup_inputs(seed: int = 0) -> dict:
    key = jax.random.key(seed)
    ks = iter(jax.random.split(key, 64))

    def nrm(shape, scale=1.0):
        return jax.random.normal(next(ks), shape, jnp.float32) * scale

    def gain(shape):
        return 1.0 + nrm(shape, 0.02)

    def inv_softplus_dt(shape):
        dt = jnp.exp(jax.random.uniform(next(ks), shape, jnp.float32, np.log(1e-3), np.log(1e-1)))
        return dt + jnp.log(-jnp.expm1(-dt))

    def a_log(shape):
        return jnp.log(jax.random.uniform(next(ks), shape, jnp.float32, 1.0, 16.0))

    Ld = DEPTH
    inp = {}
    inp['x_prompt'] = nrm((BATCH, SEQ, D_MODEL))
    inp['x_sample'] = nrm((DEC_BATCH, DEC_SEQ, D_MODEL))
    inp['mem_prompt'] = nrm((BATCH, N_MEM, D_MODEL))
    inp['state_gdn_conv'] = nrm((Ld, DEC_BATCH, CONV_W - 1, GDN_CONV_CH))
    inp['state_gdn'] = nrm((Ld, DEC_BATCH, GDN_HEADS, GDN_DK, GDN_DV), 0.1)
    inp['state_ssm_conv'] = nrm((Ld, DEC_BATCH, CONV_W - 1, SSM_CONV_CH))
    inp['state_ssm'] = nrm((Ld, DEC_BATCH, SSM_HEADS, SSM_HEADDIM, SSM_DSTATE), 0.1)
    inp['cache_mem_k'] = nrm((Ld, DEC_BATCH, N_MEM, X_HEADS, X_HEADDIM))
    inp['cache_mem_v'] = nrm((Ld, DEC_BATCH, N_MEM, X_HEADS, X_HEADDIM))
    inp['g_mix'] = gain((Ld, D_MODEL))
    inp['w_in'] = nrm((Ld, D_MODEL, IN_COLS), D_MODEL ** -0.5)
    inp['gdn_conv_w'] = nrm((Ld, CONV_W, GDN_CONV_CH), CONV_W ** -0.5)
    inp['gdn_A_log'] = a_log((Ld, GDN_HEADS))
    inp['gdn_dt_bias'] = inv_softplus_dt((Ld, GDN_HEADS))
    inp['gdn_norm_w'] = gain((Ld, GDN_DV))
    inp['w_gdn_up'] = nrm((Ld, GDN_V, D_MODEL), GDN_V ** -0.5)
    inp['ssm_conv_w'] = nrm((Ld, CONV_W, SSM_CONV_CH), CONV_W ** -0.5)
    inp['ssm_conv_b'] = nrm((Ld, SSM_CONV_CH), 0.01)
    inp['ssm_dt_bias'] = inv_softplus_dt((Ld, SSM_HEADS))
    inp['ssm_A_log'] = a_log((Ld, SSM_HEADS))
    inp['ssm_D'] = gain((Ld, SSM_HEADS))
    inp['ssm_norm_w'] = gain((Ld, SSM_DINNER))
    inp['w_ssm_up'] = nrm((Ld, SSM_DINNER, D_MODEL), SSM_DINNER ** -0.5)
    inp['w_out'] = nrm((Ld, D_MODEL, D_MODEL), D_MODEL ** -0.5)
    inp['g_mem'] = gain((Ld, D_MODEL))
    inp['w_mk'] = nrm((Ld, D_MODEL, D_MODEL), D_MODEL ** -0.5)
    inp['w_mv'] = nrm((Ld, D_MODEL, D_MODEL), D_MODEL ** -0.5)
    inp['g_x'] = gain((Ld, D_MODEL))
    inp['w_cq'] = nrm((Ld, D_MODEL, D_MODEL), D_MODEL ** -0.5)
    inp['w_co'] = nrm((Ld, D_MODEL, D_MODEL), D_MODEL ** -0.5)
    inp['g_ff'] = gain((Ld, D_MODEL))
    inp['w_ff_up'] = nrm((Ld, D_MODEL, D_FF), D_MODEL ** -0.5)
    inp['w_ff_down'] = nrm((Ld, D_FF, D_MODEL), D_FF ** -0.5)
    inp['g_final'] = gain((D_MODEL,))
    return inp


def reference(x_prompt, x_sample, mem_prompt, state_gdn_conv, state_gdn, state_ssm_conv, state_ssm,
              cache_mem_k, cache_mem_v, g_mix, w_in, gdn_conv_w, gdn_A_log, gdn_dt_bias, gdn_norm_w, w_gdn_up,
              ssm_conv_w, ssm_conv_b, ssm_dt_bias, ssm_A_log, ssm_D, ssm_norm_w, w_ssm_up, w_out,
              g_mem, w_mk, w_mv, g_x, w_cq, w_co, g_ff, w_ff_up, w_ff_down, g_final):
    xp, xs = x_prompt, x_sample
    Bp = x_prompt.shape[0]
    p_gc, p_g, p_sc, p_s, p_mk, p_mv = [], [], [], [], [], []
    s_gc, s_g, s_sc, s_s = [], [], [], []
    for l in range(DEPTH):
        lw = (g_mix[l], w_in[l], gdn_conv_w[l], gdn_A_log[l], gdn_dt_bias[l], gdn_norm_w[l], w_gdn_up[l],
              ssm_conv_w[l], ssm_conv_b[l], ssm_dt_bias[l], ssm_A_log[l], ssm_D[l], ssm_norm_w[l], w_ssm_up[l],
              w_out[l], g_x[l], w_cq[l], w_co[l], g_ff[l], w_ff_up[l], w_ff_down[l])
        mk, mv = mem_kv(mem_prompt, g_mem[l], w_mk[l], w_mv[l])
        zb_g = jnp.zeros((Bp, CONV_W - 1, GDN_CONV_CH), xp.dtype)
        zs_g = jnp.zeros((Bp, GDN_HEADS, GDN_DK, GDN_DV), jnp.float32)
        zb_s = jnp.zeros((Bp, CONV_W - 1, SSM_CONV_CH), xp.dtype)
        zs_s = jnp.zeros((Bp, SSM_HEADS, SSM_HEADDIM, SSM_DSTATE), jnp.float32)
        xp, bg, sg, bs, ss = block(xp, mk, mv, zb_g, zs_g, zb_s, zs_s, *lw)
        p_gc.append(bg); p_g.append(sg); p_sc.append(bs); p_s.append(ss); p_mk.append(mk); p_mv.append(mv)
        xs, bg2, sg2, bs2, ss2 = block(xs, cache_mem_k[l], cache_mem_v[l], state_gdn_conv[l], state_gdn[l],
                                       state_ssm_conv[l], state_ssm[l], *lw)
        s_gc.append(bg2); s_g.append(sg2); s_sc.append(bs2); s_s.append(ss2)
    y_prompt = rmsnorm(xp, g_final)
    y_sample = rmsnorm(xs, g_final)
    return (y_prompt, y_sample, jnp.stack(p_gc), jnp.stack(p_g), jnp.stack(p_sc), jnp.stack(p_s),
            jnp.stack(p_mk), jnp.stack(p_mv), jnp.stack(s_gc), jnp.stack(s_g), jnp.stack(s_sc), jnp.stack(s_s))
```

```python
import functools

import jax
import jax.numpy as jnp
from jax import lax
from jax.experimental import pallas as pl
from jax.experimental.pallas import tpu as pltpu

F32 = jnp.float32
BF16 = jnp.bfloat16

EPS = 1e-6
CHUNK = 64
CONV_W = 4
D_MODEL = 1024
GDN_HEADS = 8
GDN_DK = 128
GDN_DV = 128
GDN_QK = GDN_HEADS * GDN_DK
GDN_V = GDN_HEADS * GDN_DV
GDN_CH = 2 * GDN_QK + GDN_V
SSM_DINNER = 2 * D_MODEL
SSM_P = 64
SSM_HEADS = SSM_DINNER // SSM_P
SSM_GROUPS = 8
SSM_R = SSM_HEADS // SSM_GROUPS
SSM_N = 128
SSM_BC = SSM_GROUPS * SSM_N
SSM_CH = SSM_DINNER + 2 * SSM_BC
SSM_GW = SSM_R * SSM_P
X_HEADS = 4
X_HD = D_MODEL // X_HEADS
D_FF = 4 * D_MODEL
IN_SPLITS = (GDN_CH, GDN_V, GDN_HEADS, GDN_HEADS, SSM_DINNER, SSM_CH, SSM_HEADS, D_MODEL, D_MODEL)

MAIN_COLS = GDN_CH + GDN_V + SSM_DINNER + SSM_CH + 2 * D_MODEL
OFF_QKV, OFF_GATE, OFF_Z = 0, GDN_CH, GDN_CH + GDN_V
OFF_XS = OFF_Z + SSM_DINNER
OFF_B = OFF_XS + SSM_DINNER
OFF_C = OFF_B + SSM_BC
OFF_GA = OFF_C + SSM_BC
OFF_GB = OFF_GA + D_MODEL
SMALL_COLS = 128
LANE_BETA, LANE_GDEC, LANE_DT = 0, GDN_HEADS, 2 * GDN_HEADS

STEP = 2 * CHUNK
TAIL = 8
SAMPLE_BT = 8
XATTN_BT = 4
VMEM_LIMIT = 48 * 1024 * 1024


def _params(n_axes):
    return pltpu.CompilerParams(dimension_semantics=("arbitrary",) * n_axes, vmem_limit_bytes=VMEM_LIMIT)


def _sigmoid(x):
    return 1.0 / (1.0 + jnp.exp(-x))


def _silu(x):
    return x * _sigmoid(x)


def _softplus(x):
    return jnp.maximum(x, 0.0) + jnp.log1p(jnp.exp(-jnp.abs(x)))


def _rmsnorm(x, g):
    xf = x.astype(F32)
    return xf * lax.rsqrt(jnp.mean(xf * xf, axis=-1, keepdims=True) + EPS) * g


def _dot(a, b):
    return jnp.dot(a.astype(BF16), b.astype(BF16), preferred_element_type=F32)


def _dot_nt(a, b):
    return lax.dot_general(a.astype(BF16), b.astype(BF16), (((1,), (1,)), ((), ())), preferred_element_type=F32)


def _dot_tn(a, b):
    return lax.dot_general(a.astype(BF16), b.astype(BF16), (((0,), (0,)), ((), ())), preferred_element_type=F32)


def _split2(a):
    hi = a.astype(BF16)
    return hi, (a - hi.astype(F32)).astype(BF16)


def _dot3(a, b):
    ah, al = _split2(a)
    bh, bl = _split2(b)
    d = functools.partial(jnp.dot, preferred_element_type=F32)
    return d(ah, bh) + (d(ah, bl) + d(al, bh))


def _dot3_tn(a, b):
    ah, al = _split2(a)
    bh, bl = _split2(b)
    d = functools.partial(lax.dot_general, dimension_numbers=(((0,), (0,)), ((), ())), preferred_element_type=F32)
    return d(ah, bh) + (d(ah, bl) + d(al, bh))


def _chunk_cumsum(x):
    n = x.shape[0]
    r = lax.broadcasted_iota(jnp.int32, (n, n), 0)
    c = lax.broadcasted_iota(jnp.int32, (n, n), 1)
    tri = jnp.where((r >= c) & ((r // CHUNK) == (c // CHUNK)), 1.0, 0.0).astype(BF16)
    h1 = x.astype(BF16)
    r1 = x - h1.astype(F32)
    h2 = r1.astype(BF16)
    h3 = (r1 - h2.astype(F32)).astype(BF16)
    d = functools.partial(jnp.dot, preferred_element_type=F32)
    return d(tri, h1) + (d(tri, h2) + d(tri, h3))


def _chunk_masks():
    r = lax.broadcasted_iota(jnp.int32, (CHUNK, CHUNK), 0)
    c = lax.broadcasted_iota(jnp.int32, (CHUNK, CHUNK), 1)
    return r >= c, r > c, jnp.where(r == c, 1.0, 0.0).astype(F32)


def _decay_matrix(col, row, tril):
    return jnp.where(tril, jnp.exp(jnp.where(tril, col - row, 0.0)), 0.0)


def _inv_unit_lower(a, eye):
    p = -a
    t = eye + p
    terms = 2
    while terms < a.shape[0]:
        p = _dot3(p, p)
        t = t + _dot3(t, p)
        terms *= 2
    return t


def _rms_mm_body(x_ref, g_ref, w_ref, o_ref, hn_ref):
    @pl.when(pl.program_id(1) == 0)
    def _():
        hn_ref[...] = _rmsnorm(x_ref[...], g_ref[...]).astype(BF16)

    o_ref[...] = jnp.dot(hn_ref[...], w_ref[...], preferred_element_type=F32).astype(o_ref.dtype)


def _rms_mm(x, g, w, *, tm, tn, out_dtype=F32):
    t, k = x.shape
    n = w.shape[1]
    tm, tn = min(tm, t), min(tn, n)
    assert t % tm == 0 and n % tn == 0
    return pl.pallas_call(
        _rms_mm_body,
        grid=(t // tm, n // tn),
        in_specs=[pl.BlockSpec((tm, k), lambda i, j: (i, 0)),
                  pl.BlockSpec((1, k), lambda i, j: (0, 0)),
                  pl.BlockSpec((k, tn), lambda i, j: (0, j))],
        out_specs=pl.BlockSpec((tm, tn), lambda i, j: (i, j)),
        out_shape=jax.ShapeDtypeStruct((t, n), out_dtype),
        scratch_shapes=[pltpu.VMEM((tm, k), BF16)],
        compiler_params=_params(2),
        name="rms_matmul",
    )(x, g.reshape(1, k), w)


def _gdn_prompt_body(qkv_ref, gate_ref, sm_ref, cw_ref, par_ref, nw_ref, o_ref, sfin_ref, xbuf, s_ref):
    step = pl.program_id(1)

    @pl.when(step == 0)
    def _():
        xbuf[0:TAIL, :] = jnp.zeros((TAIL, GDN_CH), F32)
        s_ref[...] = jnp.zeros_like(s_ref)

    xbuf[TAIL:TAIL + STEP, :] = qkv_ref[...]
    sm = sm_ref[...]
    beta_all = _sigmoid(sm)
    g_all = -jnp.exp(par_ref[0:1, :]) * _softplus(sm + par_ref[1:2, :])
    gc_all = _chunk_cumsum(g_all)
    gc_t = gc_all.T
    tril, strict, eye = _chunk_masks()
    nw = nw_ref[...]

    def conv(blk, r0):
        lo = blk * 128
        acc = cw_ref[CONV_W - 1:CONV_W, lo:lo + 128] * xbuf[pl.ds(TAIL + r0, CHUNK), lo:lo + 128]
        for j in range(CONV_W - 1):
            acc = acc + cw_ref[j:j + 1, lo:lo + 128] * xbuf[pl.ds(TAIL - (CONV_W - 1) + j + r0, CHUNK), lo:lo + 128]
        return _silu(acc)

    for c in range(STEP // CHUNK):
        r0 = c * CHUNK
        for h in range(GDN_HEADS):
            q = conv(h, r0)
            k = conv(GDN_HEADS + h, r0)
            v = conv(2 * GDN_HEADS + h, r0)
            q = q * lax.rsqrt(jnp.sum(q * q, axis=-1, keepdims=True) + EPS) * (GDN_DK ** -0.5)
            k = k * lax.rsqrt(jnp.sum(k * k, axis=-1, keepdims=True) + EPS)
            lg = LANE_GDEC + h
            bcol = beta_all[r0:r0 + CHUNK, LANE_BETA + h:LANE_BETA + h + 1]
            gcol = gc_all[r0:r0 + CHUNK, lg:lg + 1]
            grow = gc_t[lg:lg + 1, r0:r0 + CHUNK]
            glast = gc_all[r0 + CHUNK - 1:r0 + CHUNK, lg:lg + 1]
            decay = _decay_matrix(gcol, grow, tril)
            kb = k * bcol
            a = jnp.where(strict, _dot_nt(kb, k) * decay, 0.0)
            tinv = _inv_unit_lower(a, eye)
            egc = jnp.exp(gcol)
            sol = _dot3(tinv, jnp.concatenate([v * bcol, kb * egc], axis=1))
            u, w = sol[:, :GDN_DV], sol[:, GDN_DV:]
            aqk = _dot_nt(q, k) * decay
            kd = k * jnp.exp(glast - gcol)
            s_old = s_ref[h]
            ws = _dot(jnp.concatenate([w, q * egc], axis=0), s_old)
            v_new = u - ws[:CHUNK]
            o = ws[CHUNK:] + _dot(aqk, v_new)
            s_ref[h] = s_old * jnp.exp(glast) + _dot_tn(kd, v_new)
            gt = gate_ref[r0:r0 + CHUNK, h * GDN_DV:(h + 1) * GDN_DV]
            o_ref[r0:r0 + CHUNK, h * GDN_DV:(h + 1) * GDN_DV] = (_rmsnorm(o, nw) * _silu(gt)).astype(o_ref.dtype)

    xbuf[0:TAIL, :] = xbuf[STEP:STEP + TAIL, :]

    @pl.when(step == pl.num_programs(1) - 1)
    def _():
        sfin_ref[0] = s_ref[...]


def _gdn_prompt(proj, small, cw, par, nw, bsz, seq):
    nsteps = seq // STEP
    row = lambda b, s: b * nsteps + s
    return pl.pallas_call(
        _gdn_prompt_body,
        grid=(bsz, nsteps),
        in_specs=[pl.BlockSpec((STEP, GDN_CH), lambda b, s: (row(b, s), OFF_QKV // GDN_CH)),
                  pl.BlockSpec((STEP, GDN_V), lambda b, s: (row(b, s), OFF_GATE // GDN_V)),
                  pl.BlockSpec((STEP, SMALL_COLS), lambda b, s: (row(b, s), 0)),
                  pl.BlockSpec((CONV_W, GDN_CH), lambda b, s: (0, 0)),
                  pl.BlockSpec((8, SMALL_COLS), lambda b, s: (0, 0)),
                  pl.BlockSpec((1, GDN_DV), lambda b, s: (0, 0))],
        out_specs=[pl.BlockSpec((STEP, GDN_V), lambda b, s: (row(b, s), 0)),
                   pl.BlockSpec((1, GDN_HEADS, GDN_DK, GDN_DV), lambda b, s: (b, 0, 0, 0))],
        out_shape=[jax.ShapeDtypeStruct((bsz * seq, GDN_V), BF16),
                   jax.ShapeDtypeStruct((bsz, GDN_HEADS, GDN_DK, GDN_DV), F32)],
        scratch_shapes=[pltpu.VMEM((TAIL + STEP, GDN_CH), F32),
                        pltpu.VMEM((GDN_HEADS, GDN_DK, GDN_DV), F32)],
        compiler_params=_params(2),
        name="gdn_prompt",
    )(proj, proj, small, cw, par, nw)


def _ssd_prompt_body(xs_ref, b_ref, c_ref, z_ref, sm_ref, cw_ref, cb_ref, par_ref, d_ref, nw_ref,
                     y_ref, sfin_ref, xb_x, xb_b, xb_c, st_ref):
    step = pl.program_id(1)

    @pl.when(step == 0)
    def _():
        xb_x[0:TAIL, :] = jnp.zeros((TAIL, SSM_DINNER), F32)
        xb_b[0:TAIL, :] = jnp.zeros((TAIL, SSM_BC), F32)
        xb_c[0:TAIL, :] = jnp.zeros((TAIL, SSM_BC), F32)
        st_ref[...] = jnp.zeros_like(st_ref)

    xb_x[TAIL:TAIL + STEP, :] = xs_ref[...]
    xb_b[TAIL:TAIL + STEP, :] = b_ref[...]
    xb_c[TAIL:TAIL + STEP, :] = c_ref[...]
    sm = sm_ref[...]
    dt_all = _softplus(sm + par_ref[0:1, :])
    cs_all = _chunk_cumsum(dt_all * (-jnp.exp(par_ref[1:2, :])))
    cs_t = cs_all.T
    tril, _, _ = _chunk_masks()

    def conv(buf, r0, lo, width, woff):
        wl = woff + lo
        acc = cw_ref[CONV_W - 1:CONV_W, wl:wl + width] * buf[pl.ds(TAIL + r0, CHUNK), lo:lo + width]
        acc = acc + cb_ref[0:1, wl:wl + width]
        for j in range(CONV_W - 1):
            acc = acc + cw_ref[j:j + 1, wl:wl + width] * buf[pl.ds(TAIL - (CONV_W - 1) + j + r0, CHUNK), lo:lo + width]
        return _silu(acc)

    for c in range(STEP // CHUNK):
        r0 = c * CHUNK
        for g in range(SSM_GROUPS):
            bc = conv(xb_b, r0, g * SSM_N, SSM_N, SSM_DINNER)
            cc = conv(xb_c, r0, g * SSM_N, SSM_N, SSM_DINNER + SSM_BC)
            cb = _dot_nt(cc, bc)
            st_g = st_ref[g]
            yoff = _dot(cc, st_g)
            ys, xds, dchs = [], [], []
            for pr in range(SSM_R // 2):
                xs_pair = conv(xb_x, r0, g * SSM_GW + pr * 2 * SSM_P, 2 * SSM_P, 0)
                for half in range(2):
                    rr = 2 * pr + half
                    h = SSM_R * g + rr
                    lane = LANE_DT + h
                    xs_h = xs_pair[:, half * SSM_P:(half + 1) * SSM_P]
                    dtc = dt_all[r0:r0 + CHUNK, lane:lane + 1]
                    csc = cs_all[r0:r0 + CHUNK, lane:lane + 1]
                    csr = cs_t[lane:lane + 1, r0:r0 + CHUNK]
                    csl = cs_all[r0 + CHUNK - 1:r0 + CHUNK, lane:lane + 1]
                    xdt = xs_h * dtc
                    ydiag = _dot(cb * _decay_matrix(csc, csr, tril), xdt)
                    ys.append(ydiag + yoff[:, rr * SSM_P:(rr + 1) * SSM_P] * jnp.exp(csc)
                              + d_ref[0:1, h * SSM_P:(h + 1) * SSM_P] * xs_h)
                    xds.append(xdt * jnp.exp(csl - csc))
                    dchs.append(jnp.broadcast_to(jnp.exp(csl), (1, SSM_P)))
            st_ref[g] = st_g * jnp.concatenate(dchs, axis=1) + _dot_tn(bc, jnp.concatenate(xds, axis=1))
            y_g = jnp.concatenate(ys, axis=1) * _silu(z_ref[r0:r0 + CHUNK, g * SSM_GW:(g + 1) * SSM_GW])
            y_ref[r0:r0 + CHUNK, g * SSM_GW:(g + 1) * SSM_GW] = _rmsnorm(
                y_g, nw_ref[0:1, g * SSM_GW:(g + 1) * SSM_GW]).astype(y_ref.dtype)

    xb_x[0:TAIL, :] = xb_x[STEP:STEP + TAIL, :]
    xb_b[0:TAIL, :] = xb_b[STEP:STEP + TAIL, :]
    xb_c[0:TAIL, :] = xb_c[STEP:STEP + TAIL, :]

    @pl.when(step == pl.num_programs(1) - 1)
    def _():
        for g in range(SSM_GROUPS):
            st_t = st_ref[g].T
            for rr in range(SSM_R):
                sfin_ref[0, SSM_R * g + rr] = st_t[rr * SSM_P:(rr + 1) * SSM_P, :]


def _ssd_prompt(proj, small, cw, cb, par, d_row, nw, bsz, seq):
    nsteps = seq // STEP
    row = lambda b, s: b * nsteps + s
    const = lambda b, s: (0, 0)
    return pl.pallas_call(
        _ssd_prompt_body,
        grid=(bsz, nsteps),
        in_specs=[pl.BlockSpec((STEP, SSM_DINNER), lambda b, s: (row(b, s), OFF_XS // SSM_DINNER)),
                  pl.BlockSpec((STEP, SSM_BC), lambda b, s: (row(b, s), OFF_B // SSM_BC)),
                  pl.BlockSpec((STEP, SSM_BC), lambda b, s: (row(b, s), OFF_C // SSM_BC)),
                  pl.BlockSpec((STEP, SSM_DINNER), lambda b, s: (row(b, s), OFF_Z // SSM_DINNER)),
                  pl.BlockSpec((STEP, SMALL_COLS), lambda b, s: (row(b, s), 0)),
                  pl.BlockSpec((CONV_W, SSM_CH), const),
                  pl.BlockSpec((1, SSM_CH), const),
                  pl.BlockSpec((8, SMALL_COLS), const),
                  pl.BlockSpec((1, SSM_DINNER), const),
                  pl.BlockSpec((1, SSM_DINNER), const)],
        out_specs=[pl.BlockSpec((STEP, SSM_DINNER), lambda b, s: (row(b, s), 0)),
                   pl.BlockSpec((1, SSM_HEADS, SSM_P, SSM_N), lambda b, s: (b, 0, 0, 0))],
        out_shape=[jax.ShapeDtypeStruct((bsz * seq, SSM_DINNER), BF16),
                   jax.ShapeDtypeStruct((bsz, SSM_HEADS, SSM_P, SSM_N), F32)],
        scratch_shapes=[pltpu.VMEM((TAIL + STEP, SSM_DINNER), F32),
                        pltpu.VMEM((TAIL + STEP, SSM_BC), F32),
                        pltpu.VMEM((TAIL + STEP, SSM_BC), F32),
                        pltpu.VMEM((SSM_GROUPS, SSM_N, SSM_GW), F32)],
        compiler_params=_params(2),
        name="ssd_prompt",
    )(proj, proj, proj, proj, small, cw, cb, par, d_row, nw)


def _gdn_sample_body(qkv_ref, gate_ref, sm_ref, cst_ref, s_ref, cw_ref, par_ref, nw_ref, o_ref, sout_ref):
    bt = qkv_ref.shape[0]
    sm = sm_ref[...]
    beta_all = _sigmoid(sm)
    eg_all = jnp.exp(-jnp.exp(par_ref[0:1, :]) * _softplus(sm + par_ref[1:2, :]))
    rowid = lax.broadcasted_iota(jnp.int32, (bt, GDN_DK), 0)
    nw = nw_ref[...]

    def conv(blk):
        lo = blk * 128
        acc = cw_ref[CONV_W - 1:CONV_W, lo:lo + 128] * qkv_ref[:, lo:lo + 128]
        for j in range(CONV_W - 1):
            acc = acc + cw_ref[j:j + 1, lo:lo + 128] * cst_ref[:, j * GDN_CH + lo:j * GDN_CH + lo + 128]
        return _silu(acc)

    for h in range(GDN_HEADS):
        q = conv(h)
        k = conv(GDN_HEADS + h)
        v = conv(2 * GDN_HEADS + h)
        q = q * lax.rsqrt(jnp.sum(q * q, axis=-1, keepdims=True) + EPS) * (GDN_DK ** -0.5)
        k = k * lax.rsqrt(jnp.sum(k * k, axis=-1, keepdims=True) + EPS)
        bcol = beta_all[:, LANE_BETA + h:LANE_BETA + h + 1]
        egcol = eg_all[:, LANE_GDEC + h:LANE_GDEC + h + 1]
        qs_rows, ks_rows = [], []
        for t in range(bt):
            lhs = jnp.where(rowid == 0, q[t:t + 1, :], jnp.where(rowid == 1, k[t:t + 1, :], 0.0))
            r = _dot(lhs, s_ref[t, h])
            qs_rows.append(r[0:1])
            ks_rows.append(r[1:2])
        q_s = jnp.concatenate(qs_rows, axis=0)
        k_s = jnp.concatenate(ks_rows, axis=0)
        v_new = bcol * v - (bcol * egcol) * k_s
        o = egcol * q_s + jnp.sum(q * k, axis=-1, keepdims=True) * v_new
        for t in range(bt):
            k_t = jnp.where(rowid == t, k, 0.0)
            sout_ref[t, h] = s_ref[t, h] * egcol[t:t + 1, :] + _dot3_tn(k_t, v_new)
        gt = gate_ref[:, h * GDN_DV:(h + 1) * GDN_DV]
        o_ref[:, h * GDN_DV:(h + 1) * GDN_DV] = (_rmsnorm(o, nw) * _silu(gt)).astype(o_ref.dtype)


def _gdn_sample(proj, small, cstate, state, cw, par, nw):
    t = proj.shape[0]
    bt = SAMPLE_BT
    const = lambda i: (0, 0)
    return pl.pallas_call(
        _gdn_sample_body,
        grid=(t // bt,),
        in_specs=[pl.BlockSpec((bt, GDN_CH), lambda i: (i, OFF_QKV // GDN_CH)),
                  pl.BlockSpec((bt, GDN_V), lambda i: (i, OFF_GATE // GDN_V)),
                  pl.BlockSpec((bt, SMALL_COLS), lambda i: (i, 0)),
                  pl.BlockSpec((bt, (CONV_W - 1) * GDN_CH), lambda i: (i, 0)),
                  pl.BlockSpec((bt, GDN_HEADS, GDN_DK, GDN_DV), lambda i: (i, 0, 0, 0)),
                  pl.BlockSpec((CONV_W, GDN_CH), const),
                  pl.BlockSpec((8, SMALL_COLS), const),
                  pl.BlockSpec((1, GDN_DV), const)],
        out_specs=[pl.BlockSpec((bt, GDN_V), lambda i: (i, 0)),
                   pl.BlockSpec((bt, GDN_HEADS, GDN_DK, GDN_DV), lambda i: (i, 0, 0, 0))],
        out_shape=[jax.ShapeDtypeStruct((t, GDN_V), BF16),
                   jax.ShapeDtypeStruct(state.shape, F32)],
        compiler_params=_params(1),
        name="gdn_sample",
    )(proj, proj, small, cstate, state, cw, par, nw)


def _ssd_sample_body(xs_ref, b_ref, c_ref, z_ref, sm_ref, cst_ref, s_ref, cw_ref, cb_ref, par_ref, d_ref, nw_ref,
                     y_ref, sout_ref):
    bt = xs_ref.shape[0]
    sm = sm_ref[...]
    dt_all = _softplus(sm + par_ref[0:1, :])
    e_all = jnp.exp(dt_all * (-jnp.exp(par_ref[1:2, :])))
    rowid_n = lax.broadcasted_iota(jnp.int32, (bt, SSM_N), 0)
    rowid_g = lax.broadcasted_iota(jnp.int32, (bt, SSM_GW), 0)

    def conv(x_ref, lo, width, woff):
        wl = woff + lo
        acc = cw_ref[CONV_W - 1:CONV_W, wl:wl + width] * x_ref[:, lo:lo + width] + cb_ref[0:1, wl:wl + width]
        for j in range(CONV_W - 1):
            acc = acc + cw_ref[j:j + 1, wl:wl + width] * cst_ref[:, j * SSM_CH + wl:j * SSM_CH + wl + width]
        return _silu(acc)

    for g in range(SSM_GROUPS):
        bc = conv(b_ref, g * SSM_N, SSM_N, SSM_DINNER)
        cc = conv(c_ref, g * SSM_N, SSM_N, SSM_DINNER + SSM_BC)
        xs_g = conv(xs_ref, g * SSM_GW, SSM_GW, 0)
        cb = jnp.sum(cc * bc, axis=-1, keepdims=True)
        yoff_rows = []
        for t in range(bt):
            c_t = jnp.where(rowid_n == 0, cc[t:t + 1, :], 0.0)
            s_tg = jnp.concatenate([s_ref[t, SSM_R * g + rr] for rr in range(SSM_R)], axis=0)
            yoff_rows.append(_dot_nt(c_t, s_tg)[0:1])
        yoff = jnp.concatenate(yoff_rows, axis=0)
        lanes = [LANE_DT + SSM_R * g + rr for rr in range(SSM_R)]
        dt_g = jnp.concatenate([jnp.broadcast_to(dt_all[:, l:l + 1], (bt, SSM_P)) for l in lanes], axis=1)
        e_g = jnp.concatenate([jnp.broadcast_to(e_all[:, l:l + 1], (bt, SSM_P)) for l in lanes], axis=1)
        xdt = xs_g * dt_g
        y = cb * xdt + yoff * e_g + d_ref[0:1, g * SSM_GW:(g + 1) * SSM_GW] * xs_g
        for t in range(bt):
            outer = _dot3_tn(jnp.where(rowid_g == t, xdt, 0.0), bc)
            for rr in range(SSM_R):
                h = SSM_R * g + rr
                sout_ref[t, h] = (s_ref[t, h] * e_all[t:t + 1, lanes[rr]:lanes[rr] + 1]
                                  + outer[rr * SSM_P:(rr + 1) * SSM_P, :])
        y = y * _silu(z_ref[:, g * SSM_GW:(g + 1) * SSM_GW])
        y_ref[:, g * SSM_GW:(g + 1) * SSM_GW] = _rmsnorm(
            y, nw_ref[0:1, g * SSM_GW:(g + 1) * SSM_GW]).astype(y_ref.dtype)


def _ssd_sample(proj, small, cstate, state, cw, cb, par, d_row, nw):
    t = proj.shape[0]
    bt = SAMPLE_BT
    const = lambda i: (0, 0)
    return pl.pallas_call(
        _ssd_sample_body,
        grid=(t // bt,),
        in_specs=[pl.BlockSpec((bt, SSM_DINNER), lambda i: (i, OFF_XS // SSM_DINNER)),
                  pl.BlockSpec((bt, SSM_BC), lambda i: (i, OFF_B // SSM_BC)),
                  pl.BlockSpec((bt, SSM_BC), lambda i: (i, OFF_C // SSM_BC)),
                  pl.BlockSpec((bt, SSM_DINNER), lambda i: (i, OFF_Z // SSM_DINNER)),
                  pl.BlockSpec((bt, SMALL_COLS), lambda i: (i, 0)),
                  pl.BlockSpec((bt, (CONV_W - 1) * SSM_CH), lambda i: (i, 0)),
                  pl.BlockSpec((bt, SSM_HEADS, SSM_P, SSM_N), lambda i: (i, 0, 0, 0)),
                  pl.BlockSpec((CONV_W, SSM_CH), const),
                  pl.BlockSpec((1, SSM_CH), const),
                  pl.BlockSpec((8, SMALL_COLS), const),
                  pl.BlockSpec((1, SSM_DINNER), const),
                  pl.BlockSpec((1, SSM_DINNER), const)],
        out_specs=[pl.BlockSpec((bt, SSM_DINNER), lambda i: (i, 0)),
                   pl.BlockSpec((bt, SSM_HEADS, SSM_P, SSM_N), lambda i: (i, 0, 0, 0))],
        out_shape=[jax.ShapeDtypeStruct((t, SSM_DINNER), BF16),
                   jax.ShapeDtypeStruct(state.shape, F32)],
        compiler_params=_params(1),
        name="ssd_sample",
    )(proj, proj, proj, proj, small, cstate, state, cw, cb, par, d_row, nw)


def _merge_body(oa_ref, yb_ref, ga_ref, gb_ref, x_ref, wg_ref, ws_ref, wo_ref, o_ref):
    a = jnp.dot(oa_ref[...], wg_ref[...], preferred_element_type=F32)
    b = jnp.dot(yb_ref[...], ws_ref[...], preferred_element_type=F32)
    merged = _sigmoid(ga_ref[...]) * a + _sigmoid(gb_ref[...]) * b
    o_ref[...] = x_ref[...] + _dot(merged, wo_ref[...])


def _merge(o_a, y_b, proj, x, wg, ws, wo, *, tm):
    t = x.shape[0]
    tm = min(tm, t)
    assert t % tm == 0
    const = lambda i: (0, 0)
    return pl.pallas_call(
        _merge_body,
        grid=(t // tm,),
        in_specs=[pl.BlockSpec((tm, GDN_V), lambda i: (i, 0)),
                  pl.BlockSpec((tm, SSM_DINNER), lambda i: (i, 0)),
                  pl.BlockSpec((tm, D_MODEL), lambda i: (i, OFF_GA // D_MODEL)),
                  pl.BlockSpec((tm, D_MODEL), lambda i: (i, OFF_GB // D_MODEL)),
                  pl.BlockSpec((tm, D_MODEL), lambda i: (i, 0)),
                  pl.BlockSpec((GDN_V, D_MODEL), const),
                  pl.BlockSpec((SSM_DINNER, D_MODEL), const),
                  pl.BlockSpec((D_MODEL, D_MODEL), const)],
        out_specs=pl.BlockSpec((tm, D_MODEL), lambda i: (i, 0)),
        out_shape=jax.ShapeDtypeStruct((t, D_MODEL), F32),
        compiler_params=_params(1),
        name="merge_out",
    )(o_a, y_b, proj, proj, x, wg, ws, wo)


def _softmax_rows(s):
    e = jnp.exp(s - jnp.max(s, axis=-1, keepdims=True))
    return e / jnp.sum(e, axis=-1, keepdims=True)


def _xattn_prompt_body(x_ref, gx_ref, wq_ref, wo_ref, k_ref, v_ref, o_ref):
    x = x_ref[...]
    q = _dot(_rmsnorm(x, gx_ref[...]), wq_ref[...])
    outs = []
    for h in range(X_HEADS):
        sl = slice(h * X_HD, (h + 1) * X_HD)
        p = _softmax_rows(_dot_nt(q[:, sl], k_ref[:, sl]) * (X_HD ** -0.5))
        outs.append(_dot(p, v_ref[:, sl]))
    o_ref[...] = x + _dot(jnp.concatenate(outs, axis=1), wo_ref[...])


def _xattn_prompt(x, gx, wq, wo, mk, mv, seq, *, tm):
    t = x.shape[0]
    n_mem = mk.shape[0] // (t // seq)
    tm = min(tm, seq)
    assert seq % tm == 0
    per_seq = seq // tm
    const = lambda i: (0, 0)
    return pl.pallas_call(
        _xattn_prompt_body,
        grid=(t // tm,),
        in_specs=[pl.BlockSpec((tm, D_MODEL), lambda i: (i, 0)),
                  pl.BlockSpec((1, D_MODEL), const),
                  pl.BlockSpec((D_MODEL, D_MODEL), const),
                  pl.BlockSpec((D_MODEL, D_MODEL), const),
                  pl.BlockSpec((n_mem, D_MODEL), lambda i: (i // per_seq, 0)),
                  pl.BlockSpec((n_mem, D_MODEL), lambda i: (i // per_seq, 0))],
        out_specs=pl.BlockSpec((tm, D_MODEL), lambda i: (i, 0)),
        out_shape=jax.ShapeDtypeStruct((t, D_MODEL), F32),
        compiler_params=_params(1),
        name="xattn_prompt",
    )(x, gx.reshape(1, D_MODEL), wq, wo, mk, mv)


def _xattn_sample_body(x_ref, gx_ref, wq_ref, wo_ref, k_ref, v_ref, o_ref, q_scr, a_scr):
    i = pl.program_id(0)
    bt = k_ref.shape[0]

    @pl.when(i == 0)
    def _():
        q_scr[...] = _dot(_rmsnorm(x_ref[...], gx_ref[...]), wq_ref[...])

    rowid = lax.broadcasted_iota(jnp.int32, (8, D_MODEL), 0)
    own_head = rowid == lax.broadcasted_iota(jnp.int32, (8, D_MODEL), 1) // X_HD
    for tt in range(bt):
        row = i * bt + tt
        q_heads = jnp.where(own_head, q_scr[pl.ds(row, 1), :], 0.0)
        p = _softmax_rows(_dot_nt(q_heads, k_ref[tt]) * (X_HD ** -0.5))
        o_heads = _dot(p, v_ref[tt])
        a_scr[pl.ds(row, 1), :] = jnp.sum(jnp.where(own_head, o_heads, 0.0), axis=0, keepdims=True)

    @pl.when(i == pl.num_programs(0) - 1)
    def _():
        o_ref[...] = x_ref[...] + _dot(a_scr[...], wo_ref[...])


def _xattn_sample(x, gx, wq, wo, ck, cv):
    t = x.shape[0]
    n_mem = ck.shape[1]
    bt = XATTN_BT
    const = lambda i: (0, 0)
    return pl.pallas_call(
        _xattn_sample_body,
        grid=(t // bt,),
        in_specs=[pl.BlockSpec((t, D_MODEL), const),
                  pl.BlockSpec((1, D_MODEL), const),
                  pl.BlockSpec((D_MODEL, D_MODEL), const),
                  pl.BlockSpec((D_MODEL, D_MODEL), const),
                  pl.BlockSpec((bt, n_mem, D_MODEL), lambda i: (i, 0, 0)),
                  pl.BlockSpec((bt, n_mem, D_MODEL), lambda i: (i, 0, 0))],
        out_specs=pl.BlockSpec((t, D_MODEL), const),
        out_shape=jax.ShapeDtypeStruct((t, D_MODEL), F32),
        scratch_shapes=[pltpu.VMEM((t, D_MODEL), F32), pltpu.VMEM((t, D_MODEL), F32)],
        compiler_params=_params(1),
        name="xattn_sample",
    )(x, gx.reshape(1, D_MODEL), wq, wo, ck, cv)


def _mlp_body(x_ref, g_ref, wu_ref, wd_ref, gf_ref, y_ref, hn_ref, acc_ref, *, final_norm):
    j = pl.program_id(1)

    @pl.when(j == 0)
    def _():
        hn_ref[...] = _rmsnorm(x_ref[...], g_ref[...]).astype(BF16)
        acc_ref[...] = jnp.zeros_like(acc_ref)

    hf = jnp.dot(hn_ref[...], wu_ref[...], preferred_element_type=F32)
    acc_ref[...] += _dot(jnp.square(jnp.maximum(hf, 0.0)), wd_ref[...])

    @pl.when(j == pl.num_programs(1) - 1)
    def _():
        x_new = x_ref[...] + acc_ref[...]
        y_ref[...] = _rmsnorm(x_new, gf_ref[...]) if final_norm else x_new


def _mlp(x, g, wu, wd, gf, *, tm, tf, final_norm):
    t = x.shape[0]
    tm = min(tm, t)
    assert t % tm == 0 and D_FF % tf == 0
    const = lambda i, j: (0, 0)
    return pl.pallas_call(
        functools.partial(_mlp_body, final_norm=final_norm),
        grid=(t // tm, D_FF // tf),
        in_specs=[pl.BlockSpec((tm, D_MODEL), lambda i, j: (i, 0)),
                  pl.BlockSpec((1, D_MODEL), const),
                  pl.BlockSpec((D_MODEL, tf), lambda i, j: (0, j)),
                  pl.BlockSpec((tf, D_MODEL), lambda i, j: (j, 0)),
                  pl.BlockSpec((1, D_MODEL), const)],
        out_specs=pl.BlockSpec((tm, D_MODEL), lambda i, j: (i, 0)),
        out_shape=jax.ShapeDtypeStruct((t, D_MODEL), F32),
        scratch_shapes=[pltpu.VMEM((tm, D_MODEL), BF16), pltpu.VMEM((tm, D_MODEL), F32)],
        compiler_params=_params(2),
        name="mlp",
    )(x, g.reshape(1, D_MODEL), wu, wd, gf.reshape(1, D_MODEL))


def _lane_row(vec, lane0):
    return jnp.zeros((SMALL_COLS,), F32).at[lane0:lane0 + vec.shape[0]].set(vec.astype(F32))


def _param_rows(*rows):
    out = jnp.zeros((8, SMALL_COLS), F32)
    for i, r in enumerate(rows):
        out = out.at[i].set(r)
    return out


def kernel(x_prompt, x_sample, mem_prompt, state_gdn_conv, state_gdn, state_ssm_conv, state_ssm, cache_mem_k, cache_mem_v, g_mix, w_in, gdn_conv_w, gdn_A_log, gdn_dt_bias, gdn_norm_w, w_gdn_up, ssm_conv_w, ssm_conv_b, ssm_dt_bias, ssm_A_log, ssm_D, ssm_norm_w, w_ssm_up, w_out, g_mem, w_mk, w_mv, g_x, w_cq, w_co, g_ff, w_ff_up, w_ff_down, g_final):
    bp, seq, _ = x_prompt.shape
    bs, dec_seq, _ = x_sample.shape
    depth = w_in.shape[0]
    n_mem = mem_prompt.shape[1]
    assert seq % STEP == 0 and dec_seq == 1 and bs % SAMPLE_BT == 0 and bs % XATTN_BT == 0
    tp = bp * seq

    xp = x_prompt.reshape(tp, D_MODEL)
    xs = x_sample.reshape(bs, D_MODEL)
    mem = mem_prompt.reshape(bp * n_mem, D_MODEL)
    outs = {k: [] for k in ("p_gc", "p_g", "p_sc", "p_s", "p_mk", "p_mv", "s_gc", "s_g", "s_sc", "s_s")}

    starts = [0]
    for width in IN_SPLITS:
        starts.append(starts[-1] + width)
    col = lambda i: slice(starts[i], starts[i + 1])

    for l in range(depth):
        wl = w_in[l]
        w_main = jnp.concatenate([wl[:, col(0)], wl[:, col(1)], wl[:, col(4)], wl[:, col(5)],
                                  wl[:, col(7)], wl[:, col(8)]], axis=1).astype(BF16)
        w_small = jnp.concatenate(
            [wl[:, col(2)], wl[:, col(3)], wl[:, col(6)],
             jnp.zeros((D_MODEL, SMALL_COLS - 2 * GDN_HEADS - SSM_HEADS), F32)], axis=1).astype(BF16)
        gdn_par = _param_rows(_lane_row(gdn_A_log[l], LANE_GDEC), _lane_row(gdn_dt_bias[l], LANE_GDEC))
        ssm_par = _param_rows(_lane_row(ssm_dt_bias[l], LANE_DT), _lane_row(ssm_A_log[l], LANE_DT))
        gdn_nw = gdn_norm_w[l].reshape(1, GDN_DV).astype(F32)
        ssm_nw = ssm_norm_w[l].reshape(1, SSM_DINNER).astype(F32)
        d_row = jnp.repeat(ssm_D[l].astype(F32), SSM_P).reshape(1, SSM_DINNER)
        gcw = gdn_conv_w[l].astype(F32)
        scw = ssm_conv_w[l].astype(F32)
        scb = ssm_conv_b[l].reshape(1, SSM_CH).astype(F32)
        wg, ws, wo = w_gdn_up[l].astype(BF16), w_ssm_up[l].astype(BF16), w_out[l].astype(BF16)
        wq, wco = w_cq[l].astype(BF16), w_co[l].astype(BF16)
        wu, wd = w_ff_up[l].astype(BF16), w_ff_down[l].astype(BF16)
        last = l == depth - 1

        mk = _rms_mm(mem, g_mem[l], w_mk[l].astype(BF16), tm=1024, tn=1024)
        mv = _rms_mm(mem, g_mem[l], w_mv[l].astype(BF16), tm=1024, tn=1024)
        outs["p_mk"].append(mk.reshape(bp, n_mem, X_HEADS, X_HD))
        outs["p_mv"].append(mv.reshape(bp, n_mem, X_HEADS, X_HD))

        proj = _rms_mm(xp, g_mix[l], w_main, tm=1024, tn=1024)
        small = _rms_mm(xp, g_mix[l], w_small, tm=1024, tn=SMALL_COLS)
        o_a, p_g = _gdn_prompt(proj, small, gcw, gdn_par, gdn_nw, bp, seq)
        y_b, p_s = _ssd_prompt(proj, small, scw, scb, ssm_par, d_row, ssm_nw, bp, seq)
        proj3 = proj.reshape(bp, seq, MAIN_COLS)
        outs["p_gc"].append(proj3[:, seq - (CONV_W - 1):, OFF_QKV:OFF_QKV + GDN_CH])
        outs["p_sc"].append(proj3[:, seq - (CONV_W - 1):, OFF_XS:OFF_XS + SSM_CH])
        outs["p_g"].append(p_g)
        outs["p_s"].append(p_s)
        xp = _merge(o_a, y_b, proj, xp, wg, ws, wo, tm=512)
        xp = _xattn_prompt(xp, g_x[l], wq, wco, mk, mv, seq, tm=512)
        xp = _mlp(xp, g_ff[l], wu, wd, g_final, tm=1024, tf=512, final_norm=last)

        proj = _rms_mm(xs, g_mix[l], w_main, tm=128, tn=1024)
        small = _rms_mm(xs, g_mix[l], w_small, tm=128, tn=SMALL_COLS)
        gst = state_gdn_conv[l].astype(F32)
        sst = state_ssm_conv[l].astype(F32)
        o_a, s_g = _gdn_sample(proj, small, gst.reshape(bs, (CONV_W - 1) * GDN_CH), state_gdn[l].astype(F32),
                               gcw, gdn_par, gdn_nw)
        y_b, s_s = _ssd_sample(proj, small, sst.reshape(bs, (CONV_W - 1) * SSM_CH), state_ssm[l].astype(F32),
                               scw, scb, ssm_par, d_row, ssm_nw)
        outs["s_gc"].append(jnp.concatenate([gst[:, 1:], proj[:, None, OFF_QKV:OFF_QKV + GDN_CH]], axis=1))
        outs["s_sc"].append(jnp.concatenate([sst[:, 1:], proj[:, None, OFF_XS:OFF_XS + SSM_CH]], axis=1))
        outs["s_g"].append(s_g)
        outs["s_s"].append(s_s)
        xs = _merge(o_a, y_b, proj, xs, wg, ws, wo, tm=128)
        xs = _xattn_sample(xs, g_x[l], wq, wco, cache_mem_k[l].reshape(bs, n_mem, D_MODEL),
                           cache_mem_v[l].reshape(bs, n_mem, D_MODEL))
        xs = _mlp(xs, g_ff[l], wu, wd, g_final, tm=128, tf=512, final_norm=last)

    stack = lambda k: jnp.stack(outs[k])
    return (xp.reshape(bp, seq, D_MODEL), xs.reshape(bs, dec_seq, D_MODEL),
            stack("p_gc"), stack("p_g"), stack("p_sc"), stack("p_s"), stack("p_mk"), stack("p_mv"),
            stack("s_gc"), stack("s_g"), stack("s_sc"), stack("s_s"))
```

```python
import functools

import jax
import jax.numpy as jnp
from jax import lax
from jax.experimental import pallas as pl
from jax.experimental.pallas import tpu as pltpu

F32 = jnp.float32
BF16 = jnp.bfloat16

EPS = 1e-6
CHUNK = 64
CONV_W = 4
D_MODEL = 1024
GDN_HEADS = 8
GDN_DK = 128
GDN_DV = 128
GDN_QK = GDN_HEADS * GDN_DK
GDN_V = GDN_HEADS * GDN_DV
GDN_CH = 2 * GDN_QK + GDN_V
SSM_DINNER = 2 * D_MODEL
SSM_P = 64
SSM_HEADS = SSM_DINNER // SSM_P
SSM_GROUPS = 8
SSM_R = SSM_HEADS // SSM_GROUPS
SSM_N = 128
SSM_BC = SSM_GROUPS * SSM_N
SSM_CH = SSM_DINNER + 2 * SSM_BC
SSM_GW = SSM_R * SSM_P
X_HEADS = 4
X_HD = D_MODEL // X_HEADS
D_FF = 4 * D_MODEL
IN_SPLITS = (GDN_CH, GDN_V, GDN_HEADS, GDN_HEADS, SSM_DINNER, SSM_CH, SSM_HEADS, D_MODEL, D_MODEL)

MAIN_COLS = GDN_CH + GDN_V + SSM_DINNER + SSM_CH + 2 * D_MODEL
OFF_QKV, OFF_GATE, OFF_Z = 0, GDN_CH, GDN_CH + GDN_V
OFF_XS = OFF_Z + SSM_DINNER
OFF_B = OFF_XS + SSM_DINNER
OFF_C = OFF_B + SSM_BC
OFF_GA = OFF_C + SSM_BC
OFF_GB = OFF_GA + D_MODEL
SMALL_COLS = 128
LANE_BETA, LANE_GDEC, LANE_DT = 0, GDN_HEADS, 2 * GDN_HEADS

STEP = 2 * CHUNK
TAIL = 8
SAMPLE_BT = 8
XATTN_BT = 4
VMEM_LIMIT = 48 * 1024 * 1024


def _params(n_axes):
    return pltpu.CompilerParams(dimension_semantics=("arbitrary",) * n_axes, vmem_limit_bytes=VMEM_LIMIT)


def _sigmoid(x):
    return 1.0 / (1.0 + jnp.exp(-x))


def _silu(x):
    return x * _sigmoid(x)


def _softplus(x):
    return jnp.maximum(x, 0.0) + jnp.log1p(jnp.exp(-jnp.abs(x)))


def _rmsnorm(x, g):
    xf = x.astype(F32)
    return xf * lax.rsqrt(jnp.mean(xf * xf, axis=-1, keepdims=True) + EPS) * g


def _dot(a, b):
    return jnp.dot(a.astype(BF16), b.astype(BF16), preferred_element_type=F32)


def _dot_nt(a, b):
    return lax.dot_general(a.astype(BF16), b.astype(BF16), (((1,), (1,)), ((), ())), preferred_element_type=F32)


def _dot_tn(a, b):
    return lax.dot_general(a.astype(BF16), b.astype(BF16), (((0,), (0,)), ((), ())), preferred_element_type=F32)


def _split2(a):
    hi = a.astype(BF16)
    return hi, (a - hi.astype(F32)).astype(BF16)


def _dot3(a, b):
    ah, al = _split2(a)
    bh, bl = _split2(b)
    d = functools.partial(jnp.dot, preferred_element_type=F32)
    return d(ah, bh) + (d(ah, bl) + d(al, bh))


def _dot3_tn(a, b):
    ah, al = _split2(a)
    bh, bl = _split2(b)
    d = functools.partial(lax.dot_general, dimension_numbers=(((0,), (0,)), ((), ())), preferred_element_type=F32)
    return d(ah, bh) + (d(ah, bl) + d(al, bh))


def _chunk_cumsum(x):
    n = x.shape[0]
    r = lax.broadcasted_iota(jnp.int32, (n, n), 0)
    c = lax.broadcasted_iota(jnp.int32, (n, n), 1)
    tri = jnp.where((r >= c) & ((r // CHUNK) == (c // CHUNK)), 1.0, 0.0).astype(BF16)
    h1 = x.astype(BF16)
    r1 = x - h1.astype(F32)
    h2 = r1.astype(BF16)
    h3 = (r1 - h2.astype(F32)).astype(BF16)
    d = functools.partial(jnp.dot, preferred_element_type=F32)
    return d(tri, h1) + (d(tri, h2) + d(tri, h3))


def _chunk_masks():
    r = lax.broadcasted_iota(jnp.int32, (CHUNK, CHUNK), 0)
    c = lax.broadcasted_iota(jnp.int32, (CHUNK, CHUNK), 1)
    return r >= c, r > c, jnp.where(r == c, 1.0, 0.0).astype(F32)


def _decay_matrix(col, row, tril):
    return jnp.where(tril, jnp.exp(jnp.where(tril, col - row, 0.0)), 0.0)


def _dot3_shared(lhs_list, b):
    bh, bl = _split2(b)
    parts = [_split2(x) for x in lhs_list]
    his = [p[0] for p in parts]
    los = [p[1] for p in parts]
    n, m = len(lhs_list), lhs_list[0].shape[0]
    d = functools.partial(jnp.dot, preferred_element_type=F32)
    big = d(jnp.concatenate(his + los, axis=0), bh)
    small = d(jnp.concatenate(his, axis=0) if n > 1 else his[0], bl)
    return [big[i * m:(i + 1) * m] + (big[(n + i) * m:(n + i + 1) * m] + small[i * m:(i + 1) * m]) for i in range(n)]


def _inv_unit_lower_many(a_list, eye):
    size = a_list[0].shape[0]
    p = [_dot3_shared([-a], -a)[0] for a in a_list]
    t = [eye - a for a in a_list]
    terms = 2
    while terms < size:
        terms *= 2
        if terms < size:
            res = [_dot3_shared([pi, ti], pi) for pi, ti in zip(p, t)]
            t = [ti + r[1] for ti, r in zip(t, res)]
            p = [r[0] for r in res]
        else:
            t = [ti + _dot3_shared([ti], pi)[0] for pi, ti in zip(p, t)]
    return t


def _rms_mm_body(x_ref, g_ref, w_ref, o_ref, hn_ref):
    @pl.when(pl.program_id(1) == 0)
    def _():
        hn_ref[...] = _rmsnorm(x_ref[...], g_ref[...]).astype(BF16)

    o_ref[...] = jnp.dot(hn_ref[...], w_ref[...], preferred_element_type=F32).astype(o_ref.dtype)


def _rms_mm(x, g, w, *, tm, tn, out_dtype=F32):
    t, k = x.shape
    n = w.shape[1]
    tm, tn = min(tm, t), min(tn, n)
    assert t % tm == 0 and n % tn == 0
    return pl.pallas_call(
        _rms_mm_body,
        grid=(t // tm, n // tn),
        in_specs=[pl.BlockSpec((tm, k), lambda i, j: (i, 0)),
                  pl.BlockSpec((1, k), lambda i, j: (0, 0)),
                  pl.BlockSpec((k, tn), lambda i, j: (0, j))],
        out_specs=pl.BlockSpec((tm, tn), lambda i, j: (i, j)),
        out_shape=jax.ShapeDtypeStruct((t, n), out_dtype),
        scratch_shapes=[pltpu.VMEM((tm, k), BF16)],
        compiler_params=_params(2),
        name="rms_matmul",
    )(x, g.reshape(1, k), w)


def _gdn_prompt_body(qkv_ref, gate_ref, sm_ref, cw_ref, par_ref, nw_ref, o_ref, sfin_ref, xbuf, s_ref):
    step = pl.program_id(1)

    @pl.when(step == 0)
    def _():
        xbuf[0:TAIL, :] = jnp.zeros((TAIL, GDN_CH), F32)
        s_ref[...] = jnp.zeros_like(s_ref)

    xbuf[TAIL:TAIL + STEP, :] = qkv_ref[...]
    sm = sm_ref[...]
    beta_all = _sigmoid(sm)
    g_all = -jnp.exp(par_ref[0:1, :]) * _softplus(sm + par_ref[1:2, :])
    gc_all = _chunk_cumsum(g_all)
    gc_t = gc_all.T
    tril, strict, eye = _chunk_masks()
    nw = nw_ref[...]

    def conv(blk, r0):
        lo = blk * 128
        acc = cw_ref[CONV_W - 1:CONV_W, lo:lo + 128] * xbuf[pl.ds(TAIL + r0, CHUNK), lo:lo + 128]
        for j in range(CONV_W - 1):
            acc = acc + cw_ref[j:j + 1, lo:lo + 128] * xbuf[pl.ds(TAIL - (CONV_W - 1) + j + r0, CHUNK), lo:lo + 128]
        return _silu(acc)

    items = [(c, h) for c in range(STEP // CHUNK) for h in range(GDN_HEADS)]
    pre = []
    for c, h in items:
        r0 = c * CHUNK
        q = conv(h, r0)
        k = conv(GDN_HEADS + h, r0)
        v = conv(2 * GDN_HEADS + h, r0)
        q = q * lax.rsqrt(jnp.sum(q * q, axis=-1, keepdims=True) + EPS) * (GDN_DK ** -0.5)
        k = k * lax.rsqrt(jnp.sum(k * k, axis=-1, keepdims=True) + EPS)
        lg = LANE_GDEC + h
        bcol = beta_all[r0:r0 + CHUNK, LANE_BETA + h:LANE_BETA + h + 1]
        gcol = gc_all[r0:r0 + CHUNK, lg:lg + 1]
        grow = gc_t[lg:lg + 1, r0:r0 + CHUNK]
        glast = gc_all[r0 + CHUNK - 1:r0 + CHUNK, lg:lg + 1]
        decay = _decay_matrix(gcol, grow, tril)
        kb = k * bcol
        egc = jnp.exp(gcol)
        pre.append(dict(
            a=jnp.where(strict, _dot_nt(kb, k) * decay, 0.0),
            rhs=jnp.concatenate([v * bcol, kb * egc], axis=1),
            aqk=_dot_nt(q, k) * decay,
            qg=q * egc,
            kd=k * jnp.exp(glast - gcol),
            dch=jnp.exp(glast)))
    tinv = _inv_unit_lower_many([p["a"] for p in pre], eye)
    sols = [_dot3(t, p["rhs"]) for t, p in zip(tinv, pre)]

    outs = []
    states = [s_ref[h] for h in range(GDN_HEADS)]
    for c in range(STEP // CHUNK):
        idx = range(c * GDN_HEADS, (c + 1) * GDN_HEADS)
        wss = [_dot(jnp.concatenate([sols[i][:, GDN_DV:], pre[i]["qg"]], axis=0), states[i % GDN_HEADS]) for i in idx]
        v_news = [sols[i][:, :GDN_DV] - ws[:CHUNK] for i, ws in zip(idx, wss)]
        outs += [ws[CHUNK:] + _dot(pre[i]["aqk"], vn) for i, ws, vn in zip(idx, wss, v_news)]
        states = [states[i % GDN_HEADS] * pre[i]["dch"] + _dot_tn(pre[i]["kd"], vn) for i, vn in zip(idx, v_news)]
    for h in range(GDN_HEADS):
        s_ref[h] = states[h]
    for (c, h), o in zip(items, outs):
        r0 = c * CHUNK
        gt = gate_ref[r0:r0 + CHUNK, h * GDN_DV:(h + 1) * GDN_DV]
        o_ref[r0:r0 + CHUNK, h * GDN_DV:(h + 1) * GDN_DV] = (_rmsnorm(o, nw) * _silu(gt)).astype(o_ref.dtype)

    xbuf[0:TAIL, :] = xbuf[STEP:STEP + TAIL, :]

    @pl.when(step == pl.num_programs(1) - 1)
    def _():
        sfin_ref[0] = s_ref[...]


def _gdn_prompt(proj, small, cw, par, nw, bsz, seq):
    nsteps = seq // STEP
    row = lambda b, s: b * nsteps + s
    return pl.pallas_call(
        _gdn_prompt_body,
        grid=(bsz, nsteps),
        in_specs=[pl.BlockSpec((STEP, GDN_CH), lambda b, s: (row(b, s), OFF_QKV // GDN_CH)),
                  pl.BlockSpec((STEP, GDN_V), lambda b, s: (row(b, s), OFF_GATE // GDN_V)),
                  pl.BlockSpec((STEP, SMALL_COLS), lambda b, s: (row(b, s), 0)),
                  pl.BlockSpec((CONV_W, GDN_CH), lambda b, s: (0, 0)),
                  pl.BlockSpec((8, SMALL_COLS), lambda b, s: (0, 0)),
                  pl.BlockSpec((1, GDN_DV), lambda b, s: (0, 0))],
        out_specs=[pl.BlockSpec((STEP, GDN_V), lambda b, s: (row(b, s), 0)),
                   pl.BlockSpec((1, GDN_HEADS, GDN_DK, GDN_DV), lambda b, s: (b, 0, 0, 0))],
        out_shape=[jax.ShapeDtypeStruct((bsz * seq, GDN_V), BF16),
                   jax.ShapeDtypeStruct((bsz, GDN_HEADS, GDN_DK, GDN_DV), F32)],
        scratch_shapes=[pltpu.VMEM((TAIL + STEP, GDN_CH), F32),
                        pltpu.VMEM((GDN_HEADS, GDN_DK, GDN_DV), F32)],
        compiler_params=_params(2),
        name="gdn_prompt",
    )(proj, proj, small, cw, par, nw)


def _ssd_prompt_body(xs_ref, b_ref, c_ref, z_ref, sm_ref, cw_ref, cb_ref, par_ref, d_ref, nw_ref,
                     y_ref, sfin_ref, xb_x, xb_b, xb_c, st_ref):
    step = pl.program_id(1)

    @pl.when(step == 0)
    def _():
        xb_x[0:TAIL, :] = jnp.zeros((TAIL, SSM_DINNER), F32)
        xb_b[0:TAIL, :] = jnp.zeros((TAIL, SSM_BC), F32)
        xb_c[0:TAIL, :] = jnp.zeros((TAIL, SSM_BC), F32)
        st_ref[...] = jnp.zeros_like(st_ref)

    xb_x[TAIL:TAIL + STEP, :] = xs_ref[...]
    xb_b[TAIL:TAIL + STEP, :] = b_ref[...]
    xb_c[TAIL:TAIL + STEP, :] = c_ref[...]
    sm = sm_ref[...]
    dt_all = _softplus(sm + par_ref[0:1, :])
    cs_all = _chunk_cumsum(dt_all * (-jnp.exp(par_ref[1:2, :])))
    cs_t = cs_all.T
    tril, _, _ = _chunk_masks()

    def conv(buf, r0, lo, width, woff):
        wl = woff + lo
        acc = cw_ref[CONV_W - 1:CONV_W, wl:wl + width] * buf[pl.ds(TAIL + r0, CHUNK), lo:lo + width]
        acc = acc + cb_ref[0:1, wl:wl + width]
        for j in range(CONV_W - 1):
            acc = acc + cw_ref[j:j + 1, wl:wl + width] * buf[pl.ds(TAIL - (CONV_W - 1) + j + r0, CHUNK), lo:lo + width]
        return _silu(acc)

    assert SSM_P == CHUNK
    pw = 2 * SSM_P
    npair = SSM_R // 2
    items = [(c, g) for c in range(STEP // CHUNK) for g in range(SSM_GROUPS)]
    pairs = [(c, g, pr) for c, g in items for pr in range(npair)]
    lane_id = lax.broadcasted_iota(jnp.int32, (CHUNK, pw), 1)
    first = lane_id < SSM_P
    first_row = first[0:1]
    r2 = lax.broadcasted_iota(jnp.int32, (pw, pw), 0)
    c2 = lax.broadcasted_iota(jnp.int32, (pw, pw), 1)
    same_head = (r2 < SSM_P) == (c2 < SSM_P)
    tril_pair = lax.broadcasted_iota(jnp.int32, (CHUNK, pw), 0) >= (lane_id % SSM_P)

    bcs = [conv(xb_b, c * CHUNK, g * SSM_N, SSM_N, SSM_DINNER) for c, g in items]
    ccs = [conv(xb_c, c * CHUNK, g * SSM_N, SSM_N, SSM_DINNER + SSM_BC) for c, g in items]
    xss = [conv(xb_x, c * CHUNK, g * SSM_GW + pr * pw, pw, 0) for c, g, pr in pairs]
    cbs = [_dot_nt(cc, jnp.concatenate([bc, bc], axis=0)) for cc, bc in zip(ccs, bcs)]

    def pair_cols(src, c, g, pr):
        l0 = LANE_DT + SSM_R * g + 2 * pr
        r0 = c * CHUNK
        return jnp.where(first, src[r0:r0 + CHUNK, l0:l0 + 1], src[r0:r0 + CHUNK, l0 + 1:l0 + 2])

    def pair_row(row0, row1):
        return jnp.where(first_row, row0, row1)

    dts = [pair_cols(dt_all, *p) for p in pairs]
    cscs = [pair_cols(cs_all, *p) for p in pairs]
    csrs, csls = [], []
    for c, g, pr in pairs:
        l0 = LANE_DT + SSM_R * g + 2 * pr
        r0 = c * CHUNK
        t0 = cs_t[l0:l0 + 1, :]
        t1 = cs_t[l0 + 1:l0 + 2, :]
        if c == 0:
            csrs.append(pair_row(t0, pltpu.roll(t1, CHUNK, 1)))
        else:
            csrs.append(pair_row(pltpu.roll(t0, CHUNK, 1), t1))
        last = r0 + CHUNK - 1
        csls.append(pair_row(cs_all[last:last + 1, l0:l0 + 1], cs_all[last:last + 1, l0 + 1:l0 + 2]))
    xdts = [xs * dt for xs, dt in zip(xss, dts)]
    ms = [cbs[i // npair] * _decay_matrix(csc, csr, tril_pair) for i, (csc, csr) in enumerate(zip(cscs, csrs))]
    bds = [jnp.where(same_head, jnp.concatenate([x, x], axis=0), 0.0) for x in xdts]
    ylocs = []
    for (c, g, pr), m, bd, xs in zip(pairs, ms, bds, xss):
        lo = g * SSM_GW + pr * pw
        ylocs.append(_dot(m, bd) + d_ref[0:1, lo:lo + pw] * xs)
    ecss = [jnp.exp(csc) for csc in cscs]
    xds = [xdt * jnp.exp(csl - csc) for xdt, csl, csc in zip(xdts, csls, cscs)]
    dchs = [jnp.exp(csl) for csl in csls]

    def group_cat(vals, i):
        return jnp.concatenate(vals[i * npair:(i + 1) * npair], axis=1)

    outs = []
    for i, (c, g) in enumerate(items):
        st_g = st_ref[g]
        outs.append(group_cat(ylocs, i) + _dot(ccs[i], st_g) * group_cat(ecss, i))
        st_ref[g] = st_g * group_cat(dchs, i) + _dot_tn(bcs[i], group_cat(xds, i))
    for (c, g), y_g in zip(items, outs):
        r0 = c * CHUNK
        y_g = y_g * _silu(z_ref[r0:r0 + CHUNK, g * SSM_GW:(g + 1) * SSM_GW])
        y_ref[r0:r0 + CHUNK, g * SSM_GW:(g + 1) * SSM_GW] = _rmsnorm(
            y_g, nw_ref[0:1, g * SSM_GW:(g + 1) * SSM_GW]).astype(y_ref.dtype)

    xb_x[0:TAIL, :] = xb_x[STEP:STEP + TAIL, :]
    xb_b[0:TAIL, :] = xb_b[STEP:STEP + TAIL, :]
    xb_c[0:TAIL, :] = xb_c[STEP:STEP + TAIL, :]

    @pl.when(step == pl.num_programs(1) - 1)
    def _():
        for g in range(SSM_GROUPS):
            st_t = st_ref[g].T
            for rr in range(SSM_R):
                sfin_ref[0, SSM_R * g + rr] = st_t[rr * SSM_P:(rr + 1) * SSM_P, :]


def _ssd_prompt(proj, small, cw, cb, par, d_row, nw, bsz, seq):
    nsteps = seq // STEP
    row = lambda b, s: b * nsteps + s
    const = lambda b, s: (0, 0)
    return pl.pallas_call(
        _ssd_prompt_body,
        grid=(bsz, nsteps),
        in_specs=[pl.BlockSpec((STEP, SSM_DINNER), lambda b, s: (row(b, s), OFF_XS // SSM_DINNER)),
                  pl.BlockSpec((STEP, SSM_BC), lambda b, s: (row(b, s), OFF_B // SSM_BC)),
                  pl.BlockSpec((STEP, SSM_BC), lambda b, s: (row(b, s), OFF_C // SSM_BC)),
                  pl.BlockSpec((STEP, SSM_DINNER), lambda b, s: (row(b, s), OFF_Z // SSM_DINNER)),
                  pl.BlockSpec((STEP, SMALL_COLS), lambda b, s: (row(b, s), 0)),
                  pl.BlockSpec((CONV_W, SSM_CH), const),
                  pl.BlockSpec((1, SSM_CH), const),
                  pl.BlockSpec((8, SMALL_COLS), const),
                  pl.BlockSpec((1, SSM_DINNER), const),
                  pl.BlockSpec((1, SSM_DINNER), const)],
        out_specs=[pl.BlockSpec((STEP, SSM_DINNER), lambda b, s: (row(b, s), 0)),
                   pl.BlockSpec((1, SSM_HEADS, SSM_P, SSM_N), lambda b, s: (b, 0, 0, 0))],
        out_shape=[jax.ShapeDtypeStruct((bsz * seq, SSM_DINNER), BF16),
                   jax.ShapeDtypeStruct((bsz, SSM_HEADS, SSM_P, SSM_N), F32)],
        scratch_shapes=[pltpu.VMEM((TAIL + STEP, SSM_DINNER), F32),
                        pltpu.VMEM((TAIL + STEP, SSM_BC), F32),
                        pltpu.VMEM((TAIL + STEP, SSM_BC), F32),
                        pltpu.VMEM((SSM_GROUPS, SSM_N, SSM_GW), F32)],
        compiler_params=_params(2),
        name="ssd_prompt",
    )(proj, proj, proj, proj, small, cw, cb, par, d_row, nw)


def _gdn_sample_body(qkv_ref, gate_ref, sm_ref, cst_ref, s_ref, cw_ref, par_ref, nw_ref, o_ref, sout_ref):
    bt = qkv_ref.shape[0]
    sm = sm_ref[...]
    beta_all = _sigmoid(sm)
    eg_all = jnp.exp(-jnp.exp(par_ref[0:1, :]) * _softplus(sm + par_ref[1:2, :]))
    rowid = lax.broadcasted_iota(jnp.int32, (bt, GDN_DK), 0)
    nw = nw_ref[...]

    def conv(blk):
        lo = blk * 128
        acc = cw_ref[CONV_W - 1:CONV_W, lo:lo + 128] * qkv_ref[:, lo:lo + 128]
        for j in range(CONV_W - 1):
            acc = acc + cw_ref[j:j + 1, lo:lo + 128] * cst_ref[:, j * GDN_CH + lo:j * GDN_CH + lo + 128]
        return _silu(acc)

    for h in range(GDN_HEADS):
        q = conv(h)
        k = conv(GDN_HEADS + h)
        v = conv(2 * GDN_HEADS + h)
        q = q * lax.rsqrt(jnp.sum(q * q, axis=-1, keepdims=True) + EPS) * (GDN_DK ** -0.5)
        k = k * lax.rsqrt(jnp.sum(k * k, axis=-1, keepdims=True) + EPS)
        bcol = beta_all[:, LANE_BETA + h:LANE_BETA + h + 1]
        egcol = eg_all[:, LANE_GDEC + h:LANE_GDEC + h + 1]
        qs_rows, ks_rows = [], []
        for t in range(bt):
            lhs = jnp.where(rowid == 0, q[t:t + 1, :], jnp.where(rowid == 1, k[t:t + 1, :], 0.0))
            r = _dot(lhs, s_ref[t, h])
            qs_rows.append(r[0:1])
            ks_rows.append(r[1:2])
        q_s = jnp.concatenate(qs_rows, axis=0)
        k_s = jnp.concatenate(ks_rows, axis=0)
        v_new = bcol * v - (bcol * egcol) * k_s
        o = egcol * q_s + jnp.sum(q * k, axis=-1, keepdims=True) * v_new
        for t in range(bt):
            k_t = jnp.where(rowid == t, k, 0.0)
            sout_ref[t, h] = s_ref[t, h] * egcol[t:t + 1, :] + _dot3_tn(k_t, v_new)
        gt = gate_ref[:, h * GDN_DV:(h + 1) * GDN_DV]
        o_ref[:, h * GDN_DV:(h + 1) * GDN_DV] = (_rmsnorm(o, nw) * _silu(gt)).astype(o_ref.dtype)


def _gdn_sample(proj, small, cstate, state, layer, cw, par, nw):
    t = proj.shape[0]
    bt = SAMPLE_BT
    blk0 = layer * (t // bt)
    const = lambda i: (0, 0)
    return pl.pallas_call(
        _gdn_sample_body,
        grid=(t // bt,),
        in_specs=[pl.BlockSpec((bt, GDN_CH), lambda i: (i, OFF_QKV // GDN_CH)),
                  pl.BlockSpec((bt, GDN_V), lambda i: (i, OFF_GATE // GDN_V)),
                  pl.BlockSpec((bt, SMALL_COLS), lambda i: (i, 0)),
                  pl.BlockSpec((bt, (CONV_W - 1) * GDN_CH), lambda i: (blk0 + i, 0)),
                  pl.BlockSpec((bt, GDN_HEADS, GDN_DK, GDN_DV), lambda i: (blk0 + i, 0, 0, 0)),
                  pl.BlockSpec((CONV_W, GDN_CH), const),
                  pl.BlockSpec((8, SMALL_COLS), const),
                  pl.BlockSpec((1, GDN_DV), const)],
        out_specs=[pl.BlockSpec((bt, GDN_V), lambda i: (i, 0)),
                   pl.BlockSpec((bt, GDN_HEADS, GDN_DK, GDN_DV), lambda i: (i, 0, 0, 0))],
        out_shape=[jax.ShapeDtypeStruct((t, GDN_V), BF16),
                   jax.ShapeDtypeStruct((t,) + state.shape[1:], F32)],
        compiler_params=_params(1),
        name="gdn_sample",
    )(proj, proj, small, cstate, state, cw, par, nw)


def _ssd_sample_body(xs_ref, b_ref, c_ref, z_ref, sm_ref, cst_ref, s_ref, cw_ref, cb_ref, par_ref, d_ref, nw_ref,
                     y_ref, sout_ref):
    bt = xs_ref.shape[0]
    sm = sm_ref[...]
    dt_all = _softplus(sm + par_ref[0:1, :])
    e_all = jnp.exp(dt_all * (-jnp.exp(par_ref[1:2, :])))
    rowid_n = lax.broadcasted_iota(jnp.int32, (bt, SSM_N), 0)
    rowid_g = lax.broadcasted_iota(jnp.int32, (bt, SSM_GW), 0)

    def conv(x_ref, lo, width, woff):
        wl = woff + lo
        acc = cw_ref[CONV_W - 1:CONV_W, wl:wl + width] * x_ref[:, lo:lo + width] + cb_ref[0:1, wl:wl + width]
        for j in range(CONV_W - 1):
            acc = acc + cw_ref[j:j + 1, wl:wl + width] * cst_ref[:, j * SSM_CH + wl:j * SSM_CH + wl + width]
        return _silu(acc)

    for g in range(SSM_GROUPS):
        bc = conv(b_ref, g * SSM_N, SSM_N, SSM_DINNER)
        cc = conv(c_ref, g * SSM_N, SSM_N, SSM_DINNER + SSM_BC)
        xs_g = conv(xs_ref, g * SSM_GW, SSM_GW, 0)
        cb = jnp.sum(cc * bc, axis=-1, keepdims=True)
        yoff_rows = []
        for t in range(bt):
            c_t = jnp.where(rowid_n == 0, cc[t:t + 1, :], 0.0)
            s_tg = jnp.concatenate([s_ref[t, SSM_R * g + rr] for rr in range(SSM_R)], axis=0)
            yoff_rows.append(_dot_nt(c_t, s_tg)[0:1])
        yoff = jnp.concatenate(yoff_rows, axis=0)
        lanes = [LANE_DT + SSM_R * g + rr for rr in range(SSM_R)]
        dt_g = jnp.concatenate([jnp.broadcast_to(dt_all[:, l:l + 1], (bt, SSM_P)) for l in lanes], axis=1)
        e_g = jnp.concatenate([jnp.broadcast_to(e_all[:, l:l + 1], (bt, SSM_P)) for l in lanes], axis=1)
        xdt = xs_g * dt_g
        y = cb * xdt + yoff * e_g + d_ref[0:1, g * SSM_GW:(g + 1) * SSM_GW] * xs_g
        for t in range(bt):
            outer = _dot3_tn(jnp.where(rowid_g == t, xdt, 0.0), bc)
            for rr in range(SSM_R):
                h = SSM_R * g + rr
                sout_ref[t, h] = (s_ref[t, h] * e_all[t:t + 1, lanes[rr]:lanes[rr] + 1]
                                  + outer[rr * SSM_P:(rr + 1) * SSM_P, :])
        y = y * _silu(z_ref[:, g * SSM_GW:(g + 1) * SSM_GW])
        y_ref[:, g * SSM_GW:(g + 1) * SSM_GW] = _rmsnorm(
            y, nw_ref[0:1, g * SSM_GW:(g + 1) * SSM_GW]).astype(y_ref.dtype)


def _ssd_sample(proj, small, cstate, state, layer, cw, cb, par, d_row, nw):
    t = proj.shape[0]
    bt = SAMPLE_BT
    blk0 = layer * (t // bt)
    const = lambda i: (0, 0)
    return pl.pallas_call(
        _ssd_sample_body,
        grid=(t // bt,),
        in_specs=[pl.BlockSpec((bt, SSM_DINNER), lambda i: (i, OFF_XS // SSM_DINNER)),
                  pl.BlockSpec((bt, SSM_BC), lambda i: (i, OFF_B // SSM_BC)),
                  pl.BlockSpec((bt, SSM_BC), lambda i: (i, OFF_C // SSM_BC)),
                  pl.BlockSpec((bt, SSM_DINNER), lambda i: (i, OFF_Z // SSM_DINNER)),
                  pl.BlockSpec((bt, SMALL_COLS), lambda i: (i, 0)),
                  pl.BlockSpec((bt, (CONV_W - 1) * SSM_CH), lambda i: (blk0 + i, 0)),
                  pl.BlockSpec((bt, SSM_HEADS, SSM_P, SSM_N), lambda i: (blk0 + i, 0, 0, 0)),
                  pl.BlockSpec((CONV_W, SSM_CH), const),
                  pl.BlockSpec((1, SSM_CH), const),
                  pl.BlockSpec((8, SMALL_COLS), const),
                  pl.BlockSpec((1, SSM_DINNER), const),
                  pl.BlockSpec((1, SSM_DINNER), const)],
        out_specs=[pl.BlockSpec((bt, SSM_DINNER), lambda i: (i, 0)),
                   pl.BlockSpec((bt, SSM_HEADS, SSM_P, SSM_N), lambda i: (i, 0, 0, 0))],
        out_shape=[jax.ShapeDtypeStruct((t, SSM_DINNER), BF16),
                   jax.ShapeDtypeStruct((t,) + state.shape[1:], F32)],
        compiler_params=_params(1),
        name="ssd_sample",
    )(proj, proj, proj, proj, small, cstate, state, cw, cb, par, d_row, nw)


def _merge_body(oa_ref, yb_ref, ga_ref, gb_ref, x_ref, wg_ref, ws_ref, wo_ref, o_ref):
    a = jnp.dot(oa_ref[...], wg_ref[...], preferred_element_type=F32)
    b = jnp.dot(yb_ref[...], ws_ref[...], preferred_element_type=F32)
    merged = _sigmoid(ga_ref[...]) * a + _sigmoid(gb_ref[...]) * b
    o_ref[...] = x_ref[...] + _dot(merged, wo_ref[...])


def _merge(o_a, y_b, proj, x, wg, ws, wo, *, tm):
    t = x.shape[0]
    tm = min(tm, t)
    assert t % tm == 0
    const = lambda i: (0, 0)
    return pl.pallas_call(
        _merge_body,
        grid=(t // tm,),
        in_specs=[pl.BlockSpec((tm, GDN_V), lambda i: (i, 0)),
                  pl.BlockSpec((tm, SSM_DINNER), lambda i: (i, 0)),
                  pl.BlockSpec((tm, D_MODEL), lambda i: (i, OFF_GA // D_MODEL)),
                  pl.BlockSpec((tm, D_MODEL), lambda i: (i, OFF_GB // D_MODEL)),
                  pl.BlockSpec((tm, D_MODEL), lambda i: (i, 0)),
                  pl.BlockSpec((GDN_V, D_MODEL), const),
                  pl.BlockSpec((SSM_DINNER, D_MODEL), const),
                  pl.BlockSpec((D_MODEL, D_MODEL), const)],
        out_specs=pl.BlockSpec((tm, D_MODEL), lambda i: (i, 0)),
        out_shape=jax.ShapeDtypeStruct((t, D_MODEL), F32),
        compiler_params=_params(1),
        name="merge_out",
    )(o_a, y_b, proj, proj, x, wg, ws, wo)


def _softmax_rows(s):
    e = jnp.exp(s - jnp.max(s, axis=-1, keepdims=True))
    return e / jnp.sum(e, axis=-1, keepdims=True)


def _xattn_prompt_body(x_ref, gx_ref, wq_ref, wo_ref, k_ref, v_ref, o_ref):
    x = x_ref[...]
    q = _dot(_rmsnorm(x, gx_ref[...]), wq_ref[...])
    outs = []
    for h in range(X_HEADS):
        sl = slice(h * X_HD, (h + 1) * X_HD)
        p = _softmax_rows(_dot_nt(q[:, sl], k_ref[:, sl]) * (X_HD ** -0.5))
        outs.append(_dot(p, v_ref[:, sl]))
    o_ref[...] = x + _dot(jnp.concatenate(outs, axis=1), wo_ref[...])


def _xattn_prompt(x, gx, wq, wo, mk, mv, seq, *, tm):
    t = x.shape[0]
    n_mem = mk.shape[0] // (t // seq)
    tm = min(tm, seq)
    assert seq % tm == 0
    per_seq = seq // tm
    const = lambda i: (0, 0)
    return pl.pallas_call(
        _xattn_prompt_body,
        grid=(t // tm,),
        in_specs=[pl.BlockSpec((tm, D_MODEL), lambda i: (i, 0)),
                  pl.BlockSpec((1, D_MODEL), const),
                  pl.BlockSpec((D_MODEL, D_MODEL), const),
                  pl.BlockSpec((D_MODEL, D_MODEL), const),
                  pl.BlockSpec((n_mem, D_MODEL), lambda i: (i // per_seq, 0)),
                  pl.BlockSpec((n_mem, D_MODEL), lambda i: (i // per_seq, 0))],
        out_specs=pl.BlockSpec((tm, D_MODEL), lambda i: (i, 0)),
        out_shape=jax.ShapeDtypeStruct((t, D_MODEL), F32),
        compiler_params=_params(1),
        name="xattn_prompt",
    )(x, gx.reshape(1, D_MODEL), wq, wo, mk, mv)


def _xattn_sample_body(x_ref, gx_ref, wq_ref, wo_ref, k_ref, v_ref, o_ref, q_scr, a_scr):
    i = pl.program_id(0)
    bt = k_ref.shape[0]

    @pl.when(i == 0)
    def _():
        q_scr[...] = _dot(_rmsnorm(x_ref[...], gx_ref[...]), wq_ref[...])

    rowid = lax.broadcasted_iota(jnp.int32, (8, D_MODEL), 0)
    own_head = rowid == lax.broadcasted_iota(jnp.int32, (8, D_MODEL), 1) // X_HD
    for tt in range(bt):
        row = i * bt + tt
        q_heads = jnp.where(own_head, q_scr[pl.ds(row, 1), :], 0.0)
        p = _softmax_rows(_dot_nt(q_heads, k_ref[tt]) * (X_HD ** -0.5))
        o_heads = _dot(p, v_ref[tt])
        a_scr[pl.ds(row, 1), :] = jnp.sum(jnp.where(own_head, o_heads, 0.0), axis=0, keepdims=True)

    @pl.when(i == pl.num_programs(0) - 1)
    def _():
        o_ref[...] = x_ref[...] + _dot(a_scr[...], wo_ref[...])


def _xattn_sample(x, gx, wq, wo, ck, cv, layer):
    t = x.shape[0]
    n_mem = ck.shape[1]
    bt = XATTN_BT
    blk0 = layer * (t // bt)
    const = lambda i: (0, 0)
    return pl.pallas_call(
        _xattn_sample_body,
        grid=(t // bt,),
        in_specs=[pl.BlockSpec((t, D_MODEL), const),
                  pl.BlockSpec((1, D_MODEL), const),
                  pl.BlockSpec((D_MODEL, D_MODEL), const),
                  pl.BlockSpec((D_MODEL, D_MODEL), const),
                  pl.BlockSpec((bt, n_mem, D_MODEL), lambda i: (blk0 + i, 0, 0)),
                  pl.BlockSpec((bt, n_mem, D_MODEL), lambda i: (blk0 + i, 0, 0))],
        out_specs=pl.BlockSpec((t, D_MODEL), const),
        out_shape=jax.ShapeDtypeStruct((t, D_MODEL), F32),
        scratch_shapes=[pltpu.VMEM((t, D_MODEL), F32), pltpu.VMEM((t, D_MODEL), F32)],
        compiler_params=_params(1),
        name="xattn_sample",
    )(x, gx.reshape(1, D_MODEL), wq, wo, ck, cv)


def _mlp_body(x_ref, g_ref, wu_ref, wd_ref, gf_ref, y_ref, hn_ref, acc_ref, *, final_norm):
    j = pl.program_id(1)

    @pl.when(j == 0)
    def _():
        hn_ref[...] = _rmsnorm(x_ref[...], g_ref[...]).astype(BF16)
        acc_ref[...] = jnp.zeros_like(acc_ref)

    hf = jnp.dot(hn_ref[...], wu_ref[...], preferred_element_type=F32)
    acc_ref[...] += _dot(jnp.square(jnp.maximum(hf, 0.0)), wd_ref[...])

    @pl.when(j == pl.num_programs(1) - 1)
    def _():
        x_new = x_ref[...] + acc_ref[...]
        y_ref[...] = _rmsnorm(x_new, gf_ref[...]) if final_norm else x_new


def _mlp(x, g, wu, wd, gf, *, tm, tf, final_norm):
    t = x.shape[0]
    tm = min(tm, t)
    assert t % tm == 0 and D_FF % tf == 0
    const = lambda i, j: (0, 0)
    return pl.pallas_call(
        functools.partial(_mlp_body, final_norm=final_norm),
        grid=(t // tm, D_FF // tf),
        in_specs=[pl.BlockSpec((tm, D_MODEL), lambda i, j: (i, 0)),
                  pl.BlockSpec((1, D_MODEL), const),
                  pl.BlockSpec((D_MODEL, tf), lambda i, j: (0, j)),
                  pl.BlockSpec((tf, D_MODEL), lambda i, j: (j, 0)),
                  pl.BlockSpec((1, D_MODEL), const)],
        out_specs=pl.BlockSpec((tm, D_MODEL), lambda i, j: (i, 0)),
        out_shape=jax.ShapeDtypeStruct((t, D_MODEL), F32),
        scratch_shapes=[pltpu.VMEM((tm, D_MODEL), BF16), pltpu.VMEM((tm, D_MODEL), F32)],
        compiler_params=_params(2),
        name="mlp",
    )(x, g.reshape(1, D_MODEL), wu, wd, gf.reshape(1, D_MODEL))


def _lane_row(vec, lane0):
    return jnp.zeros((SMALL_COLS,), F32).at[lane0:lane0 + vec.shape[0]].set(vec.astype(F32))


def _param_rows(*rows):
    out = jnp.zeros((8, SMALL_COLS), F32)
    for i, r in enumerate(rows):
        out = out.at[i].set(r)
    return out


def kernel(x_prompt, x_sample, mem_prompt, state_gdn_conv, state_gdn, state_ssm_conv, state_ssm, cache_mem_k, cache_mem_v, g_mix, w_in, gdn_conv_w, gdn_A_log, gdn_dt_bias, gdn_norm_w, w_gdn_up, ssm_conv_w, ssm_conv_b, ssm_dt_bias, ssm_A_log, ssm_D, ssm_norm_w, w_ssm_up, w_out, g_mem, w_mk, w_mv, g_x, w_cq, w_co, g_ff, w_ff_up, w_ff_down, g_final):
    bp, seq, _ = x_prompt.shape
    bs, dec_seq, _ = x_sample.shape
    depth = w_in.shape[0]
    n_mem = mem_prompt.shape[1]
    assert seq % STEP == 0 and dec_seq == 1 and bs % SAMPLE_BT == 0 and bs % XATTN_BT == 0
    tp = bp * seq

    xp = x_prompt.reshape(tp, D_MODEL)
    xs = x_sample.reshape(bs, D_MODEL)
    mem = mem_prompt.reshape(bp * n_mem, D_MODEL)
    gst_all = state_gdn_conv.astype(F32).reshape(depth * bs, (CONV_W - 1) * GDN_CH)
    sst_all = state_ssm_conv.astype(F32).reshape(depth * bs, (CONV_W - 1) * SSM_CH)
    sg_all = state_gdn.astype(F32).reshape(depth * bs, GDN_HEADS, GDN_DK, GDN_DV)
    ss_all = state_ssm.astype(F32).reshape(depth * bs, SSM_HEADS, SSM_P, SSM_N)
    ck_all = cache_mem_k.astype(F32).reshape(depth * bs, n_mem, D_MODEL)
    cv_all = cache_mem_v.astype(F32).reshape(depth * bs, n_mem, D_MODEL)
    outs ={k: [] for k in ("p_gc", "p_g", "p_sc", "p_s", "p_mk", "p_mv", "s_gc", "s_g", "s_sc", "s_s")}

    starts = [0]
    for width in IN_SPLITS:
        starts.append(starts[-1] + width)
    col = lambda i: slice(starts[i], starts[i + 1])

    for l in range(depth):
        wl = w_in[l]
        w_main = jnp.concatenate([wl[:, col(0)], wl[:, col(1)], wl[:, col(4)], wl[:, col(5)],
                                  wl[:, col(7)], wl[:, col(8)]], axis=1).astype(BF16)
        w_small = jnp.concatenate(
            [wl[:, col(2)], wl[:, col(3)], wl[:, col(6)],
             jnp.zeros((D_MODEL, SMALL_COLS - 2 * GDN_HEADS - SSM_HEADS), F32)], axis=1).astype(BF16)
        gdn_par = _param_rows(_lane_row(gdn_A_log[l], LANE_GDEC), _lane_row(gdn_dt_bias[l], LANE_GDEC))
        ssm_par = _param_rows(_lane_row(ssm_dt_bias[l], LANE_DT), _lane_row(ssm_A_log[l], LANE_DT))
        gdn_nw = gdn_norm_w[l].reshape(1, GDN_DV).astype(F32)
        ssm_nw = ssm_norm_w[l].reshape(1, SSM_DINNER).astype(F32)
        d_row = jnp.repeat(ssm_D[l].astype(F32), SSM_P).reshape(1, SSM_DINNER)
        gcw = gdn_conv_w[l].astype(F32)
        scw = ssm_conv_w[l].astype(F32)
        scb = ssm_conv_b[l].reshape(1, SSM_CH).astype(F32)
        wg, ws, wo = w_gdn_up[l].astype(BF16), w_ssm_up[l].astype(BF16), w_out[l].astype(BF16)
        wq, wco = w_cq[l].astype(BF16), w_co[l].astype(BF16)
        wu, wd = w_ff_up[l].astype(BF16), w_ff_down[l].astype(BF16)
        last = l == depth - 1

        mk = _rms_mm(mem, g_mem[l], w_mk[l].astype(BF16), tm=1024, tn=1024)
        mv = _rms_mm(mem, g_mem[l], w_mv[l].astype(BF16), tm=1024, tn=1024)
        outs["p_mk"].append(mk.reshape(bp, n_mem, X_HEADS, X_HD))
        outs["p_mv"].append(mv.reshape(bp, n_mem, X_HEADS, X_HD))

        proj = _rms_mm(xp, g_mix[l], w_main, tm=1024, tn=1024)
        small = _rms_mm(xp, g_mix[l], w_small, tm=1024, tn=SMALL_COLS)
        o_a, p_g = _gdn_prompt(proj, small, gcw, gdn_par, gdn_nw, bp, seq)
        y_b, p_s = _ssd_prompt(proj, small, scw, scb, ssm_par, d_row, ssm_nw, bp, seq)
        proj3 = proj.reshape(bp, seq, MAIN_COLS)
        outs["p_gc"].append(proj3[:, seq - (CONV_W - 1):, OFF_QKV:OFF_QKV + GDN_CH])
        outs["p_sc"].append(proj3[:, seq - (CONV_W - 1):, OFF_XS:OFF_XS + SSM_CH])
        outs["p_g"].append(p_g)
        outs["p_s"].append(p_s)
        xp = _merge(o_a, y_b, proj, xp, wg, ws, wo, tm=512)
        xp = _xattn_prompt(xp, g_x[l], wq, wco, mk, mv, seq, tm=512)
        xp = _mlp(xp, g_ff[l], wu, wd, g_final, tm=1024, tf=512, final_norm=last)

        proj = _rms_mm(xs, g_mix[l], w_main, tm=128, tn=1024)
        small = _rms_mm(xs, g_mix[l], w_small, tm=128, tn=SMALL_COLS)
        o_a, s_g = _gdn_sample(proj, small, gst_all, sg_all, l, gcw, gdn_par, gdn_nw)
        y_b, s_s = _ssd_sample(proj, small, sst_all, ss_all, l, scw, scb, ssm_par, d_row, ssm_nw)
        outs["s_gc"].append(jnp.concatenate(
            [state_gdn_conv[l, :, 1:].astype(F32), proj[:, None, OFF_QKV:OFF_QKV + GDN_CH]], axis=1))
        outs["s_sc"].append(jnp.concatenate(
            [state_ssm_conv[l, :, 1:].astype(F32), proj[:, None, OFF_XS:OFF_XS + SSM_CH]], axis=1))
        outs["s_g"].append(s_g)
        outs["s_s"].append(s_s)
        xs = _merge(o_a, y_b, proj, xs, wg, ws, wo, tm=128)
        xs = _xattn_sample(xs, g_x[l], wq, wco, ck_all, cv_all, l)
        xs = _mlp(xs, g_ff[l], wu, wd, g_final, tm=128, tf=512, final_norm=last)

    stack = lambda k: jnp.stack(outs[k])
    return (xp.reshape(bp, seq, D_MODEL), xs.reshape(bs, dec_seq, D_MODEL),
            stack("p_gc"), stack("p_g"), stack("p_sc"), stack("p_s"), stack("p_mk"), stack("p_mv"),
            stack("s_gc"), stack("s_g"), stack("s_sc"), stack("s_s"))
```

```python
import functools

import jax
import jax.numpy as jnp
from jax import lax
from jax.experimental import pallas as pl
from jax.experimental.pallas import tpu as pltpu

F32 = jnp.float32
BF16 = jnp.bfloat16

EPS = 1e-6
CHUNK = 64
CONV_W = 4
D_MODEL = 1024
GDN_HEADS = 8
GDN_DK = 128
GDN_DV = 128
GDN_QK = GDN_HEADS * GDN_DK
GDN_V = GDN_HEADS * GDN_DV
GDN_CH = 2 * GDN_QK + GDN_V
SSM_DINNER = 2 * D_MODEL
SSM_P = 64
SSM_HEADS = SSM_DINNER // SSM_P
SSM_GROUPS = 8
SSM_R = SSM_HEADS // SSM_GROUPS
SSM_N = 128
SSM_BC = SSM_GROUPS * SSM_N
SSM_CH = SSM_DINNER + 2 * SSM_BC
SSM_GW = SSM_R * SSM_P
X_HEADS = 4
X_HD = D_MODEL // X_HEADS
D_FF = 4 * D_MODEL
IN_SPLITS = (GDN_CH, GDN_V, GDN_HEADS, GDN_HEADS, SSM_DINNER, SSM_CH, SSM_HEADS, D_MODEL, D_MODEL)

MAIN_COLS = GDN_CH + GDN_V + SSM_DINNER + SSM_CH + 2 * D_MODEL
OFF_QKV, OFF_GATE, OFF_Z = 0, GDN_CH, GDN_CH + GDN_V
OFF_XS = OFF_Z + SSM_DINNER
OFF_B = OFF_XS + SSM_DINNER
OFF_C = OFF_B + SSM_BC
OFF_GA = OFF_C + SSM_BC
OFF_GB = OFF_GA + D_MODEL
SMALL_COLS = 128
LANE_BETA, LANE_GDEC, LANE_DT = 0, GDN_HEADS, 2 * GDN_HEADS

STEP = 4 * CHUNK
GDN_SET = 2
TAIL = 8
SAMPLE_BT = 8
XATTN_BT = 4
VMEM_LIMIT = 54 * 1024 * 1024


def _params(n_axes):
    return pltpu.CompilerParams(dimension_semantics=("arbitrary",) * n_axes, vmem_limit_bytes=VMEM_LIMIT)


def _sigmoid(x):
    return 1.0 / (1.0 + jnp.exp(-x))


def _silu(x):
    return x * _sigmoid(x)


def _softplus(x):
    return jnp.maximum(x, 0.0) + jnp.log1p(jnp.exp(-jnp.abs(x)))


def _rmsnorm(x, g):
    xf = x.astype(F32)
    return xf * lax.rsqrt(jnp.mean(xf * xf, axis=-1, keepdims=True) + EPS) * g


def _dot(a, b):
    return jnp.dot(a.astype(BF16), b.astype(BF16), preferred_element_type=F32)


def _dot_nt(a, b):
    return lax.dot_general(a.astype(BF16), b.astype(BF16), (((1,), (1,)), ((), ())), preferred_element_type=F32)


def _dot_tn(a, b):
    return lax.dot_general(a.astype(BF16), b.astype(BF16), (((0,), (0,)), ((), ())), preferred_element_type=F32)


def _split2(a):
    hi = a.astype(BF16)
    return hi, (a - hi.astype(F32)).astype(BF16)


def _dot3_tn(a, b):
    ah, al = _split2(a)
    bh, bl = _split2(b)
    d = functools.partial(lax.dot_general, dimension_numbers=(((0,), (0,)), ((), ())), preferred_element_type=F32)
    return d(ah, bh) + (d(ah, bl) + d(al, bh))


def _chunk_cumsum(x):
    n = x.shape[0]
    r = lax.broadcasted_iota(jnp.int32, (n, n), 0)
    c = lax.broadcasted_iota(jnp.int32, (n, n), 1)
    tri = jnp.where((r >= c) & ((r // CHUNK) == (c // CHUNK)), 1.0, 0.0).astype(BF16)
    h1 = x.astype(BF16)
    r1 = x - h1.astype(F32)
    h2 = r1.astype(BF16)
    h3 = (r1 - h2.astype(F32)).astype(BF16)
    d = functools.partial(jnp.dot, preferred_element_type=F32)
    return d(tri, h1) + (d(tri, h2) + d(tri, h3))


def _chunk_masks():
    r = lax.broadcasted_iota(jnp.int32, (CHUNK, CHUNK), 0)
    c = lax.broadcasted_iota(jnp.int32, (CHUNK, CHUNK), 1)
    return r >= c, r > c, jnp.where(r == c, 1.0, 0.0).astype(F32)


def _decay_matrix(col, row, tril):
    return jnp.where(tril, jnp.exp(jnp.where(tril, col - row, 0.0)), 0.0)


INV_BASE = 8


def _mm_shared(lhs_list, b):
    m = lhs_list[0].shape[0]
    out = _dot(jnp.concatenate(lhs_list, axis=0), b)
    return [out[i * m:(i + 1) * m] for i in range(len(lhs_list))]


def _inv_unit_lower_levels(a_list, eye, out):
    size = a_list[0].shape[0]
    r = lax.broadcasted_iota(jnp.int32, (size, size), 0)
    c = lax.broadcasted_iota(jnp.int32, (size, size), 1)
    same = (r // INV_BASE) == (c // INV_BASE)
    n = [jnp.where(same, -a, 0.0) for a in a_list]
    t = [eye + ni for ni in n]
    p = [_dot(ni, ni) for ni in n]
    yield
    terms = 2
    while terms < INV_BASE:
        terms *= 2
        if terms < INV_BASE:
            res = [_mm_shared([pi, ti], pi) for pi, ti in zip(p, t)]
            t = [ti + ri[1] for ti, ri in zip(t, res)]
            p = [ri[0] for ri in res]
        else:
            t = [ti + _dot(ti, pi) for pi, ti in zip(p, t)]
        yield
    b = INV_BASE
    while b < size:
        low = ((r // (2 * b)) == (c // (2 * b))) & ((r // b) % 2 == 1) & ((c // b) % 2 == 0)
        y = [_dot(jnp.where(low, a, 0.0), ti) for a, ti in zip(a_list, t)]
        yield
        t = [ti - _dot(ti, yi) for ti, yi in zip(t, y)]
        yield
        b *= 2
    out.extend(t)


def _interleave(gen, thunks, gen_steps):
    thunks = list(thunks)
    per_step = -(-len(thunks) // gen_steps)
    for _ in gen:
        for th in thunks[:per_step]:
            th()
        thunks = thunks[per_step:]
    for th in thunks:
        th()


def _rms_mm_body(x_ref, g_ref, w_ref, o_ref, hn_ref):
    @pl.when(pl.program_id(1) == 0)
    def _():
        hn_ref[...] = _rmsnorm(x_ref[...], g_ref[...]).astype(BF16)

    o_ref[...] = jnp.dot(hn_ref[...], w_ref[...], preferred_element_type=F32).astype(o_ref.dtype)


def _rms_mm2_body(x_ref, g_ref, w_ref, w2_ref, o_ref, o2_ref, hn_ref):
    @pl.when(pl.program_id(1) == 0)
    def _():
        hn_ref[...] = _rmsnorm(x_ref[...], g_ref[...]).astype(BF16)
        o2_ref[...] = jnp.dot(hn_ref[...], w2_ref[...], preferred_element_type=F32)

    o_ref[...] = jnp.dot(hn_ref[...], w_ref[...], preferred_element_type=F32).astype(o_ref.dtype)


def _rms_mm(x, g, w, w2=None, *, tm, tn, out_dtype=F32):
    t, k = x.shape
    n = w.shape[1]
    tm, tn = min(tm, t), min(tn, n)
    assert t % tm == 0 and n % tn == 0
    in_specs = [pl.BlockSpec((tm, k), lambda i, j: (i, 0)),
                pl.BlockSpec((1, k), lambda i, j: (0, 0)),
                pl.BlockSpec((k, tn), lambda i, j: (0, j))]
    out_specs = [pl.BlockSpec((tm, tn), lambda i, j: (i, j))]
    out_shape = [jax.ShapeDtypeStruct((t, n), out_dtype)]
    args = [x, g.reshape(1, k), w]
    if w2 is not None:
        n2 = w2.shape[1]
        in_specs.append(pl.BlockSpec((k, n2), lambda i, j: (0, 0)))
        out_specs.append(pl.BlockSpec((tm, n2), lambda i, j: (i, 0)))
        out_shape.append(jax.ShapeDtypeStruct((t, n2), F32))
        args.append(w2)
    res = pl.pallas_call(
        _rms_mm_body if w2 is None else _rms_mm2_body,
        grid=(t // tm, n // tn),
        in_specs=in_specs,
        out_specs=out_specs,
        out_shape=out_shape,
        scratch_shapes=[pltpu.VMEM((tm, k), BF16)],
        compiler_params=_params(2),
        name="rms_matmul",
    )(*args)
    return res if w2 is not None else res[0]


def _gdn_prompt_body(qkv_ref, gate_ref, sm_ref, cw_ref, par_ref, nw_ref, o_ref, sfin_ref, xbuf, s_ref):
    step = pl.program_id(1)

    @pl.when(step == 0)
    def _():
        xbuf[0:TAIL, :] = jnp.zeros((TAIL, GDN_CH), F32)
        s_ref[...] = jnp.zeros_like(s_ref)

    xbuf[TAIL:TAIL + STEP, :] = qkv_ref[...]
    sm = sm_ref[...]
    beta_all = _sigmoid(sm)
    g_all = -jnp.exp(par_ref[0:1, :]) * _softplus(sm + par_ref[1:2, :])
    gc_all = _chunk_cumsum(g_all)
    gc_t = gc_all.T
    tril, strict, eye = _chunk_masks()
    nw = nw_ref[...]

    def conv(blk, r0):
        lo = blk * 128
        acc = cw_ref[CONV_W - 1:CONV_W, lo:lo + 128] * xbuf[pl.ds(TAIL + r0, CHUNK), lo:lo + 128]
        for j in range(CONV_W - 1):
            acc = acc + cw_ref[j:j + 1, lo:lo + 128] * xbuf[pl.ds(TAIL - (CONV_W - 1) + j + r0, CHUNK), lo:lo + 128]
        return _silu(acc)

    def pre_item(c, h):
        r0 = c * CHUNK
        q = conv(h, r0)
        k = conv(GDN_HEADS + h, r0)
        v = conv(2 * GDN_HEADS + h, r0)
        q = q * lax.rsqrt(jnp.sum(q * q, axis=-1, keepdims=True) + EPS) * (GDN_DK ** -0.5)
        k = k * lax.rsqrt(jnp.sum(k * k, axis=-1, keepdims=True) + EPS)
        lg = LANE_GDEC + h
        bcol = beta_all[r0:r0 + CHUNK, LANE_BETA + h:LANE_BETA + h + 1]
        gcol = gc_all[r0:r0 + CHUNK, lg:lg + 1]
        grow = gc_t[lg:lg + 1, r0:r0 + CHUNK]
        glast = gc_all[r0 + CHUNK - 1:r0 + CHUNK, lg:lg + 1]
        decay = _decay_matrix(gcol, grow, tril)
        kb = k * bcol
        egc = jnp.exp(gcol)
        return dict(
            a=jnp.where(strict, _dot_nt(kb, k) * decay, 0.0),
            rhs=jnp.concatenate([v * bcol, kb * egc], axis=1),
            aqk=_dot_nt(q, k) * decay,
            qg=q * egc,
            kd=k * jnp.exp(glast - gcol),
            dch=jnp.exp(glast))

    heads = range(GDN_HEADS)
    states = [s_ref[h] for h in heads]

    def recurrence_thunks(chunks, pre_s, sols):
        box = {}

        def round_ws(i):
            box["ws"] = [_dot(jnp.concatenate([sols[i][h][:, GDN_DV:], pre_s[i][h]["qg"]], axis=0), states[h])
                         for h in heads]

        def round_out(i):
            box["vn"] = [sols[i][h][:, :GDN_DV] - box["ws"][h][:CHUNK] for h in heads]
            box["o", i] = [box["ws"][h][CHUNK:] + _dot(pre_s[i][h]["aqk"], box["vn"][h]) for h in heads]

        def round_state(i):
            for h in heads:
                states[h] = states[h] * pre_s[i][h]["dch"] + _dot_tn(pre_s[i][h]["kd"], box["vn"][h])

        def store(i, h):
            r0 = chunks[i] * CHUNK
            gt = gate_ref[r0:r0 + CHUNK, h * GDN_DV:(h + 1) * GDN_DV]
            o_ref[r0:r0 + CHUNK, h * GDN_DV:(h + 1) * GDN_DV] = (
                _rmsnorm(box["o", i][h], nw) * _silu(gt)).astype(o_ref.dtype)

        ths = []
        for i in range(len(chunks)):
            ths += [functools.partial(f, i) for f in (round_ws, round_out, round_state)]
        return ths + [functools.partial(store, i, h) for i in range(len(chunks)) for h in heads]

    sets = [list(range(s, s + GDN_SET)) for s in range(0, STEP // CHUNK, GDN_SET)]
    doublings, merges = INV_BASE.bit_length() - 2, (CHUNK // INV_BASE).bit_length() - 1
    inv_steps = 1 + doublings + 2 * merges
    pre = {0: [[pre_item(c, h) for h in heads] for c in sets[0]]}
    pending = []
    for s, chunks in enumerate(sets):
        ahead = []
        if s + 1 < len(sets):
            pre[s + 1] = [[] for _ in sets[s + 1]]
            ahead = [functools.partial(lambda ss, i, c, h: pre[ss][i].append(pre_item(c, h)), s + 1, i, c, h)
                     for i, c in enumerate(sets[s + 1]) for h in heads]
        mixed = [th for pair in zip(pending, ahead) for th in pair]
        mixed += pending[len(ahead):] + ahead[len(pending):]
        tinv = []
        _interleave(_inv_unit_lower_levels([p["a"] for pc in pre[s] for p in pc], eye, tinv), mixed, inv_steps)
        sols = [[_dot(tinv[i * GDN_HEADS + h], pre[s][i][h]["rhs"]) for h in heads] for i in range(len(chunks))]
        pending = recurrence_thunks(chunks, pre[s], sols)
    for th in pending:
        th()
    for h in heads:
        s_ref[h] = states[h]

    xbuf[0:TAIL, :] = xbuf[STEP:STEP + TAIL, :]

    @pl.when(step == pl.num_programs(1) - 1)
    def _():
        sfin_ref[0] = s_ref[...]


def _gdn_prompt(proj, small, cw, par, nw, bsz, seq):
    nsteps = seq // STEP
    row = lambda b, s: b * nsteps + s
    return pl.pallas_call(
        _gdn_prompt_body,
        grid=(bsz, nsteps),
        in_specs=[pl.BlockSpec((STEP, GDN_CH), lambda b, s: (row(b, s), OFF_QKV // GDN_CH)),
                  pl.BlockSpec((STEP, GDN_V), lambda b, s: (row(b, s), OFF_GATE // GDN_V)),
                  pl.BlockSpec((STEP, SMALL_COLS), lambda b, s: (row(b, s), 0)),
                  pl.BlockSpec((CONV_W, GDN_CH), lambda b, s: (0, 0)),
                  pl.BlockSpec((8, SMALL_COLS), lambda b, s: (0, 0)),
                  pl.BlockSpec((1, GDN_DV), lambda b, s: (0, 0))],
        out_specs=[pl.BlockSpec((STEP, GDN_V), lambda b, s: (row(b, s), 0)),
                   pl.BlockSpec((1, GDN_HEADS, GDN_DK, GDN_DV), lambda b, s: (b, 0, 0, 0))],
        out_shape=[jax.ShapeDtypeStruct((bsz * seq, GDN_V), BF16),
                   jax.ShapeDtypeStruct((bsz, GDN_HEADS, GDN_DK, GDN_DV), F32)],
        scratch_shapes=[pltpu.VMEM((TAIL + STEP, GDN_CH), F32),
                        pltpu.VMEM((GDN_HEADS, GDN_DK, GDN_DV), F32)],
        compiler_params=_params(2),
        name="gdn_prompt",
    )(proj, proj, small, cw, par, nw)


def _ssd_prompt_body(xs_ref, b_ref, c_ref, z_ref, sm_ref, cw_ref, cb_ref, par_ref, d_ref, nw_ref,
                     y_ref, sfin_ref, xb_x, xb_b, xb_c, st_ref):
    step = pl.program_id(1)

    @pl.when(step == 0)
    def _():
        xb_x[0:TAIL, :] = jnp.zeros((TAIL, SSM_DINNER), F32)
        xb_b[0:TAIL, :] = jnp.zeros((TAIL, SSM_BC), F32)
        xb_c[0:TAIL, :] = jnp.zeros((TAIL, SSM_BC), F32)
        st_ref[...] = jnp.zeros_like(st_ref)

    xb_x[TAIL:TAIL + STEP, :] = xs_ref[...]
    xb_b[TAIL:TAIL + STEP, :] = b_ref[...]
    xb_c[TAIL:TAIL + STEP, :] = c_ref[...]
    sm = sm_ref[...]
    dt_all = _softplus(sm + par_ref[0:1, :])
    cs_all = _chunk_cumsum(dt_all * (-jnp.exp(par_ref[1:2, :])))
    cs_t = cs_all.T
    tril, _, _ = _chunk_masks()

    def conv(buf, r0, lo, width, woff):
        wl = woff + lo
        acc = cw_ref[CONV_W - 1:CONV_W, wl:wl + width] * buf[pl.ds(TAIL + r0, CHUNK), lo:lo + width]
        acc = acc + cb_ref[0:1, wl:wl + width]
        for j in range(CONV_W - 1):
            acc = acc + cw_ref[j:j + 1, wl:wl + width] * buf[pl.ds(TAIL - (CONV_W - 1) + j + r0, CHUNK), lo:lo + width]
        return _silu(acc)

    assert SSM_P == CHUNK
    pw = 2 * SSM_P
    npair = SSM_R // 2
    items = [(c, g) for c in range(STEP // CHUNK) for g in range(SSM_GROUPS)]
    pairs = [(c, g, pr) for c, g in items for pr in range(npair)]
    lane_id = lax.broadcasted_iota(jnp.int32, (CHUNK, pw), 1)
    first = lane_id < SSM_P
    first_row = first[0:1]
    r2 = lax.broadcasted_iota(jnp.int32, (pw, pw), 0)
    c2 = lax.broadcasted_iota(jnp.int32, (pw, pw), 1)
    same_head = (r2 < SSM_P) == (c2 < SSM_P)
    tril_pair = lax.broadcasted_iota(jnp.int32, (CHUNK, pw), 0) >= (lane_id % SSM_P)

    bcs = [conv(xb_b, c * CHUNK, g * SSM_N, SSM_N, SSM_DINNER) for c, g in items]
    ccs = [conv(xb_c, c * CHUNK, g * SSM_N, SSM_N, SSM_DINNER + SSM_BC) for c, g in items]
    xss = [conv(xb_x, c * CHUNK, g * SSM_GW + pr * pw, pw, 0) for c, g, pr in pairs]
    cbs = [_dot_nt(cc, jnp.concatenate([bc, bc], axis=0)) for cc, bc in zip(ccs, bcs)]

    def pair_cols(src, c, g, pr):
        l0 = LANE_DT + SSM_R * g + 2 * pr
        r0 = c * CHUNK
        return jnp.where(first, src[r0:r0 + CHUNK, l0:l0 + 1], src[r0:r0 + CHUNK, l0 + 1:l0 + 2])

    def pair_row(row0, row1):
        return jnp.where(first_row, row0, row1)

    dts = [pair_cols(dt_all, *p) for p in pairs]
    cscs = [pair_cols(cs_all, *p) for p in pairs]
    csrs, csls = [], []
    for c, g, pr in pairs:
        l0 = LANE_DT + SSM_R * g + 2 * pr
        r0 = c * CHUNK
        seg = (r0 // pw) * pw
        t0 = cs_t[l0:l0 + 1, seg:seg + pw]
        t1 = cs_t[l0 + 1:l0 + 2, seg:seg + pw]
        if r0 == seg:
            csrs.append(pair_row(t0, pltpu.roll(t1, CHUNK, 1)))
        else:
            csrs.append(pair_row(pltpu.roll(t0, CHUNK, 1), t1))
        last = r0 + CHUNK - 1
        csls.append(pair_row(cs_all[last:last + 1, l0:l0 + 1], cs_all[last:last + 1, l0 + 1:l0 + 2]))
    xdts = [xs * dt for xs, dt in zip(xss, dts)]
    ms = [cbs[i // npair] * _decay_matrix(csc, csr, tril_pair) for i, (csc, csr) in enumerate(zip(cscs, csrs))]
    bds = [jnp.where(same_head, jnp.concatenate([x, x], axis=0), 0.0) for x in xdts]
    ylocs = []
    for (c, g, pr), m, bd, xs in zip(pairs, ms, bds, xss):
        lo = g * SSM_GW + pr * pw
        ylocs.append(_dot(m, bd) + d_ref[0:1, lo:lo + pw] * xs)
    ecss = [jnp.exp(csc) for csc in cscs]
    xds = [xdt * jnp.exp(csl - csc) for xdt, csl, csc in zip(xdts, csls, cscs)]
    dchs = [jnp.exp(csl) for csl in csls]

    def group_cat(vals, i):
        return jnp.concatenate(vals[i * npair:(i + 1) * npair], axis=1)

    outs = []
    for i, (c, g) in enumerate(items):
        st_g = st_ref[g]
        outs.append(group_cat(ylocs, i) + _dot(ccs[i], st_g) * group_cat(ecss, i))
        st_ref[g] = st_g * group_cat(dchs, i) + _dot_tn(bcs[i], group_cat(xds, i))
    for (c, g), y_g in zip(items, outs):
        r0 = c * CHUNK
        y_g = y_g * _silu(z_ref[r0:r0 + CHUNK, g * SSM_GW:(g + 1) * SSM_GW])
        y_ref[r0:r0 + CHUNK, g * SSM_GW:(g + 1) * SSM_GW] = _rmsnorm(
            y_g, nw_ref[0:1, g * SSM_GW:(g + 1) * SSM_GW]).astype(y_ref.dtype)

    xb_x[0:TAIL, :] = xb_x[STEP:STEP + TAIL, :]
    xb_b[0:TAIL, :] = xb_b[STEP:STEP + TAIL, :]
    xb_c[0:TAIL, :] = xb_c[STEP:STEP + TAIL, :]

    @pl.when(step == pl.num_programs(1) - 1)
    def _():
        for g in range(SSM_GROUPS):
            st_t = st_ref[g].T
            for rr in range(SSM_R):
                sfin_ref[0, SSM_R * g + rr] = st_t[rr * SSM_P:(rr + 1) * SSM_P, :]


def _ssd_prompt(proj, small, cw, cb, par, d_row, nw, bsz, seq):
    nsteps = seq // STEP
    row = lambda b, s: b * nsteps + s
    const = lambda b, s: (0, 0)
    return pl.pallas_call(
        _ssd_prompt_body,
        grid=(bsz, nsteps),
        in_specs=[pl.BlockSpec((STEP, SSM_DINNER), lambda b, s: (row(b, s), OFF_XS // SSM_DINNER)),
                  pl.BlockSpec((STEP, SSM_BC), lambda b, s: (row(b, s), OFF_B // SSM_BC)),
                  pl.BlockSpec((STEP, SSM_BC), lambda b, s: (row(b, s), OFF_C // SSM_BC)),
                  pl.BlockSpec((STEP, SSM_DINNER), lambda b, s: (row(b, s), OFF_Z // SSM_DINNER)),
                  pl.BlockSpec((STEP, SMALL_COLS), lambda b, s: (row(b, s), 0)),
                  pl.BlockSpec((CONV_W, SSM_CH), const),
                  pl.BlockSpec((1, SSM_CH), const),
                  pl.BlockSpec((8, SMALL_COLS), const),
                  pl.BlockSpec((1, SSM_DINNER), const),
                  pl.BlockSpec((1, SSM_DINNER), const)],
        out_specs=[pl.BlockSpec((STEP, SSM_DINNER), lambda b, s: (row(b, s), 0)),
                   pl.BlockSpec((1, SSM_HEADS, SSM_P, SSM_N), lambda b, s: (b, 0, 0, 0))],
        out_shape=[jax.ShapeDtypeStruct((bsz * seq, SSM_DINNER), BF16),
                   jax.ShapeDtypeStruct((bsz, SSM_HEADS, SSM_P, SSM_N), F32)],
        scratch_shapes=[pltpu.VMEM((TAIL + STEP, SSM_DINNER), F32),
                        pltpu.VMEM((TAIL + STEP, SSM_BC), F32),
                        pltpu.VMEM((TAIL + STEP, SSM_BC), F32),
                        pltpu.VMEM((SSM_GROUPS, SSM_N, SSM_GW), F32)],
        compiler_params=_params(2),
        name="ssd_prompt",
    )(proj, proj, proj, proj, small, cw, cb, par, d_row, nw)


def _gdn_sample_body(qkv_ref, gate_ref, sm_ref, cst_ref, s_ref, cw_ref, par_ref, nw_ref, o_ref, sout_ref):
    bt = qkv_ref.shape[0]
    sm = sm_ref[...]
    beta_all = _sigmoid(sm)
    eg_all = jnp.exp(-jnp.exp(par_ref[0:1, :]) * _softplus(sm + par_ref[1:2, :]))
    rowid = lax.broadcasted_iota(jnp.int32, (bt, GDN_DK), 0)
    nw = nw_ref[...]

    def conv(blk):
        lo = blk * 128
        acc = cw_ref[CONV_W - 1:CONV_W, lo:lo + 128] * qkv_ref[:, lo:lo + 128]
        for j in range(CONV_W - 1):
            acc = acc + cw_ref[j:j + 1, lo:lo + 128] * cst_ref[:, j * GDN_CH + lo:j * GDN_CH + lo + 128]
        return _silu(acc)

    for h in range(GDN_HEADS):
        q = conv(h)
        k = conv(GDN_HEADS + h)
        v = conv(2 * GDN_HEADS + h)
        q = q * lax.rsqrt(jnp.sum(q * q, axis=-1, keepdims=True) + EPS) * (GDN_DK ** -0.5)
        k = k * lax.rsqrt(jnp.sum(k * k, axis=-1, keepdims=True) + EPS)
        bcol = beta_all[:, LANE_BETA + h:LANE_BETA + h + 1]
        egcol = eg_all[:, LANE_GDEC + h:LANE_GDEC + h + 1]
        qs_rows, ks_rows = [], []
        for t in range(bt):
            lhs = jnp.where(rowid == 0, q[t:t + 1, :], jnp.where(rowid == 1, k[t:t + 1, :], 0.0))
            r = _dot(lhs, s_ref[t, h])
            qs_rows.append(r[0:1])
            ks_rows.append(r[1:2])
        q_s = jnp.concatenate(qs_rows, axis=0)
        k_s = jnp.concatenate(ks_rows, axis=0)
        v_new = bcol * v - (bcol * egcol) * k_s
        o = egcol * q_s + jnp.sum(q * k, axis=-1, keepdims=True) * v_new
        for t in range(bt):
            k_t = jnp.where(rowid == t, k, 0.0)
            sout_ref[t, h] = s_ref[t, h] * egcol[t:t + 1, :] + _dot3_tn(k_t, v_new)
        gt = gate_ref[:, h * GDN_DV:(h + 1) * GDN_DV]
        o_ref[:, h * GDN_DV:(h + 1) * GDN_DV] = (_rmsnorm(o, nw) * _silu(gt)).astype(o_ref.dtype)


def _gdn_sample(proj, small, cstate, state, layer, cw, par, nw):
    t = proj.shape[0]
    bt = SAMPLE_BT
    blk0 = layer * (t // bt)
    const = lambda i: (0, 0)
    return pl.pallas_call(
        _gdn_sample_body,
        grid=(t // bt,),
        in_specs=[pl.BlockSpec((bt, GDN_CH), lambda i: (i, OFF_QKV // GDN_CH)),
                  pl.BlockSpec((bt, GDN_V), lambda i: (i, OFF_GATE // GDN_V)),
                  pl.BlockSpec((bt, SMALL_COLS), lambda i: (i, 0)),
                  pl.BlockSpec((bt, (CONV_W - 1) * GDN_CH), lambda i: (blk0 + i, 0)),
                  pl.BlockSpec((bt, GDN_HEADS, GDN_DK, GDN_DV), lambda i: (blk0 + i, 0, 0, 0)),
                  pl.BlockSpec((CONV_W, GDN_CH), const),
                  pl.BlockSpec((8, SMALL_COLS), const),
                  pl.BlockSpec((1, GDN_DV), const)],
        out_specs=[pl.BlockSpec((bt, GDN_V), lambda i: (i, 0)),
                   pl.BlockSpec((bt, GDN_HEADS, GDN_DK, GDN_DV), lambda i: (i, 0, 0, 0))],
        out_shape=[jax.ShapeDtypeStruct((t, GDN_V), BF16),
                   jax.ShapeDtypeStruct((t,) + state.shape[1:], F32)],
        compiler_params=_params(1),
        name="gdn_sample",
    )(proj, proj, small, cstate, state, cw, par, nw)


def _ssd_sample_body(xs_ref, b_ref, c_ref, z_ref, sm_ref, cst_ref, s_ref, cw_ref, cb_ref, par_ref, d_ref, nw_ref,
                     y_ref, sout_ref):
    bt = xs_ref.shape[0]
    sm = sm_ref[...]
    dt_all = _softplus(sm + par_ref[0:1, :])
    e_all = jnp.exp(dt_all * (-jnp.exp(par_ref[1:2, :])))
    rowid_n = lax.broadcasted_iota(jnp.int32, (bt, SSM_N), 0)
    rowid_g = lax.broadcasted_iota(jnp.int32, (bt, SSM_GW), 0)

    def conv(x_ref, lo, width, woff):
        wl = woff + lo
        acc = cw_ref[CONV_W - 1:CONV_W, wl:wl + width] * x_ref[:, lo:lo + width] + cb_ref[0:1, wl:wl + width]
        for j in range(CONV_W - 1):
            acc = acc + cw_ref[j:j + 1, wl:wl + width] * cst_ref[:, j * SSM_CH + wl:j * SSM_CH + wl + width]
        return _silu(acc)

    for g in range(SSM_GROUPS):
        bc = conv(b_ref, g * SSM_N, SSM_N, SSM_DINNER)
        cc = conv(c_ref, g * SSM_N, SSM_N, SSM_DINNER + SSM_BC)
        xs_g = conv(xs_ref, g * SSM_GW, SSM_GW, 0)
        cb = jnp.sum(cc * bc, axis=-1, keepdims=True)
        yoff_rows = []
        for t in range(bt):
            c_t = jnp.where(rowid_n == 0, cc[t:t + 1, :], 0.0)
            s_tg = jnp.concatenate([s_ref[t, SSM_R * g + rr] for rr in range(SSM_R)], axis=0)
            yoff_rows.append(_dot_nt(c_t, s_tg)[0:1])
        yoff = jnp.concatenate(yoff_rows, axis=0)
        lanes = [LANE_DT + SSM_R * g + rr for rr in range(SSM_R)]
        dt_g = jnp.concatenate([jnp.broadcast_to(dt_all[:, l:l + 1], (bt, SSM_P)) for l in lanes], axis=1)
        e_g = jnp.concatenate([jnp.broadcast_to(e_all[:, l:l + 1], (bt, SSM_P)) for l in lanes], axis=1)
        xdt = xs_g * dt_g
        y = cb * xdt + yoff * e_g + d_ref[0:1, g * SSM_GW:(g + 1) * SSM_GW] * xs_g
        for t in range(bt):
            outer = _dot3_tn(jnp.where(rowid_g == t, xdt, 0.0), bc)
            for rr in range(SSM_R):
                h = SSM_R * g + rr
                sout_ref[t, h] = (s_ref[t, h] * e_all[t:t + 1, lanes[rr]:lanes[rr] + 1]
                                  + outer[rr * SSM_P:(rr + 1) * SSM_P, :])
        y = y * _silu(z_ref[:, g * SSM_GW:(g + 1) * SSM_GW])
        y_ref[:, g * SSM_GW:(g + 1) * SSM_GW] = _rmsnorm(
            y, nw_ref[0:1, g * SSM_GW:(g + 1) * SSM_GW]).astype(y_ref.dtype)


def _ssd_sample(proj, small, cstate, state, layer, cw, cb, par, d_row, nw):
    t = proj.shape[0]
    bt = SAMPLE_BT
    blk0 = layer * (t // bt)
    const = lambda i: (0, 0)
    return pl.pallas_call(
        _ssd_sample_body,
        grid=(t // bt,),
        in_specs=[pl.BlockSpec((bt, SSM_DINNER), lambda i: (i, OFF_XS // SSM_DINNER)),
                  pl.BlockSpec((bt, SSM_BC), lambda i: (i, OFF_B // SSM_BC)),
                  pl.BlockSpec((bt, SSM_BC), lambda i: (i, OFF_C // SSM_BC)),
                  pl.BlockSpec((bt, SSM_DINNER), lambda i: (i, OFF_Z // SSM_DINNER)),
                  pl.BlockSpec((bt, SMALL_COLS), lambda i: (i, 0)),
                  pl.BlockSpec((bt, (CONV_W - 1) * SSM_CH), lambda i: (blk0 + i, 0)),
                  pl.BlockSpec((bt, SSM_HEADS, SSM_P, SSM_N), lambda i: (blk0 + i, 0, 0, 0)),
                  pl.BlockSpec((CONV_W, SSM_CH), const),
                  pl.BlockSpec((1, SSM_CH), const),
                  pl.BlockSpec((8, SMALL_COLS), const),
                  pl.BlockSpec((1, SSM_DINNER), const),
                  pl.BlockSpec((1, SSM_DINNER), const)],
        out_specs=[pl.BlockSpec((bt, SSM_DINNER), lambda i: (i, 0)),
                   pl.BlockSpec((bt, SSM_HEADS, SSM_P, SSM_N), lambda i: (i, 0, 0, 0))],
        out_shape=[jax.ShapeDtypeStruct((t, SSM_DINNER), BF16),
                   jax.ShapeDtypeStruct((t,) + state.shape[1:], F32)],
        compiler_params=_params(1),
        name="ssd_sample",
    )(proj, proj, proj, proj, small, cstate, state, cw, cb, par, d_row, nw)


def _merge_body(oa_ref, yb_ref, ga_ref, gb_ref, x_ref, wg_ref, ws_ref, wo_ref, o_ref):
    a = jnp.dot(oa_ref[...], wg_ref[...], preferred_element_type=F32)
    b = jnp.dot(yb_ref[...], ws_ref[...], preferred_element_type=F32)
    merged = _sigmoid(ga_ref[...]) * a + _sigmoid(gb_ref[...]) * b
    o_ref[...] = x_ref[...] + _dot(merged, wo_ref[...])


def _merge(o_a, y_b, proj, x, wg, ws, wo, *, tm):
    t = x.shape[0]
    tm = min(tm, t)
    assert t % tm == 0
    const = lambda i: (0, 0)
    return pl.pallas_call(
        _merge_body,
        grid=(t // tm,),
        in_specs=[pl.BlockSpec((tm, GDN_V), lambda i: (i, 0)),
                  pl.BlockSpec((tm, SSM_DINNER), lambda i: (i, 0)),
                  pl.BlockSpec((tm, D_MODEL), lambda i: (i, OFF_GA // D_MODEL)),
                  pl.BlockSpec((tm, D_MODEL), lambda i: (i, OFF_GB // D_MODEL)),
                  pl.BlockSpec((tm, D_MODEL), lambda i: (i, 0)),
                  pl.BlockSpec((GDN_V, D_MODEL), const),
                  pl.BlockSpec((SSM_DINNER, D_MODEL), const),
                  pl.BlockSpec((D_MODEL, D_MODEL), const)],
        out_specs=pl.BlockSpec((tm, D_MODEL), lambda i: (i, 0)),
        out_shape=jax.ShapeDtypeStruct((t, D_MODEL), F32),
        compiler_params=_params(1),
        name="merge_out",
    )(o_a, y_b, proj, proj, x, wg, ws, wo)


def _softmax_rows(s):
    e = jnp.exp(s - jnp.max(s, axis=-1, keepdims=True))
    return e / jnp.sum(e, axis=-1, keepdims=True)


def _xattn_prompt_body(x_ref, gx_ref, wq_ref, wo_ref, k_ref, v_ref, o_ref):
    x = x_ref[...]
    q = _dot(_rmsnorm(x, gx_ref[...]), wq_ref[...])
    outs = []
    for h in range(X_HEADS):
        sl = slice(h * X_HD, (h + 1) * X_HD)
        p = _softmax_rows(_dot_nt(q[:, sl], k_ref[:, sl]) * (X_HD ** -0.5))
        outs.append(_dot(p, v_ref[:, sl]))
    o_ref[...] = x + _dot(jnp.concatenate(outs, axis=1), wo_ref[...])


def _xattn_prompt(x, gx, wq, wo, mk, mv, seq, *, tm):
    t = x.shape[0]
    n_mem = mk.shape[0] // (t // seq)
    tm = min(tm, seq)
    assert seq % tm == 0
    per_seq = seq // tm
    const = lambda i: (0, 0)
    return pl.pallas_call(
        _xattn_prompt_body,
        grid=(t // tm,),
        in_specs=[pl.BlockSpec((tm, D_MODEL), lambda i: (i, 0)),
                  pl.BlockSpec((1, D_MODEL), const),
                  pl.BlockSpec((D_MODEL, D_MODEL), const),
                  pl.BlockSpec((D_MODEL, D_MODEL), const),
                  pl.BlockSpec((n_mem, D_MODEL), lambda i: (i // per_seq, 0)),
                  pl.BlockSpec((n_mem, D_MODEL), lambda i: (i // per_seq, 0))],
        out_specs=pl.BlockSpec((tm, D_MODEL), lambda i: (i, 0)),
        out_shape=jax.ShapeDtypeStruct((t, D_MODEL), F32),
        compiler_params=_params(1),
        name="xattn_prompt",
    )(x, gx.reshape(1, D_MODEL), wq, wo, mk, mv)


def _xattn_sample_body(x_ref, gx_ref, wq_ref, wo_ref, k_ref, v_ref, o_ref, q_scr, a_scr):
    i = pl.program_id(0)
    bt = k_ref.shape[0]

    @pl.when(i == 0)
    def _():
        q_scr[...] = _dot(_rmsnorm(x_ref[...], gx_ref[...]), wq_ref[...])

    nrow = k_ref.shape[1]
    sub = lax.broadcasted_iota(jnp.int32, (8, nrow), 0)
    lane = lax.broadcasted_iota(jnp.int32, (8, nrow), 1)
    own = (lane % 8) == sub
    lower = sub < X_HEADS
    for tt in range(bt):
        row = i * bt + tt
        q_t = q_scr[pl.ds(row, 1), :]
        q_rows = jnp.concatenate(
            [q_t[:, h * X_HD + half * 128:h * X_HD + (half + 1) * 128]
             for half in range(X_HD // 128) for h in range(X_HEADS)], axis=0)
        e = jnp.where(own, _dot_nt(q_rows, k_ref[tt]), 0.0)
        s = (e + pltpu.roll(pltpu.roll(e, X_HEADS, 0), nrow - X_HEADS, 1)) * (X_HD ** -0.5)
        s = jnp.where(own, s, -jnp.inf)
        p = jnp.exp(s - jnp.max(s, axis=-1, keepdims=True))
        p = jnp.where(lower, p / jnp.sum(p, axis=-1, keepdims=True), 0.0)
        p2 = p + pltpu.roll(pltpu.roll(p, X_HEADS, 0), X_HEADS, 1)
        o2 = _dot(p2, v_ref[tt])
        a_scr[pl.ds(row, 1), :] = jnp.concatenate(
            [o2[half * X_HEADS + h:half * X_HEADS + h + 1, :]
             for h in range(X_HEADS) for half in range(X_HD // 128)], axis=1)

    @pl.when(i == pl.num_programs(0) - 1)
    def _():
        o_ref[...] = x_ref[...] + _dot(a_scr[...], wo_ref[...])


def _xattn_sample(x, gx, wq, wo, ck, cv, layer):
    t = x.shape[0]
    n_mem = ck.shape[1]
    bt = XATTN_BT
    blk0 = layer * (t // bt)
    const = lambda i: (0, 0)
    return pl.pallas_call(
        _xattn_sample_body,
        grid=(t // bt,),
        in_specs=[pl.BlockSpec((t, D_MODEL), const),
                  pl.BlockSpec((1, D_MODEL), const),
                  pl.BlockSpec((D_MODEL, D_MODEL), const),
                  pl.BlockSpec((D_MODEL, D_MODEL), const),
                  pl.BlockSpec((bt,) + ck.shape[1:], lambda i: (blk0 + i, 0, 0)),
                  pl.BlockSpec((bt,) + cv.shape[1:], lambda i: (blk0 + i, 0, 0))],
        out_specs=pl.BlockSpec((t, D_MODEL), const),
        out_shape=jax.ShapeDtypeStruct((t, D_MODEL), F32),
        scratch_shapes=[pltpu.VMEM((t, D_MODEL), F32), pltpu.VMEM((t, D_MODEL), F32)],
        compiler_params=_params(1),
        name="xattn_sample",
    )(x, gx.reshape(1, D_MODEL), wq, wo, ck, cv)


def _mlp_body(x_ref, g_ref, wu_ref, wd_ref, gf_ref, y_ref, hn_ref, acc_ref, *, final_norm):
    j = pl.program_id(1)

    @pl.when(j == 0)
    def _():
        hn_ref[...] = _rmsnorm(x_ref[...], g_ref[...]).astype(BF16)
        acc_ref[...] = jnp.zeros_like(acc_ref)

    hf = jnp.dot(hn_ref[...], wu_ref[...], preferred_element_type=F32)
    acc_ref[...] += _dot(jnp.square(jnp.maximum(hf, 0.0)), wd_ref[...])

    @pl.when(j == pl.num_programs(1) - 1)
    def _():
        x_new = x_ref[...] + acc_ref[...]
        y_ref[...] = _rmsnorm(x_new, gf_ref[...]) if final_norm else x_new


def _mlp(x, g, wu, wd, gf, *, tm, tf, final_norm):
    t = x.shape[0]
    tm = min(tm, t)
    assert t % tm == 0 and D_FF % tf == 0
    const = lambda i, j: (0, 0)
    return pl.pallas_call(
        functools.partial(_mlp_body, final_norm=final_norm),
        grid=(t // tm, D_FF // tf),
        in_specs=[pl.BlockSpec((tm, D_MODEL), lambda i, j: (i, 0)),
                  pl.BlockSpec((1, D_MODEL), const),
                  pl.BlockSpec((D_MODEL, tf), lambda i, j: (0, j)),
                  pl.BlockSpec((tf, D_MODEL), lambda i, j: (j, 0)),
                  pl.BlockSpec((1, D_MODEL), const)],
        out_specs=pl.BlockSpec((tm, D_MODEL), lambda i, j: (i, 0)),
        out_shape=jax.ShapeDtypeStruct((t, D_MODEL), F32),
        scratch_shapes=[pltpu.VMEM((tm, D_MODEL), BF16), pltpu.VMEM((tm, D_MODEL), F32)],
        compiler_params=_params(2),
        name="mlp",
    )(x, g.reshape(1, D_MODEL), wu, wd, gf.reshape(1, D_MODEL))


def _lane_row(vec, lane0):
    return jnp.zeros((SMALL_COLS,), F32).at[lane0:lane0 + vec.shape[0]].set(vec.astype(F32))


def _param_rows(*rows):
    out = jnp.zeros((8, SMALL_COLS), F32)
    for i, r in enumerate(rows):
        out = out.at[i].set(r)
    return out


def kernel(x_prompt, x_sample, mem_prompt, state_gdn_conv, state_gdn, state_ssm_conv, state_ssm, cache_mem_k, cache_mem_v, g_mix, w_in, gdn_conv_w, gdn_A_log, gdn_dt_bias, gdn_norm_w, w_gdn_up, ssm_conv_w, ssm_conv_b, ssm_dt_bias, ssm_A_log, ssm_D, ssm_norm_w, w_ssm_up, w_out, g_mem, w_mk, w_mv, g_x, w_cq, w_co, g_ff, w_ff_up, w_ff_down, g_final):
    bp, seq, _ = x_prompt.shape
    bs, dec_seq, _ = x_sample.shape
    depth = w_in.shape[0]
    n_mem = mem_prompt.shape[1]
    assert seq % STEP == 0 and dec_seq == 1 and bs % SAMPLE_BT == 0 and bs % XATTN_BT == 0
    tp = bp * seq

    xp = x_prompt.reshape(tp, D_MODEL)
    xs = x_sample.reshape(bs, D_MODEL)
    mem = mem_prompt.reshape(bp * n_mem, D_MODEL)
    gst_all = state_gdn_conv.astype(F32).reshape(depth * bs, (CONV_W - 1) * GDN_CH)
    sst_all = state_ssm_conv.astype(F32).reshape(depth * bs, (CONV_W - 1) * SSM_CH)
    sg_all = state_gdn.astype(F32).reshape(depth * bs, GDN_HEADS, GDN_DK, GDN_DV)
    ss_all = state_ssm.astype(F32).reshape(depth * bs, SSM_HEADS, SSM_P, SSM_N)
    def cache_rows(c):
        c = c.astype(F32).reshape(depth * bs, n_mem, X_HEADS, X_HD // 128, 128)
        return c.transpose(0, 1, 3, 2, 4).reshape(depth * bs, n_mem * X_HEADS * (X_HD // 128), 128)

    ck_all = cache_rows(cache_mem_k)
    cv_all = cache_rows(cache_mem_v)
    outs ={k: [] for k in ("p_gc", "p_g", "p_sc", "p_s", "p_mk", "p_mv", "s_gc", "s_g", "s_sc", "s_s")}

    starts = [0]
    for width in IN_SPLITS:
        starts.append(starts[-1] + width)
    col = lambda i: slice(starts[i], starts[i + 1])

    for l in range(depth):
        wl = w_in[l]
        w_main = jnp.concatenate([wl[:, col(0)], wl[:, col(1)], wl[:, col(4)], wl[:, col(5)],
                                  wl[:, col(7)], wl[:, col(8)]], axis=1).astype(BF16)
        w_small = jnp.concatenate(
            [wl[:, col(2)], wl[:, col(3)], wl[:, col(6)],
             jnp.zeros((D_MODEL, SMALL_COLS - 2 * GDN_HEADS - SSM_HEADS), F32)], axis=1).astype(BF16)
        gdn_par = _param_rows(_lane_row(gdn_A_log[l], LANE_GDEC), _lane_row(gdn_dt_bias[l], LANE_GDEC))
        ssm_par = _param_rows(_lane_row(ssm_dt_bias[l], LANE_DT), _lane_row(ssm_A_log[l], LANE_DT))
        gdn_nw = gdn_norm_w[l].reshape(1, GDN_DV).astype(F32)
        ssm_nw = ssm_norm_w[l].reshape(1, SSM_DINNER).astype(F32)
        d_row = jnp.repeat(ssm_D[l].astype(F32), SSM_P).reshape(1, SSM_DINNER)
        gcw = gdn_conv_w[l].astype(F32)
        scw = ssm_conv_w[l].astype(F32)
        scb = ssm_conv_b[l].reshape(1, SSM_CH).astype(F32)
        wg, ws, wo = w_gdn_up[l].astype(BF16), w_ssm_up[l].astype(BF16), w_out[l].astype(BF16)
        wq, wco = w_cq[l].astype(BF16), w_co[l].astype(BF16)
        wu, wd = w_ff_up[l].astype(BF16), w_ff_down[l].astype(BF16)
        last = l == depth - 1

        mk = _rms_mm(mem, g_mem[l], w_mk[l].astype(BF16), tm=1024, tn=1024)
        mv = _rms_mm(mem, g_mem[l], w_mv[l].astype(BF16), tm=1024, tn=1024)
        outs["p_mk"].append(mk.reshape(bp, n_mem, X_HEADS, X_HD))
        outs["p_mv"].append(mv.reshape(bp, n_mem, X_HEADS, X_HD))

        proj, small = _rms_mm(xp, g_mix[l], w_main, w_small, tm=2048, tn=1024)
        o_a, p_g = _gdn_prompt(proj, small, gcw, gdn_par, gdn_nw, bp, seq)
        y_b, p_s = _ssd_prompt(proj, small, scw, scb, ssm_par, d_row, ssm_nw, bp, seq)
        proj3 = proj.reshape(bp, seq, MAIN_COLS)
        outs["p_gc"].append(proj3[:, seq - (CONV_W - 1):, OFF_QKV:OFF_QKV + GDN_CH])
        outs["p_sc"].append(proj3[:, seq - (CONV_W - 1):, OFF_XS:OFF_XS + SSM_CH])
        outs["p_g"].append(p_g)
        outs["p_s"].append(p_s)
        xp = _merge(o_a, y_b, proj, xp, wg, ws, wo, tm=512)
        xp = _xattn_prompt(xp, g_x[l], wq, wco, mk, mv, seq, tm=512)
        xp = _mlp(xp, g_ff[l], wu, wd, g_final, tm=1024, tf=1024, final_norm=last)

        proj, small = _rms_mm(xs, g_mix[l], w_main, w_small, tm=128, tn=2048)
        o_a, s_g = _gdn_sample(proj, small, gst_all, sg_all, l, gcw, gdn_par, gdn_nw)
        y_b, s_s = _ssd_sample(proj, small, sst_all, ss_all, l, scw, scb, ssm_par, d_row, ssm_nw)
        outs["s_gc"].append(jnp.concatenate(
            [state_gdn_conv[l, :, 1:].astype(F32), proj[:, None, OFF_QKV:OFF_QKV + GDN_CH]], axis=1))
        outs["s_sc"].append(jnp.concatenate(
            [state_ssm_conv[l, :, 1:].astype(F32), proj[:, None, OFF_XS:OFF_XS + SSM_CH]], axis=1))
        outs["s_g"].append(s_g)
        outs["s_s"].append(s_s)
        xs = _merge(o_a, y_b, proj, xs, wg, ws, wo, tm=128)
        xs = _xattn_sample(xs, g_x[l], wq, wco, ck_all, cv_all, l)
        xs = _mlp(xs, g_ff[l], wu, wd, g_final, tm=128, tf=2048, final_norm=last)

    stack = lambda k: jnp.stack(outs[k])
    return (xp.reshape(bp, seq, D_MODEL), xs.reshape(bs, dec_seq, D_MODEL),
            stack("p_gc"), stack("p_g"), stack("p_sc"), stack("p_s"), stack("p_mk"), stack("p_mv"),
            stack("s_gc"), stack("s_g"), stack("s_sc"), stack("s_s"))
```

```python
import functools

import jax
import jax.numpy as jnp
from jax import lax
from jax.experimental import pallas as pl
from jax.experimental.pallas import tpu as pltpu

F32 = jnp.float32
BF16 = jnp.bfloat16

EPS = 1e-6
CHUNK = 64
CONV_W = 4
D_MODEL = 1024
GDN_HEADS = 8
GDN_DK = 128
GDN_DV = 128
GDN_QK = GDN_HEADS * GDN_DK
GDN_V = GDN_HEADS * GDN_DV
GDN_CH = 2 * GDN_QK + GDN_V
SSM_DINNER = 2 * D_MODEL
SSM_P = 64
SSM_HEADS = SSM_DINNER // SSM_P
SSM_GROUPS = 8
SSM_R = SSM_HEADS // SSM_GROUPS
SSM_N = 128
SSM_BC = SSM_GROUPS * SSM_N
SSM_CH = SSM_DINNER + 2 * SSM_BC
SSM_GW = SSM_R * SSM_P
X_HEADS = 4
X_HD = D_MODEL // X_HEADS
D_FF = 4 * D_MODEL
IN_SPLITS = (GDN_CH, GDN_V, GDN_HEADS, GDN_HEADS, SSM_DINNER, SSM_CH, SSM_HEADS, D_MODEL, D_MODEL)

MAIN_COLS = GDN_CH + GDN_V + SSM_DINNER + SSM_CH + 2 * D_MODEL
OFF_QKV, OFF_GATE, OFF_Z = 0, GDN_CH, GDN_CH + GDN_V
OFF_XS = OFF_Z + SSM_DINNER
OFF_B = OFF_XS + SSM_DINNER
OFF_C = OFF_B + SSM_BC
OFF_GA = OFF_C + SSM_BC
OFF_GB = OFF_GA + D_MODEL
SMALL_COLS = 128
LANE_BETA, LANE_GDEC, LANE_DT = 0, GDN_HEADS, 2 * GDN_HEADS

STEP = 4 * CHUNK
GDN_SET = 2
TAIL = 8
SAMPLE_BT = 8
XATTN_BT = 4
VMEM_LIMIT = 54 * 1024 * 1024


def _params(n_axes):
    return pltpu.CompilerParams(dimension_semantics=("arbitrary",) * n_axes, vmem_limit_bytes=VMEM_LIMIT)


def _sigmoid(x):
    return 0.5 * jnp.tanh(0.5 * x) + 0.5


def _silu(x):
    h = 0.5 * x
    return h + h * jnp.tanh(h)


def _softplus(x):
    return jnp.maximum(x, 0.0) + jnp.log1p(jnp.exp(-jnp.abs(x)))


def _rmsnorm(x, g):
    xf = x.astype(F32)
    return xf * lax.rsqrt(jnp.mean(xf * xf, axis=-1, keepdims=True) + EPS) * g


def _dot(a, b):
    return jnp.dot(a.astype(BF16), b.astype(BF16), preferred_element_type=F32)


def _dot_nt(a, b):
    return lax.dot_general(a.astype(BF16), b.astype(BF16), (((1,), (1,)), ((), ())), preferred_element_type=F32)


def _dot_tn(a, b):
    return lax.dot_general(a.astype(BF16), b.astype(BF16), (((0,), (0,)), ((), ())), preferred_element_type=F32)


def _chunk_cumsum(x):
    n = x.shape[0]
    r = lax.broadcasted_iota(jnp.int32, (n, n), 0)
    c = lax.broadcasted_iota(jnp.int32, (n, n), 1)
    tri = jnp.where((r >= c) & ((r // CHUNK) == (c // CHUNK)), 1.0, 0.0).astype(BF16)
    h1 = x.astype(BF16)
    r1 = x - h1.astype(F32)
    h2 = r1.astype(BF16)
    h3 = (r1 - h2.astype(F32)).astype(BF16)
    d = functools.partial(jnp.dot, preferred_element_type=F32)
    return d(tri, h1) + (d(tri, h2) + d(tri, h3))


def _chunk_masks():
    r = lax.broadcasted_iota(jnp.int32, (CHUNK, CHUNK), 0)
    c = lax.broadcasted_iota(jnp.int32, (CHUNK, CHUNK), 1)
    return r >= c, r > c, jnp.where(r == c, 1.0, 0.0).astype(F32)


def _decay_matrix(col, row, tril):
    return jnp.where(tril, jnp.exp(jnp.where(tril, col - row, 0.0)), 0.0)


INV_BASE = 8


def _mm_shared(lhs_list, b):
    m = lhs_list[0].shape[0]
    out = _dot(jnp.concatenate(lhs_list, axis=0), b)
    return [out[i * m:(i + 1) * m] for i in range(len(lhs_list))]


def _inv_unit_lower_levels(a_list, eye, out):
    size = a_list[0].shape[0]
    r = lax.broadcasted_iota(jnp.int32, (size, size), 0)
    c = lax.broadcasted_iota(jnp.int32, (size, size), 1)
    same = (r // INV_BASE) == (c // INV_BASE)
    n = [jnp.where(same, -a, 0.0) for a in a_list]
    t = [eye + ni for ni in n]
    p = [_dot(ni, ni) for ni in n]
    yield
    terms = 2
    while terms < INV_BASE:
        terms *= 2
        if terms < INV_BASE:
            res = [_mm_shared([pi, ti], pi) for pi, ti in zip(p, t)]
            t = [ti + ri[1] for ti, ri in zip(t, res)]
            p = [ri[0] for ri in res]
        else:
            t = [ti + _dot(ti, pi) for pi, ti in zip(p, t)]
        yield
    b = INV_BASE
    while b < size:
        low = ((r // (2 * b)) == (c // (2 * b))) & ((r // b) % 2 == 1) & ((c // b) % 2 == 0)
        y = [_dot(jnp.where(low, a, 0.0), ti) for a, ti in zip(a_list, t)]
        yield
        t = [ti - _dot(ti, yi) for ti, yi in zip(t, y)]
        yield
        b *= 2
    out.extend(t)


def _interleave(gen, thunks, gen_steps):
    thunks = list(thunks)
    per_step = -(-len(thunks) // gen_steps)
    for _ in gen:
        for th in thunks[:per_step]:
            th()
        thunks = thunks[per_step:]
    for th in thunks:
        th()


def _rms_mm_body(x_ref, g_ref, w_ref, o_ref, hn_ref):
    @pl.when(pl.program_id(1) == 0)
    def _():
        hn_ref[...] = _rmsnorm(x_ref[...], g_ref[...]).astype(BF16)

    o_ref[...] = jnp.dot(hn_ref[...], w_ref[...], preferred_element_type=F32).astype(o_ref.dtype)


def _rms_mm2_body(x_ref, g_ref, w_ref, w2_ref, o_ref, o2_ref, hn_ref):
    @pl.when(pl.program_id(1) == 0)
    def _():
        hn_ref[...] = _rmsnorm(x_ref[...], g_ref[...]).astype(BF16)
        o2_ref[...] = jnp.dot(hn_ref[...], w2_ref[...], preferred_element_type=F32)

    o_ref[...] = jnp.dot(hn_ref[...], w_ref[...], preferred_element_type=F32).astype(o_ref.dtype)


def _rms_mm(x, g, w, w2=None, *, tm, tn, out_dtype=F32):
    t, k = x.shape
    n = w.shape[1]
    tm, tn = min(tm, t), min(tn, n)
    assert t % tm == 0 and n % tn == 0
    in_specs = [pl.BlockSpec((tm, k), lambda i, j: (i, 0)),
                pl.BlockSpec((1, k), lambda i, j: (0, 0)),
                pl.BlockSpec((k, tn), lambda i, j: (0, j))]
    out_specs = [pl.BlockSpec((tm, tn), lambda i, j: (i, j))]
    out_shape = [jax.ShapeDtypeStruct((t, n), out_dtype)]
    args = [x, g.reshape(1, k), w]
    if w2 is not None:
        n2 = w2.shape[1]
        in_specs.append(pl.BlockSpec((k, n2), lambda i, j: (0, 0)))
        out_specs.append(pl.BlockSpec((tm, n2), lambda i, j: (i, 0)))
        out_shape.append(jax.ShapeDtypeStruct((t, n2), F32))
        args.append(w2)
    res = pl.pallas_call(
        _rms_mm_body if w2 is None else _rms_mm2_body,
        grid=(t // tm, n // tn),
        in_specs=in_specs,
        out_specs=out_specs,
        out_shape=out_shape,
        scratch_shapes=[pltpu.VMEM((tm, k), BF16)],
        compiler_params=_params(2),
        name="rms_matmul",
    )(*args)
    return res if w2 is not None else res[0]


def _gdn_prompt_body(qkv_ref, gate_ref, sm_ref, cw_ref, par_ref, nw_ref, o_ref, sfin_ref, xbuf, s_ref):
    step = pl.program_id(1)

    @pl.when(step == 0)
    def _():
        xbuf[0:TAIL, :] = jnp.zeros((TAIL, GDN_CH), F32)
        s_ref[...] = jnp.zeros_like(s_ref)

    xbuf[TAIL:TAIL + STEP, :] = qkv_ref[...].astype(F32)
    sm = sm_ref[...]
    beta_all = _sigmoid(sm)
    g_all = -jnp.exp(par_ref[0:1, :]) * _softplus(sm + par_ref[1:2, :])
    gc_all = _chunk_cumsum(g_all)
    gc_t = gc_all.T
    tril, strict, eye = _chunk_masks()
    nw = nw_ref[...]

    def conv(blk, r0):
        lo = blk * 128
        acc = cw_ref[CONV_W - 1:CONV_W, lo:lo + 128] * xbuf[pl.ds(TAIL + r0, CHUNK), lo:lo + 128]
        for j in range(CONV_W - 1):
            acc = acc + cw_ref[j:j + 1, lo:lo + 128] * xbuf[pl.ds(TAIL - (CONV_W - 1) + j + r0, CHUNK), lo:lo + 128]
        return _silu(acc)

    def pre_item(c, h):
        r0 = c * CHUNK
        q = conv(h, r0)
        k = conv(GDN_HEADS + h, r0)
        v = conv(2 * GDN_HEADS + h, r0)
        q = q * lax.rsqrt(jnp.sum(q * q, axis=-1, keepdims=True) + EPS) * (GDN_DK ** -0.5)
        k = k * lax.rsqrt(jnp.sum(k * k, axis=-1, keepdims=True) + EPS)
        lg = LANE_GDEC + h
        bcol = beta_all[r0:r0 + CHUNK, LANE_BETA + h:LANE_BETA + h + 1]
        gcol = gc_all[r0:r0 + CHUNK, lg:lg + 1]
        grow = gc_t[lg:lg + 1, r0:r0 + CHUNK]
        glast = gc_all[r0 + CHUNK - 1:r0 + CHUNK, lg:lg + 1]
        decay = _decay_matrix(gcol, grow, tril)
        kb = k * bcol
        egc = jnp.exp(gcol)
        return dict(
            a=jnp.where(strict, _dot_nt(kb, k) * decay, 0.0),
            rhs=jnp.concatenate([v * bcol, kb * egc], axis=1),
            aqk=_dot_nt(q, k) * decay,
            qg=q * egc,
            kd=k * jnp.exp(glast - gcol),
            dch=jnp.exp(glast))

    heads = range(GDN_HEADS)
    states = [s_ref[h] for h in heads]

    def recurrence_thunks(chunks, pre_s, sols):
        box = {}

        def round_ws(i):
            box["ws"] = [_dot(jnp.concatenate([sols[i][h][:, GDN_DV:], pre_s[i][h]["qg"]], axis=0), states[h])
                         for h in heads]

        def round_out(i):
            box["vn"] = [sols[i][h][:, :GDN_DV] - box["ws"][h][:CHUNK] for h in heads]
            box["o", i] = [box["ws"][h][CHUNK:] + _dot(pre_s[i][h]["aqk"], box["vn"][h]) for h in heads]

        def round_state(i):
            for h in heads:
                states[h] = states[h] * pre_s[i][h]["dch"] + _dot_tn(pre_s[i][h]["kd"], box["vn"][h])

        def store(i, h):
            r0 = chunks[i] * CHUNK
            gt = gate_ref[r0:r0 + CHUNK, h * GDN_DV:(h + 1) * GDN_DV].astype(F32)
            o_ref[r0:r0 + CHUNK, h * GDN_DV:(h + 1) * GDN_DV] = (
                _rmsnorm(box["o", i][h], nw) * _silu(gt)).astype(o_ref.dtype)

        ths = []
        for i in range(len(chunks)):
            ths += [functools.partial(f, i) for f in (round_ws, round_out, round_state)]
        return ths + [functools.partial(store, i, h) for i in range(len(chunks)) for h in heads]

    sets = [list(range(s, s + GDN_SET)) for s in range(0, STEP // CHUNK, GDN_SET)]
    doublings, merges = INV_BASE.bit_length() - 2, (CHUNK // INV_BASE).bit_length() - 1
    inv_steps = 1 + doublings + 2 * merges
    pre = {0: [[pre_item(c, h) for h in heads] for c in sets[0]]}
    pending = []
    for s, chunks in enumerate(sets):
        ahead = []
        if s + 1 < len(sets):
            pre[s + 1] = [[] for _ in sets[s + 1]]
            ahead = [functools.partial(lambda ss, i, c, h: pre[ss][i].append(pre_item(c, h)), s + 1, i, c, h)
                     for i, c in enumerate(sets[s + 1]) for h in heads]
        mixed = [th for pair in zip(pending, ahead) for th in pair]
        mixed += pending[len(ahead):] + ahead[len(pending):]
        tinv = []
        _interleave(_inv_unit_lower_levels([p["a"] for pc in pre[s] for p in pc], eye, tinv), mixed, inv_steps)
        sols = [[_dot(tinv[i * GDN_HEADS + h], pre[s][i][h]["rhs"]) for h in heads] for i in range(len(chunks))]
        pending = recurrence_thunks(chunks, pre[s], sols)
    for th in pending:
        th()
    for h in heads:
        s_ref[h] = states[h]

    xbuf[0:TAIL, :] = xbuf[STEP:STEP + TAIL, :]

    @pl.when(step == pl.num_programs(1) - 1)
    def _():
        sfin_ref[0] = s_ref[...]


def _gdn_prompt(proj, small, cw, par, nw, bsz, seq):
    nsteps = seq // STEP
    row = lambda b, s: b * nsteps + s
    return pl.pallas_call(
        _gdn_prompt_body,
        grid=(bsz, nsteps),
        in_specs=[pl.BlockSpec((STEP, GDN_CH), lambda b, s: (row(b, s), OFF_QKV // GDN_CH)),
                  pl.BlockSpec((STEP, GDN_V), lambda b, s: (row(b, s), OFF_GATE // GDN_V)),
                  pl.BlockSpec((STEP, SMALL_COLS), lambda b, s: (row(b, s), 0)),
                  pl.BlockSpec((CONV_W, GDN_CH), lambda b, s: (0, 0)),
                  pl.BlockSpec((8, SMALL_COLS), lambda b, s: (0, 0)),
                  pl.BlockSpec((1, GDN_DV), lambda b, s: (0, 0))],
        out_specs=[pl.BlockSpec((STEP, GDN_V), lambda b, s: (row(b, s), 0)),
                   pl.BlockSpec((1, GDN_HEADS, GDN_DK, GDN_DV), lambda b, s: (b, 0, 0, 0))],
        out_shape=[jax.ShapeDtypeStruct((bsz * seq, GDN_V), BF16),
                   jax.ShapeDtypeStruct((bsz, GDN_HEADS, GDN_DK, GDN_DV), F32)],
        scratch_shapes=[pltpu.VMEM((TAIL + STEP, GDN_CH), F32),
                        pltpu.VMEM((GDN_HEADS, GDN_DK, GDN_DV), F32)],
        compiler_params=_params(2),
        name="gdn_prompt",
    )(proj, proj, small, cw, par, nw)


def _ssd_prompt_body(xs_ref, b_ref, c_ref, z_ref, sm_ref, cw_ref, cb_ref, par_ref, d_ref, nw_ref,
                     y_ref, sfin_ref, xb_x, xb_b, xb_c, st_ref):
    step = pl.program_id(1)

    @pl.when(step == 0)
    def _():
        xb_x[0:TAIL, :] = jnp.zeros((TAIL, SSM_DINNER), F32)
        xb_b[0:TAIL, :] = jnp.zeros((TAIL, SSM_BC), F32)
        xb_c[0:TAIL, :] = jnp.zeros((TAIL, SSM_BC), F32)
        st_ref[...] = jnp.zeros_like(st_ref)

    xb_x[TAIL:TAIL + STEP, :] = xs_ref[...].astype(F32)
    xb_b[TAIL:TAIL + STEP, :] = b_ref[...].astype(F32)
    xb_c[TAIL:TAIL + STEP, :] = c_ref[...].astype(F32)

    def conv(buf, r0, lo, width, woff):
        wl = woff + lo
        acc = cw_ref[CONV_W - 1:CONV_W, wl:wl + width] * buf[pl.ds(TAIL + r0, CHUNK), lo:lo + width]
        acc = acc + cb_ref[0:1, wl:wl + width]
        for j in range(CONV_W - 1):
            acc = acc + cw_ref[j:j + 1, wl:wl + width] * buf[pl.ds(TAIL - (CONV_W - 1) + j + r0, CHUNK), lo:lo + width]
        return _silu(acc)

    sm = sm_ref[...]
    dt_all = _softplus(sm + par_ref[0:1, :])
    cs_all = _chunk_cumsum(dt_all * (-jnp.exp(par_ref[1:2, :])))
    cs_t = cs_all.T
    tril, _, _ = _chunk_masks()

    assert SSM_P == CHUNK
    pw = 2 * SSM_P
    npair = SSM_R // 2
    items = [(c, g) for c in range(STEP // CHUNK) for g in range(SSM_GROUPS)]
    pairs = [(c, g, pr) for c, g in items for pr in range(npair)]
    lane_id = lax.broadcasted_iota(jnp.int32, (CHUNK, pw), 1)
    first = lane_id < SSM_P
    first_row = first[0:1]
    r2 = lax.broadcasted_iota(jnp.int32, (pw, pw), 0)
    c2 = lax.broadcasted_iota(jnp.int32, (pw, pw), 1)
    same_head = (r2 < SSM_P) == (c2 < SSM_P)
    tril_pair = lax.broadcasted_iota(jnp.int32, (CHUNK, pw), 0) >= (lane_id % SSM_P)

    def pair_cols(src, c, g, pr):
        l0 = LANE_DT + SSM_R * g + 2 * pr
        r0 = c * CHUNK
        return jnp.where(first, src[r0:r0 + CHUNK, l0:l0 + 1], src[r0:r0 + CHUNK, l0 + 1:l0 + 2])

    def pair_row(row0, row1):
        return jnp.where(first_row, row0, row1)

    dts, cscs, csrs, csls, lms = [], [], [], [], []

    def gating(c, g, pr):
        l0 = LANE_DT + SSM_R * g + 2 * pr
        r0 = c * CHUNK
        seg = (r0 // pw) * pw
        t0 = cs_t[l0:l0 + 1, seg:seg + pw]
        t1 = cs_t[l0 + 1:l0 + 2, seg:seg + pw]
        if r0 == seg:
            csr = pair_row(t0, pltpu.roll(t1, CHUNK, 1))
        else:
            csr = pair_row(pltpu.roll(t0, CHUNK, 1), t1)
        last = r0 + CHUNK - 1
        csc = pair_cols(cs_all, c, g, pr)
        dts.append(pair_cols(dt_all, c, g, pr))
        cscs.append(csc)
        csrs.append(csr)
        csls.append(pair_row(cs_all[last:last + 1, l0:l0 + 1], cs_all[last:last + 1, l0 + 1:l0 + 2]))
        lms.append(_decay_matrix(csc, csr, tril_pair))

    for p in pairs:
        gating(*p)
    bcs = [conv(xb_b, c * CHUNK, g * SSM_N, SSM_N, SSM_DINNER) for c, g in items]
    ccs = [conv(xb_c, c * CHUNK, g * SSM_N, SSM_N, SSM_DINNER + SSM_BC) for c, g in items]
    xss = [conv(xb_x, c * CHUNK, g * SSM_GW + pr * pw, pw, 0) for c, g, pr in pairs]
    cbs = [_dot_nt(cc, jnp.concatenate([bc, bc], axis=0)) for cc, bc in zip(ccs, bcs)]
    xdts = [xs * dt for xs, dt in zip(xss, dts)]
    ms = [cbs[i // npair] * lm for i, lm in enumerate(lms)]
    bds = [jnp.where(same_head, jnp.concatenate([x, x], axis=0), 0.0) for x in xdts]
    ylocs = []
    for (c, g, pr), m, bd, xs in zip(pairs, ms, bds, xss):
        lo = g * SSM_GW + pr * pw
        ylocs.append(_dot(m, bd) + d_ref[0:1, lo:lo + pw] * xs)
    ecss = [jnp.exp(csc) for csc in cscs]
    xds = [xdt * jnp.exp(csl - csc) for xdt, csl, csc in zip(xdts, csls, cscs)]
    dchs = [jnp.exp(csl) for csl in csls]

    def group_cat(vals, i):
        return jnp.concatenate(vals[i * npair:(i + 1) * npair], axis=1)

    outs = []
    for i, (c, g) in enumerate(items):
        st_g = st_ref[g]
        outs.append(group_cat(ylocs, i) + _dot(ccs[i], st_g) * group_cat(ecss, i))
        st_ref[g] = st_g * group_cat(dchs, i) + _dot_tn(bcs[i], group_cat(xds, i))
    for (c, g), y_g in zip(items, outs):
        r0 = c * CHUNK
        y_g = y_g * _silu(z_ref[r0:r0 + CHUNK, g * SSM_GW:(g + 1) * SSM_GW].astype(F32))
        y_ref[r0:r0 + CHUNK, g * SSM_GW:(g + 1) * SSM_GW] = _rmsnorm(
            y_g, nw_ref[0:1, g * SSM_GW:(g + 1) * SSM_GW]).astype(y_ref.dtype)

    xb_x[0:TAIL, :] = xb_x[STEP:STEP + TAIL, :]
    xb_b[0:TAIL, :] = xb_b[STEP:STEP + TAIL, :]
    xb_c[0:TAIL, :] = xb_c[STEP:STEP + TAIL, :]

    @pl.when(step == pl.num_programs(1) - 1)
    def _():
        for g in range(SSM_GROUPS):
            st_t = st_ref[g].T
            for rr in range(SSM_R):
                sfin_ref[0, SSM_R * g + rr] = st_t[rr * SSM_P:(rr + 1) * SSM_P, :]


def _ssd_prompt(proj, small, cw, cb, par, d_row, nw, bsz, seq):
    nsteps = seq // STEP
    row = lambda b, s: b * nsteps + s
    const = lambda b, s: (0, 0)
    return pl.pallas_call(
        _ssd_prompt_body,
        grid=(bsz, nsteps),
        in_specs=[pl.BlockSpec((STEP, SSM_DINNER), lambda b, s: (row(b, s), OFF_XS // SSM_DINNER)),
                  pl.BlockSpec((STEP, SSM_BC), lambda b, s: (row(b, s), OFF_B // SSM_BC)),
                  pl.BlockSpec((STEP, SSM_BC), lambda b, s: (row(b, s), OFF_C // SSM_BC)),
                  pl.BlockSpec((STEP, SSM_DINNER), lambda b, s: (row(b, s), OFF_Z // SSM_DINNER)),
                  pl.BlockSpec((STEP, SMALL_COLS), lambda b, s: (row(b, s), 0)),
                  pl.BlockSpec((CONV_W, SSM_CH), const),
                  pl.BlockSpec((1, SSM_CH), const),
                  pl.BlockSpec((8, SMALL_COLS), const),
                  pl.BlockSpec((1, SSM_DINNER), const),
                  pl.BlockSpec((1, SSM_DINNER), const)],
        out_specs=[pl.BlockSpec((STEP, SSM_DINNER), lambda b, s: (row(b, s), 0)),
                   pl.BlockSpec((1, SSM_HEADS, SSM_P, SSM_N), lambda b, s: (b, 0, 0, 0))],
        out_shape=[jax.ShapeDtypeStruct((bsz * seq, SSM_DINNER), BF16),
                   jax.ShapeDtypeStruct((bsz, SSM_HEADS, SSM_P, SSM_N), F32)],
        scratch_shapes=[pltpu.VMEM((TAIL + STEP, SSM_DINNER), F32),
                        pltpu.VMEM((TAIL + STEP, SSM_BC), F32),
                        pltpu.VMEM((TAIL + STEP, SSM_BC), F32),
                        pltpu.VMEM((SSM_GROUPS, SSM_N, SSM_GW), F32)],
        compiler_params=_params(2),
        name="ssd_prompt",
    )(proj, proj, proj, proj, small, cw, cb, par, d_row, nw)


def _gdn_sample_body(qkv_ref, gate_ref, sm_ref, cst_ref, s_ref, cw_ref, par_ref, nw_ref, o_ref, sout_ref):
    bt = qkv_ref.shape[0]
    sm = sm_ref[...]
    beta_all = _sigmoid(sm)
    eg_all = jnp.exp(-jnp.exp(par_ref[0:1, :]) * _softplus(sm + par_ref[1:2, :]))
    rowid = lax.broadcasted_iota(jnp.int32, (bt, GDN_DK), 0)
    nw = nw_ref[...]

    def conv(blk):
        lo = blk * 128
        acc = cw_ref[CONV_W - 1:CONV_W, lo:lo + 128] * qkv_ref[:, lo:lo + 128]
        for j in range(CONV_W - 1):
            acc = acc + cw_ref[j:j + 1, lo:lo + 128] * cst_ref[:, j * GDN_CH + lo:j * GDN_CH + lo + 128]
        return _silu(acc)

    for h in range(GDN_HEADS):
        q = conv(h)
        k = conv(GDN_HEADS + h)
        v = conv(2 * GDN_HEADS + h)
        q = q * lax.rsqrt(jnp.sum(q * q, axis=-1, keepdims=True) + EPS) * (GDN_DK ** -0.5)
        k = k * lax.rsqrt(jnp.sum(k * k, axis=-1, keepdims=True) + EPS)
        bcol = beta_all[:, LANE_BETA + h:LANE_BETA + h + 1]
        egcol = eg_all[:, LANE_GDEC + h:LANE_GDEC + h + 1]
        qs_rows, ks_rows = [], []
        for t in range(bt):
            lhs = jnp.where(rowid == 0, q[t:t + 1, :], jnp.where(rowid == 1, k[t:t + 1, :], 0.0))
            r = _dot(lhs, s_ref[t, h])
            qs_rows.append(r[0:1])
            ks_rows.append(r[1:2])
        q_s = jnp.concatenate(qs_rows, axis=0)
        k_s = jnp.concatenate(ks_rows, axis=0)
        v_new = bcol * v - (bcol * egcol) * k_s
        o = egcol * q_s + jnp.sum(q * k, axis=-1, keepdims=True) * v_new
        for t in range(bt):
            k_t = jnp.where(rowid == t, k, 0.0)
            sout_ref[t, h] = s_ref[t, h] * egcol[t:t + 1, :] + _dot_tn(k_t, v_new)
        gt = gate_ref[:, h * GDN_DV:(h + 1) * GDN_DV]
        o_ref[:, h * GDN_DV:(h + 1) * GDN_DV] = (_rmsnorm(o, nw) * _silu(gt)).astype(o_ref.dtype)


def _gdn_sample(proj, small, cstate, state, layer, cw, par, nw):
    t = proj.shape[0]
    bt = SAMPLE_BT
    blk0 = layer * (t // bt)
    const = lambda i: (0, 0)
    return pl.pallas_call(
        _gdn_sample_body,
        grid=(t // bt,),
        in_specs=[pl.BlockSpec((bt, GDN_CH), lambda i: (i, OFF_QKV // GDN_CH)),
                  pl.BlockSpec((bt, GDN_V), lambda i: (i, OFF_GATE // GDN_V)),
                  pl.BlockSpec((bt, SMALL_COLS), lambda i: (i, 0)),
                  pl.BlockSpec((bt, (CONV_W - 1) * GDN_CH), lambda i: (blk0 + i, 0)),
                  pl.BlockSpec((bt, GDN_HEADS, GDN_DK, GDN_DV), lambda i: (blk0 + i, 0, 0, 0)),
                  pl.BlockSpec((CONV_W, GDN_CH), const),
                  pl.BlockSpec((8, SMALL_COLS), const),
                  pl.BlockSpec((1, GDN_DV), const)],
        out_specs=[pl.BlockSpec((bt, GDN_V), lambda i: (i, 0)),
                   pl.BlockSpec((bt, GDN_HEADS, GDN_DK, GDN_DV), lambda i: (i, 0, 0, 0))],
        out_shape=[jax.ShapeDtypeStruct((t, GDN_V), BF16),
                   jax.ShapeDtypeStruct((t,) + state.shape[1:], F32)],
        compiler_params=_params(1),
        name="gdn_sample",
    )(proj, proj, small, cstate, state, cw, par, nw)


def _ssd_sample_body(xs_ref, b_ref, c_ref, z_ref, sm_ref, cst_ref, s_ref, cw_ref, cb_ref, par_ref, d_ref, nw_ref,
                     y_ref, sout_ref):
    bt = xs_ref.shape[0]
    sm = sm_ref[...]
    dt_all = _softplus(sm + par_ref[0:1, :])
    e_all = jnp.exp(dt_all * (-jnp.exp(par_ref[1:2, :])))
    rowid_n = lax.broadcasted_iota(jnp.int32, (bt, SSM_N), 0)
    rowid_g = lax.broadcasted_iota(jnp.int32, (bt, SSM_GW), 0)

    def conv(x_ref, lo, width, woff):
        wl = woff + lo
        acc = cw_ref[CONV_W - 1:CONV_W, wl:wl + width] * x_ref[:, lo:lo + width] + cb_ref[0:1, wl:wl + width]
        for j in range(CONV_W - 1):
            acc = acc + cw_ref[j:j + 1, wl:wl + width] * cst_ref[:, j * SSM_CH + wl:j * SSM_CH + wl + width]
        return _silu(acc)

    for g in range(SSM_GROUPS):
        bc = conv(b_ref, g * SSM_N, SSM_N, SSM_DINNER)
        cc = conv(c_ref, g * SSM_N, SSM_N, SSM_DINNER + SSM_BC)
        xs_g = conv(xs_ref, g * SSM_GW, SSM_GW, 0)
        cb = jnp.sum(cc * bc, axis=-1, keepdims=True)
        yoff_rows = []
        for t in range(bt):
            c_t = jnp.where(rowid_n == 0, cc[t:t + 1, :], 0.0)
            s_tg = jnp.concatenate([s_ref[t, SSM_R * g + rr] for rr in range(SSM_R)], axis=0)
            yoff_rows.append(_dot_nt(c_t, s_tg)[0:1])
        yoff = jnp.concatenate(yoff_rows, axis=0)
        lanes = [LANE_DT + SSM_R * g + rr for rr in range(SSM_R)]
        dt_g = jnp.concatenate([jnp.broadcast_to(dt_all[:, l:l + 1], (bt, SSM_P)) for l in lanes], axis=1)
        e_g = jnp.concatenate([jnp.broadcast_to(e_all[:, l:l + 1], (bt, SSM_P)) for l in lanes], axis=1)
        xdt = xs_g * dt_g
        y = cb * xdt + yoff * e_g + d_ref[0:1, g * SSM_GW:(g + 1) * SSM_GW] * xs_g
        for t in range(bt):
            outer = _dot_tn(jnp.where(rowid_g == t, xdt, 0.0), bc)
            for rr in range(SSM_R):
                h = SSM_R * g + rr
                sout_ref[t, h] = (s_ref[t, h] * e_all[t:t + 1, lanes[rr]:lanes[rr] + 1]
                                  + outer[rr * SSM_P:(rr + 1) * SSM_P, :])
        y = y * _silu(z_ref[:, g * SSM_GW:(g + 1) * SSM_GW])
        y_ref[:, g * SSM_GW:(g + 1) * SSM_GW] = _rmsnorm(
            y, nw_ref[0:1, g * SSM_GW:(g + 1) * SSM_GW]).astype(y_ref.dtype)


def _ssd_sample(proj, small, cstate, state, layer, cw, cb, par, d_row, nw):
    t = proj.shape[0]
    bt = SAMPLE_BT
    blk0 = layer * (t // bt)
    const = lambda i: (0, 0)
    return pl.pallas_call(
        _ssd_sample_body,
        grid=(t // bt,),
        in_specs=[pl.BlockSpec((bt, SSM_DINNER), lambda i: (i, OFF_XS // SSM_DINNER)),
                  pl.BlockSpec((bt, SSM_BC), lambda i: (i, OFF_B // SSM_BC)),
                  pl.BlockSpec((bt, SSM_BC), lambda i: (i, OFF_C // SSM_BC)),
                  pl.BlockSpec((bt, SSM_DINNER), lambda i: (i, OFF_Z // SSM_DINNER)),
                  pl.BlockSpec((bt, SMALL_COLS), lambda i: (i, 0)),
                  pl.BlockSpec((bt, (CONV_W - 1) * SSM_CH), lambda i: (blk0 + i, 0)),
                  pl.BlockSpec((bt, SSM_HEADS, SSM_P, SSM_N), lambda i: (blk0 + i, 0, 0, 0)),
                  pl.BlockSpec((CONV_W, SSM_CH), const),
                  pl.BlockSpec((1, SSM_CH), const),
                  pl.BlockSpec((8, SMALL_COLS), const),
                  pl.BlockSpec((1, SSM_DINNER), const),
                  pl.BlockSpec((1, SSM_DINNER), const)],
        out_specs=[pl.BlockSpec((bt, SSM_DINNER), lambda i: (i, 0)),
                   pl.BlockSpec((bt, SSM_HEADS, SSM_P, SSM_N), lambda i: (i, 0, 0, 0))],
        out_shape=[jax.ShapeDtypeStruct((t, SSM_DINNER), BF16),
                   jax.ShapeDtypeStruct((t,) + state.shape[1:], F32)],
        compiler_params=_params(1),
        name="ssd_sample",
    )(proj, proj, proj, proj, small, cstate, state, cw, cb, par, d_row, nw)


def _merge_body(oa_ref, yb_ref, ga_ref, gb_ref, x_ref, wg_ref, ws_ref, wo_ref, o_ref):
    a = jnp.dot(oa_ref[...], wg_ref[...], preferred_element_type=F32)
    b = jnp.dot(yb_ref[...], ws_ref[...], preferred_element_type=F32)
    merged = _sigmoid(ga_ref[...].astype(F32)) * a + _sigmoid(gb_ref[...].astype(F32)) * b
    o_ref[...] = x_ref[...] + _dot(merged, wo_ref[...])


def _merge(o_a, y_b, proj, x, wg, ws, wo, *, tm):
    t = x.shape[0]
    tm = min(tm, t)
    assert t % tm == 0
    const = lambda i: (0, 0)
    return pl.pallas_call(
        _merge_body,
        grid=(t // tm,),
        in_specs=[pl.BlockSpec((tm, GDN_V), lambda i: (i, 0)),
                  pl.BlockSpec((tm, SSM_DINNER), lambda i: (i, 0)),
                  pl.BlockSpec((tm, D_MODEL), lambda i: (i, OFF_GA // D_MODEL)),
                  pl.BlockSpec((tm, D_MODEL), lambda i: (i, OFF_GB // D_MODEL)),
                  pl.BlockSpec((tm, D_MODEL), lambda i: (i, 0)),
                  pl.BlockSpec((GDN_V, D_MODEL), const),
                  pl.BlockSpec((SSM_DINNER, D_MODEL), const),
                  pl.BlockSpec((D_MODEL, D_MODEL), const)],
        out_specs=pl.BlockSpec((tm, D_MODEL), lambda i: (i, 0)),
        out_shape=jax.ShapeDtypeStruct((t, D_MODEL), F32),
        compiler_params=_params(1),
        name="merge_out",
    )(o_a, y_b, proj, proj, x, wg, ws, wo)


def _softmax_rows(s):
    e = jnp.exp(s - jnp.max(s, axis=-1, keepdims=True))
    return e / jnp.sum(e, axis=-1, keepdims=True)


def _xattn_prompt_body(x_ref, gx_ref, wq_ref, wo_ref, k_ref, v_ref, o_ref):
    x = x_ref[...]
    q = _dot(_rmsnorm(x, gx_ref[...]), wq_ref[...])
    outs = []
    for h in range(X_HEADS):
        sl = slice(h * X_HD, (h + 1) * X_HD)
        p = _softmax_rows(_dot_nt(q[:, sl], k_ref[:, sl]) * (X_HD ** -0.5))
        outs.append(_dot(p, v_ref[:, sl]))
    o_ref[...] = x + _dot(jnp.concatenate(outs, axis=1), wo_ref[...])


def _xattn_prompt(x, gx, wq, wo, mk, mv, seq, *, tm):
    t = x.shape[0]
    n_mem = mk.shape[0] // (t // seq)
    tm = min(tm, seq)
    assert seq % tm == 0
    per_seq = seq // tm
    const = lambda i: (0, 0)
    return pl.pallas_call(
        _xattn_prompt_body,
        grid=(t // tm,),
        in_specs=[pl.BlockSpec((tm, D_MODEL), lambda i: (i, 0)),
                  pl.BlockSpec((1, D_MODEL), const),
                  pl.BlockSpec((D_MODEL, D_MODEL), const),
                  pl.BlockSpec((D_MODEL, D_MODEL), const),
                  pl.BlockSpec((n_mem, D_MODEL), lambda i: (i // per_seq, 0)),
                  pl.BlockSpec((n_mem, D_MODEL), lambda i: (i // per_seq, 0))],
        out_specs=pl.BlockSpec((tm, D_MODEL), lambda i: (i, 0)),
        out_shape=jax.ShapeDtypeStruct((t, D_MODEL), F32),
        compiler_params=_params(1),
        name="xattn_prompt",
    )(x, gx.reshape(1, D_MODEL), wq, wo, mk, mv)


def _xattn_sample_body(x_ref, gx_ref, wq_ref, wo_ref, k_ref, v_ref, o_ref, q_scr, a_scr):
    i = pl.program_id(0)
    bt = k_ref.shape[0]

    @pl.when(i == 0)
    def _():
        q_scr[...] = _dot(_rmsnorm(x_ref[...], gx_ref[...]), wq_ref[...])

    nrow = k_ref.shape[1]
    sub = lax.broadcasted_iota(jnp.int32, (8, nrow), 0)
    lane = lax.broadcasted_iota(jnp.int32, (8, nrow), 1)
    own = (lane % 8) == sub
    lower = sub < X_HEADS
    for tt in range(bt):
        row = i * bt + tt
        q_t = q_scr[pl.ds(row, 1), :]
        q_rows = jnp.concatenate(
            [q_t[:, h * X_HD + half * 128:h * X_HD + (half + 1) * 128]
             for half in range(X_HD // 128) for h in range(X_HEADS)], axis=0)
        e = jnp.where(own, _dot_nt(q_rows, k_ref[tt]), 0.0)
        s = (e + pltpu.roll(pltpu.roll(e, X_HEADS, 0), nrow - X_HEADS, 1)) * (X_HD ** -0.5)
        s = jnp.where(own, s, -jnp.inf)
        p = jnp.exp(s - jnp.max(s, axis=-1, keepdims=True))
        p = jnp.where(lower, p / jnp.sum(p, axis=-1, keepdims=True), 0.0)
        p2 = p + pltpu.roll(pltpu.roll(p, X_HEADS, 0), X_HEADS, 1)
        o2 = _dot(p2, v_ref[tt])
        a_scr[pl.ds(row, 1), :] = jnp.concatenate(
            [o2[half * X_HEADS + h:half * X_HEADS + h + 1, :]
             for h in range(X_HEADS) for half in range(X_HD // 128)], axis=1)

    @pl.when(i == pl.num_programs(0) - 1)
    def _():
        o_ref[...] = x_ref[...] + _dot(a_scr[...], wo_ref[...])


def _xattn_sample(x, gx, wq, wo, ck, cv, layer):
    t = x.shape[0]
    n_mem = ck.shape[1]
    bt = XATTN_BT
    blk0 = layer * (t // bt)
    const = lambda i: (0, 0)
    return pl.pallas_call(
        _xattn_sample_body,
        grid=(t // bt,),
        in_specs=[pl.BlockSpec((t, D_MODEL), const),
                  pl.BlockSpec((1, D_MODEL), const),
                  pl.BlockSpec((D_MODEL, D_MODEL), const),
                  pl.BlockSpec((D_MODEL, D_MODEL), const),
                  pl.BlockSpec((bt,) + ck.shape[1:], lambda i: (blk0 + i, 0, 0)),
                  pl.BlockSpec((bt,) + cv.shape[1:], lambda i: (blk0 + i, 0, 0))],
        out_specs=pl.BlockSpec((t, D_MODEL), const),
        out_shape=jax.ShapeDtypeStruct((t, D_MODEL), F32),
        scratch_shapes=[pltpu.VMEM((t, D_MODEL), F32), pltpu.VMEM((t, D_MODEL), F32)],
        compiler_params=_params(1),
        name="xattn_sample",
    )(x, gx.reshape(1, D_MODEL), wq, wo, ck, cv)


def _mlp_body(x_ref, g_ref, wu_ref, wd_ref, gf_ref, y_ref, hn_ref, acc_ref, *, final_norm):
    j = pl.program_id(1)

    @pl.when(j == 0)
    def _():
        hn_ref[...] = _rmsnorm(x_ref[...], g_ref[...]).astype(BF16)
        acc_ref[...] = jnp.zeros_like(acc_ref)

    hf = jnp.dot(hn_ref[...], wu_ref[...], preferred_element_type=F32)
    acc_ref[...] += _dot(jnp.square(jnp.maximum(hf, 0.0)), wd_ref[...])

    @pl.when(j == pl.num_programs(1) - 1)
    def _():
        x_new = x_ref[...] + acc_ref[...]
        y_ref[...] = _rmsnorm(x_new, gf_ref[...]) if final_norm else x_new


def _mlp(x, g, wu, wd, gf, *, tm, tf, final_norm):
    t = x.shape[0]
    tm = min(tm, t)
    assert t % tm == 0 and D_FF % tf == 0
    const = lambda i, j: (0, 0)
    return pl.pallas_call(
        functools.partial(_mlp_body, final_norm=final_norm),
        grid=(t // tm, D_FF // tf),
        in_specs=[pl.BlockSpec((tm, D_MODEL), lambda i, j: (i, 0)),
                  pl.BlockSpec((1, D_MODEL), const),
                  pl.BlockSpec((D_MODEL, tf), lambda i, j: (0, j)),
                  pl.BlockSpec((tf, D_MODEL), lambda i, j: (j, 0)),
                  pl.BlockSpec((1, D_MODEL), const)],
        out_specs=pl.BlockSpec((tm, D_MODEL), lambda i, j: (i, 0)),
        out_shape=jax.ShapeDtypeStruct((t, D_MODEL), F32),
        scratch_shapes=[pltpu.VMEM((tm, D_MODEL), BF16), pltpu.VMEM((tm, D_MODEL), F32)],
        compiler_params=_params(2),
        name="mlp",
    )(x, g.reshape(1, D_MODEL), wu, wd, gf.reshape(1, D_MODEL))


def _lane_row(vec, lane0):
    return jnp.zeros((SMALL_COLS,), F32).at[lane0:lane0 + vec.shape[0]].set(vec.astype(F32))


def _param_rows(*rows):
    out = jnp.zeros((8, SMALL_COLS), F32)
    for i, r in enumerate(rows):
        out = out.at[i].set(r)
    return out


def kernel(x_prompt, x_sample, mem_prompt, state_gdn_conv, state_gdn, state_ssm_conv, state_ssm, cache_mem_k, cache_mem_v, g_mix, w_in, gdn_conv_w, gdn_A_log, gdn_dt_bias, gdn_norm_w, w_gdn_up, ssm_conv_w, ssm_conv_b, ssm_dt_bias, ssm_A_log, ssm_D, ssm_norm_w, w_ssm_up, w_out, g_mem, w_mk, w_mv, g_x, w_cq, w_co, g_ff, w_ff_up, w_ff_down, g_final):
    bp, seq, _ = x_prompt.shape
    bs, dec_seq, _ = x_sample.shape
    depth = w_in.shape[0]
    n_mem = mem_prompt.shape[1]
    assert seq % STEP == 0 and dec_seq == 1 and bs % SAMPLE_BT == 0 and bs % XATTN_BT == 0
    tp = bp * seq

    xp = x_prompt.reshape(tp, D_MODEL)
    xs = x_sample.reshape(bs, D_MODEL)
    mem = mem_prompt.reshape(bp * n_mem, D_MODEL)
    gst_all = state_gdn_conv.astype(F32).reshape(depth * bs, (CONV_W - 1) * GDN_CH)
    sst_all = state_ssm_conv.astype(F32).reshape(depth * bs, (CONV_W - 1) * SSM_CH)
    sg_all = state_gdn.astype(F32).reshape(depth * bs, GDN_HEADS, GDN_DK, GDN_DV)
    ss_all = state_ssm.astype(F32).reshape(depth * bs, SSM_HEADS, SSM_P, SSM_N)
    def cache_rows(c):
        c = c.astype(F32).reshape(depth * bs, n_mem, X_HEADS, X_HD // 128, 128)
        return c.transpose(0, 1, 3, 2, 4).reshape(depth * bs, n_mem * X_HEADS * (X_HD // 128), 128)

    ck_all = cache_rows(cache_mem_k)
    cv_all = cache_rows(cache_mem_v)
    outs ={k: [] for k in ("p_gc", "p_g", "p_sc", "p_s", "p_mk", "p_mv", "s_gc", "s_g", "s_sc", "s_s")}

    starts = [0]
    for width in IN_SPLITS:
        starts.append(starts[-1] + width)
    col = lambda i: slice(starts[i], starts[i + 1])

    for l in range(depth):
        wl = w_in[l]
        w_main = jnp.concatenate([wl[:, col(0)], wl[:, col(1)], wl[:, col(4)], wl[:, col(5)],
                                  wl[:, col(7)], wl[:, col(8)]], axis=1).astype(BF16)
        w_small = jnp.concatenate(
            [wl[:, col(2)], wl[:, col(3)], wl[:, col(6)],
             jnp.zeros((D_MODEL, SMALL_COLS - 2 * GDN_HEADS - SSM_HEADS), F32)], axis=1).astype(BF16)
        gdn_par = _param_rows(_lane_row(gdn_A_log[l], LANE_GDEC), _lane_row(gdn_dt_bias[l], LANE_GDEC))
        ssm_par = _param_rows(_lane_row(ssm_dt_bias[l], LANE_DT), _lane_row(ssm_A_log[l], LANE_DT))
        gdn_nw = gdn_norm_w[l].reshape(1, GDN_DV).astype(F32)
        ssm_nw = ssm_norm_w[l].reshape(1, SSM_DINNER).astype(F32)
        d_row = jnp.repeat(ssm_D[l].astype(F32), SSM_P).reshape(1, SSM_DINNER)
        gcw = gdn_conv_w[l].astype(F32)
        scw = ssm_conv_w[l].astype(F32)
        scb = ssm_conv_b[l].reshape(1, SSM_CH).astype(F32)
        wg, ws, wo = w_gdn_up[l].astype(BF16), w_ssm_up[l].astype(BF16), w_out[l].astype(BF16)
        wq, wco = w_cq[l].astype(BF16), w_co[l].astype(BF16)
        wu, wd = w_ff_up[l].astype(BF16), w_ff_down[l].astype(BF16)
        last = l == depth - 1

        mk = _rms_mm(mem, g_mem[l], w_mk[l].astype(BF16), tm=1024, tn=1024)
        mv = _rms_mm(mem, g_mem[l], w_mv[l].astype(BF16), tm=1024, tn=1024)
        outs["p_mk"].append(mk.reshape(bp, n_mem, X_HEADS, X_HD))
        outs["p_mv"].append(mv.reshape(bp, n_mem, X_HEADS, X_HD))

        proj, small = _rms_mm(xp, g_mix[l], w_main, w_small, tm=2048, tn=1024, out_dtype=BF16)
        o_a, p_g = _gdn_prompt(proj, small, gcw, gdn_par, gdn_nw, bp, seq)
        y_b, p_s = _ssd_prompt(proj, small, scw, scb, ssm_par, d_row, ssm_nw, bp, seq)
        proj3 = proj.reshape(bp, seq, MAIN_COLS)
        outs["p_gc"].append(proj3[:, seq - (CONV_W - 1):, OFF_QKV:OFF_QKV + GDN_CH].astype(F32))
        outs["p_sc"].append(proj3[:, seq - (CONV_W - 1):, OFF_XS:OFF_XS + SSM_CH].astype(F32))
        outs["p_g"].append(p_g)
        outs["p_s"].append(p_s)
        xp = _merge(o_a, y_b, proj, xp, wg, ws, wo, tm=512)
        xp = _xattn_prompt(xp, g_x[l], wq, wco, mk, mv, seq, tm=512)
        xp = _mlp(xp, g_ff[l], wu, wd, g_final, tm=1024, tf=1024, final_norm=last)

        proj, small = _rms_mm(xs, g_mix[l], w_main, w_small, tm=128, tn=2048)
        o_a, s_g = _gdn_sample(proj, small, gst_all, sg_all, l, gcw, gdn_par, gdn_nw)
        y_b, s_s = _ssd_sample(proj, small, sst_all, ss_all, l, scw, scb, ssm_par, d_row, ssm_nw)
        outs["s_gc"].append(jnp.concatenate(
            [state_gdn_conv[l, :, 1:].astype(F32), proj[:, None, OFF_QKV:OFF_QKV + GDN_CH]], axis=1))
        outs["s_sc"].append(jnp.concatenate(
            [state_ssm_conv[l, :, 1:].astype(F32), proj[:, None, OFF_XS:OFF_XS + SSM_CH]], axis=1))
        outs["s_g"].append(s_g)
        outs["s_s"].append(s_s)
        xs = _merge(o_a, y_b, proj, xs, wg, ws, wo, tm=128)
        xs = _xattn_sample(xs, g_x[l], wq, wco, ck_all, cv_all, l)
        xs = _mlp(xs, g_ff[l], wu, wd, g_final, tm=128, tf=2048, final_norm=last)

    stack = lambda k: jnp.stack(outs[k])
    return (xp.reshape(bp, seq, D_MODEL), xs.reshape(bs, dec_seq, D_MODEL),
            stack("p_gc"), stack("p_g"), stack("p_sc"), stack("p_s"), stack("p_mk"), stack("p_mv"),
            stack("s_gc"), stack("s_g"), stack("s_sc"), stack("s_s"))
```

```python
import functools

import jax
import jax.numpy as jnp
from jax import lax
from jax.experimental import pallas as pl
from jax.experimental.pallas import tpu as pltpu

F32 = jnp.float32
BF16 = jnp.bfloat16

EPS = 1e-6
CHUNK = 64
CONV_W = 4
D_MODEL = 1024
GDN_HEADS = 8
GDN_DK = 128
GDN_DV = 128
GDN_QK = GDN_HEADS * GDN_DK
GDN_V = GDN_HEADS * GDN_DV
GDN_CH = 2 * GDN_QK + GDN_V
SSM_DINNER = 2 * D_MODEL
SSM_P = 64
SSM_HEADS = SSM_DINNER // SSM_P
SSM_GROUPS = 8
SSM_R = SSM_HEADS // SSM_GROUPS
SSM_N = 128
SSM_BC = SSM_GROUPS * SSM_N
SSM_CH = SSM_DINNER + 2 * SSM_BC
SSM_GW = SSM_R * SSM_P
X_HEADS = 4
X_HD = D_MODEL // X_HEADS
D_FF = 4 * D_MODEL
IN_SPLITS = (GDN_CH, GDN_V, GDN_HEADS, GDN_HEADS, SSM_DINNER, SSM_CH, SSM_HEADS, D_MODEL, D_MODEL)

MAIN_COLS = GDN_CH + GDN_V + SSM_DINNER + SSM_CH + 2 * D_MODEL
OFF_QKV, OFF_GATE, OFF_Z = 0, GDN_CH, GDN_CH + GDN_V
OFF_XS = OFF_Z + SSM_DINNER
OFF_B = OFF_XS + SSM_DINNER
OFF_C = OFF_B + SSM_BC
OFF_GA = OFF_C + SSM_BC
OFF_GB = OFF_GA + D_MODEL
SMALL_COLS = 128
LANE_BETA, LANE_GDEC, LANE_DT = 0, GDN_HEADS, 2 * GDN_HEADS

STEP = 4 * CHUNK
GDN_SET = 2
TAIL = 8
SAMPLE_BT = 8
XATTN_BT = 8
VMEM_LIMIT = 54 * 1024 * 1024


def _params(n_axes):
    return pltpu.CompilerParams(dimension_semantics=("arbitrary",) * n_axes, vmem_limit_bytes=VMEM_LIMIT)


def _sigmoid(x):
    return 0.5 * jnp.tanh(0.5 * x) + 0.5


def _silu(x):
    h = 0.5 * x
    return h + h * jnp.tanh(h)


def _softplus(x):
    return jnp.maximum(x, 0.0) + jnp.log1p(jnp.exp(-jnp.abs(x)))


def _rmsnorm(x, g):
    xf = x.astype(F32)
    return xf * lax.rsqrt(jnp.mean(xf * xf, axis=-1, keepdims=True) + EPS) * g


def _dot(a, b):
    return jnp.dot(a.astype(BF16), b.astype(BF16), preferred_element_type=F32)


def _dot_nt(a, b):
    return lax.dot_general(a.astype(BF16), b.astype(BF16), (((1,), (1,)), ((), ())), preferred_element_type=F32)


def _dot_tn(a, b):
    return lax.dot_general(a.astype(BF16), b.astype(BF16), (((0,), (0,)), ((), ())), preferred_element_type=F32)


def _chunk_cumsum(x):
    n = x.shape[0]
    r = lax.broadcasted_iota(jnp.int32, (n, n), 0)
    c = lax.broadcasted_iota(jnp.int32, (n, n), 1)
    tri = jnp.where((r >= c) & ((r // CHUNK) == (c // CHUNK)), 1.0, 0.0).astype(BF16)
    h1 = x.astype(BF16)
    r1 = x - h1.astype(F32)
    h2 = r1.astype(BF16)
    h3 = (r1 - h2.astype(F32)).astype(BF16)
    d = functools.partial(jnp.dot, preferred_element_type=F32)
    return d(tri, h1) + (d(tri, h2) + d(tri, h3))


def _chunk_masks():
    r = lax.broadcasted_iota(jnp.int32, (CHUNK, CHUNK), 0)
    c = lax.broadcasted_iota(jnp.int32, (CHUNK, CHUNK), 1)
    return r >= c, r > c, jnp.where(r == c, 1.0, 0.0).astype(F32)


def _decay_matrix(col, row, tril):
    return jnp.where(tril, jnp.exp(jnp.where(tril, col - row, 0.0)), 0.0)


INV_BASE = 8


def _mm_shared(lhs_list, b):
    m = lhs_list[0].shape[0]
    out = _dot(jnp.concatenate(lhs_list, axis=0), b)
    return [out[i * m:(i + 1) * m] for i in range(len(lhs_list))]


def _inv_unit_lower_levels(a_list, eye, out):
    size = a_list[0].shape[0]
    r = lax.broadcasted_iota(jnp.int32, (size, size), 0)
    c = lax.broadcasted_iota(jnp.int32, (size, size), 1)
    same = (r // INV_BASE) == (c // INV_BASE)
    n = [jnp.where(same, -a, 0.0) for a in a_list]
    t = [eye + ni for ni in n]
    p = [_dot(ni, ni) for ni in n]
    yield
    terms = 2
    while terms < INV_BASE:
        terms *= 2
        if terms < INV_BASE:
            res = [_mm_shared([pi, ti], pi) for pi, ti in zip(p, t)]
            t = [ti + ri[1] for ti, ri in zip(t, res)]
            p = [ri[0] for ri in res]
        else:
            t = [ti + _dot(ti, pi) for pi, ti in zip(p, t)]
        yield
    b = INV_BASE
    while b < size:
        low = ((r // (2 * b)) == (c // (2 * b))) & ((r // b) % 2 == 1) & ((c // b) % 2 == 0)
        y = [_dot(jnp.where(low, a, 0.0), ti) for a, ti in zip(a_list, t)]
        yield
        t = [ti - _dot(ti, yi) for ti, yi in zip(t, y)]
        yield
        b *= 2
    out.extend(t)


def _interleave(gen, thunks, gen_steps):
    thunks = list(thunks)
    per_step = -(-len(thunks) // gen_steps)
    for _ in gen:
        for th in thunks[:per_step]:
            th()
        thunks = thunks[per_step:]
    for th in thunks:
        th()


def _rms_mm_body(x_ref, g_ref, w_ref, o_ref, hn_ref):
    @pl.when(pl.program_id(1) == 0)
    def _():
        hn_ref[...] = _rmsnorm(x_ref[...], g_ref[...]).astype(BF16)

    o_ref[...] = jnp.dot(hn_ref[...], w_ref[...], preferred_element_type=F32).astype(o_ref.dtype)


def _rms_mm(x, g, w, *, tm, tn, out_dtype=F32):
    t, k = x.shape
    n = w.shape[1]
    tm, tn = min(tm, t), min(tn, n)
    assert t % tm == 0 and n % tn == 0
    return pl.pallas_call(
        _rms_mm_body,
        grid=(t // tm, n // tn),
        in_specs=[pl.BlockSpec((tm, k), lambda i, j: (i, 0)),
                  pl.BlockSpec((1, k), lambda i, j: (0, 0)),
                  pl.BlockSpec((k, tn), lambda i, j: (0, j))],
        out_specs=pl.BlockSpec((tm, tn), lambda i, j: (i, j)),
        out_shape=jax.ShapeDtypeStruct((t, n), out_dtype),
        scratch_shapes=[pltpu.VMEM((tm, k), BF16)],
        compiler_params=_params(2),
        name="rms_matmul",
    )(x, g.reshape(1, k), w)


def _in_proj_body(x_ref, g_ref, wa_ref, wb_ref, wc_ref, ws_ref, o_ref, os_ref, hn_ref, *, na, nb):
    j = pl.program_id(1)

    @pl.when(j == 0)
    def _():
        hn_ref[...] = _rmsnorm(x_ref[...], g_ref[...]).astype(BF16)
        os_ref[...] = jnp.dot(hn_ref[...], ws_ref[...], preferred_element_type=F32)

    def emit(w_ref):
        o_ref[...] = jnp.dot(hn_ref[...], w_ref[...], preferred_element_type=F32).astype(o_ref.dtype)

    pl.when(j < na)(functools.partial(emit, wa_ref))
    pl.when((j >= na) & (j < na + nb))(functools.partial(emit, wb_ref))
    pl.when(j >= na + nb)(functools.partial(emit, wc_ref))


def _in_proj(x, g, w_parts, w_small, *, tm, tn, out_dtype):
    t, k = x.shape
    tm = min(tm, t)
    counts = [w.shape[1] // tn for w in w_parts]
    assert t % tm == 0 and all(w.shape[1] % tn == 0 for w in w_parts) and len(w_parts) == 3
    na, nb, nc = counts
    ns = w_small.shape[1]
    part_spec = lambda off, cnt: pl.BlockSpec((k, tn), lambda i, j: (0, jnp.clip(j - off, 0, cnt - 1)))
    return pl.pallas_call(
        functools.partial(_in_proj_body, na=na, nb=nb),
        grid=(t // tm, na + nb + nc),
        in_specs=[pl.BlockSpec((tm, k), lambda i, j: (i, 0)),
                  pl.BlockSpec((1, k), lambda i, j: (0, 0)),
                  part_spec(0, na), part_spec(na, nb), part_spec(na + nb, nc),
                  pl.BlockSpec((k, ns), lambda i, j: (0, 0))],
        out_specs=[pl.BlockSpec((tm, tn), lambda i, j: (i, j)),
                   pl.BlockSpec((tm, ns), lambda i, j: (i, 0))],
        out_shape=[jax.ShapeDtypeStruct((t, tn * (na + nb + nc)), out_dtype),
                   jax.ShapeDtypeStruct((t, ns), F32)],
        scratch_shapes=[pltpu.VMEM((tm, k), BF16)],
        compiler_params=_params(2),
        name="in_proj",
    )(x, g.reshape(1, k), *w_parts, w_small)


def _gdn_prompt_body(qkv_ref, gate_ref, sm_ref, cw_ref, par_ref, nw_ref, o_ref, sfin_ref, xbuf, s_ref):
    step = pl.program_id(1)

    @pl.when(step == 0)
    def _():
        xbuf[0:TAIL, :] = jnp.zeros((TAIL, GDN_CH), F32)
        s_ref[...] = jnp.zeros_like(s_ref)

    xbuf[TAIL:TAIL + STEP, :] = qkv_ref[...].astype(F32)
    sm = sm_ref[...]
    beta_all = _sigmoid(sm)
    g_all = -jnp.exp(par_ref[0:1, :]) * _softplus(sm + par_ref[1:2, :])
    gc_all = _chunk_cumsum(g_all)
    gc_t = gc_all.T
    tril, strict, eye = _chunk_masks()
    nw = nw_ref[...]

    def conv(blk, r0):
        lo = blk * 128
        acc = cw_ref[CONV_W - 1:CONV_W, lo:lo + 128] * xbuf[pl.ds(TAIL + r0, CHUNK), lo:lo + 128]
        for j in range(CONV_W - 1):
            acc = acc + cw_ref[j:j + 1, lo:lo + 128] * xbuf[pl.ds(TAIL - (CONV_W - 1) + j + r0, CHUNK), lo:lo + 128]
        return _silu(acc)

    def pre_item(c, h):
        r0 = c * CHUNK
        q = conv(h, r0)
        k = conv(GDN_HEADS + h, r0)
        v = conv(2 * GDN_HEADS + h, r0)
        q = q * lax.rsqrt(jnp.sum(q * q, axis=-1, keepdims=True) + EPS) * (GDN_DK ** -0.5)
        k = k * lax.rsqrt(jnp.sum(k * k, axis=-1, keepdims=True) + EPS)
        lg = LANE_GDEC + h
        bcol = beta_all[r0:r0 + CHUNK, LANE_BETA + h:LANE_BETA + h + 1]
        gcol = gc_all[r0:r0 + CHUNK, lg:lg + 1]
        grow = gc_t[lg:lg + 1, r0:r0 + CHUNK]
        glast = gc_all[r0 + CHUNK - 1:r0 + CHUNK, lg:lg + 1]
        decay = _decay_matrix(gcol, grow, tril)
        kb = k * bcol
        egc = jnp.exp(gcol)
        return dict(
            a=jnp.where(strict, _dot_nt(kb, k) * decay, 0.0),
            rhs=jnp.concatenate([v * bcol, kb * egc], axis=1),
            aqk=_dot_nt(q, k) * decay,
            qg=q * egc,
            kd=k * jnp.exp(glast - gcol),
            dch=jnp.exp(glast))

    heads = range(GDN_HEADS)
    states = [s_ref[h] for h in heads]

    def recurrence_thunks(chunks, pre_s, sols):
        box = {}

        def round_ws(i):
            box["ws"] = [_dot(jnp.concatenate([sols[i][h][:, GDN_DV:], pre_s[i][h]["qg"]], axis=0), states[h])
                         for h in heads]

        def round_out(i):
            box["vn"] = [sols[i][h][:, :GDN_DV] - box["ws"][h][:CHUNK] for h in heads]
            box["o", i] = [box["ws"][h][CHUNK:] + _dot(pre_s[i][h]["aqk"], box["vn"][h]) for h in heads]

        def round_state(i):
            for h in heads:
                states[h] = states[h] * pre_s[i][h]["dch"] + _dot_tn(pre_s[i][h]["kd"], box["vn"][h])

        def store(i, h):
            r0 = chunks[i] * CHUNK
            gt = gate_ref[r0:r0 + CHUNK, h * GDN_DV:(h + 1) * GDN_DV].astype(F32)
            o_ref[r0:r0 + CHUNK, h * GDN_DV:(h + 1) * GDN_DV] = (
                _rmsnorm(box["o", i][h], nw) * _silu(gt)).astype(o_ref.dtype)

        ths = []
        for i in range(len(chunks)):
            ths += [functools.partial(f, i) for f in (round_ws, round_out, round_state)]
        return ths + [functools.partial(store, i, h) for i in range(len(chunks)) for h in heads]

    sets = [list(range(s, s + GDN_SET)) for s in range(0, STEP // CHUNK, GDN_SET)]
    doublings, merges = INV_BASE.bit_length() - 2, (CHUNK // INV_BASE).bit_length() - 1
    inv_steps = 1 + doublings + 2 * merges
    pre = {0: [[pre_item(c, h) for h in heads] for c in sets[0]]}
    pending = []
    for s, chunks in enumerate(sets):
        ahead = []
        if s + 1 < len(sets):
            pre[s + 1] = [[] for _ in sets[s + 1]]
            ahead = [functools.partial(lambda ss, i, c, h: pre[ss][i].append(pre_item(c, h)), s + 1, i, c, h)
                     for i, c in enumerate(sets[s + 1]) for h in heads]
        mixed = [th for pair in zip(pending, ahead) for th in pair]
        mixed += pending[len(ahead):] + ahead[len(pending):]
        tinv = []
        _interleave(_inv_unit_lower_levels([p["a"] for pc in pre[s] for p in pc], eye, tinv), mixed, inv_steps)
        sols = [[_dot(tinv[i * GDN_HEADS + h], pre[s][i][h]["rhs"]) for h in heads] for i in range(len(chunks))]
        pending = recurrence_thunks(chunks, pre[s], sols)
    for th in pending:
        th()
    for h in heads:
        s_ref[h] = states[h]

    xbuf[0:TAIL, :] = xbuf[STEP:STEP + TAIL, :]

    @pl.when(step == pl.num_programs(1) - 1)
    def _():
        sfin_ref[0] = s_ref[...]


def _gdn_prompt(proj, small, cw, par, nw, bsz, seq):
    nsteps = seq // STEP
    row = lambda b, s: b * nsteps + s
    return pl.pallas_call(
        _gdn_prompt_body,
        grid=(bsz, nsteps),
        in_specs=[pl.BlockSpec((STEP, GDN_CH), lambda b, s: (row(b, s), OFF_QKV // GDN_CH)),
                  pl.BlockSpec((STEP, GDN_V), lambda b, s: (row(b, s), OFF_GATE // GDN_V)),
                  pl.BlockSpec((STEP, SMALL_COLS), lambda b, s: (row(b, s), 0)),
                  pl.BlockSpec((CONV_W, GDN_CH), lambda b, s: (0, 0)),
                  pl.BlockSpec((8, SMALL_COLS), lambda b, s: (0, 0)),
                  pl.BlockSpec((1, GDN_DV), lambda b, s: (0, 0))],
        out_specs=[pl.BlockSpec((STEP, GDN_V), lambda b, s: (row(b, s), 0)),
                   pl.BlockSpec((1, GDN_HEADS, GDN_DK, GDN_DV), lambda b, s: (b, 0, 0, 0))],
        out_shape=[jax.ShapeDtypeStruct((bsz * seq, GDN_V), BF16),
                   jax.ShapeDtypeStruct((bsz, GDN_HEADS, GDN_DK, GDN_DV), F32)],
        scratch_shapes=[pltpu.VMEM((TAIL + STEP, GDN_CH), F32),
                        pltpu.VMEM((GDN_HEADS, GDN_DK, GDN_DV), F32)],
        compiler_params=_params(2),
        name="gdn_prompt",
    )(proj, proj, small, cw, par, nw)


def _ssd_prompt_body(xs_ref, b_ref, c_ref, z_ref, sm_ref, cw_ref, cb_ref, par_ref, d_ref, nw_ref,
                     y_ref, sfin_ref, xb_x, xb_b, xb_c, st_ref):
    step = pl.program_id(1)

    @pl.when(step == 0)
    def _():
        xb_x[0:TAIL, :] = jnp.zeros((TAIL, SSM_DINNER), F32)
        xb_b[0:TAIL, :] = jnp.zeros((TAIL, SSM_BC), F32)
        xb_c[0:TAIL, :] = jnp.zeros((TAIL, SSM_BC), F32)
        st_ref[...] = jnp.zeros_like(st_ref)

    xb_x[TAIL:TAIL + STEP, :] = xs_ref[...].astype(F32)
    xb_b[TAIL:TAIL + STEP, :] = b_ref[...].astype(F32)
    xb_c[TAIL:TAIL + STEP, :] = c_ref[...].astype(F32)

    def conv(buf, r0, lo, width, woff):
        wl = woff + lo
        acc = cw_ref[CONV_W - 1:CONV_W, wl:wl + width] * buf[pl.ds(TAIL + r0, CHUNK), lo:lo + width]
        acc = acc + cb_ref[0:1, wl:wl + width]
        for j in range(CONV_W - 1):
            acc = acc + cw_ref[j:j + 1, wl:wl + width] * buf[pl.ds(TAIL - (CONV_W - 1) + j + r0, CHUNK), lo:lo + width]
        return _silu(acc)

    sm = sm_ref[...]
    dt_all = _softplus(sm + par_ref[0:1, :])
    cs_all = _chunk_cumsum(dt_all * (-jnp.exp(par_ref[1:2, :])))
    cs_t = cs_all.T
    tril, _, _ = _chunk_masks()

    assert SSM_P == CHUNK
    pw = 2 * SSM_P
    npair = SSM_R // 2
    items = [(c, g) for c in range(STEP // CHUNK) for g in range(SSM_GROUPS)]
    pairs = [(c, g, pr) for c, g in items for pr in range(npair)]
    lane_id = lax.broadcasted_iota(jnp.int32, (CHUNK, pw), 1)
    first = lane_id < SSM_P
    first_row = first[0:1]
    r2 = lax.broadcasted_iota(jnp.int32, (pw, pw), 0)
    c2 = lax.broadcasted_iota(jnp.int32, (pw, pw), 1)
    same_head = (r2 < SSM_P) == (c2 < SSM_P)
    tril_pair = lax.broadcasted_iota(jnp.int32, (CHUNK, pw), 0) >= (lane_id % SSM_P)

    def pair_cols(src, c, g, pr):
        l0 = LANE_DT + SSM_R * g + 2 * pr
        r0 = c * CHUNK
        return jnp.where(first, src[r0:r0 + CHUNK, l0:l0 + 1], src[r0:r0 + CHUNK, l0 + 1:l0 + 2])

    def pair_row(row0, row1):
        return jnp.where(first_row, row0, row1)

    dts, cscs, csrs, csls, lms = [], [], [], [], []

    def gating(c, g, pr):
        l0 = LANE_DT + SSM_R * g + 2 * pr
        r0 = c * CHUNK
        seg = (r0 // pw) * pw
        t0 = cs_t[l0:l0 + 1, seg:seg + pw]
        t1 = cs_t[l0 + 1:l0 + 2, seg:seg + pw]
        if r0 == seg:
            csr = pair_row(t0, pltpu.roll(t1, CHUNK, 1))
        else:
            csr = pair_row(pltpu.roll(t0, CHUNK, 1), t1)
        last = r0 + CHUNK - 1
        csc = pair_cols(cs_all, c, g, pr)
        dts.append(pair_cols(dt_all, c, g, pr))
        cscs.append(csc)
        csrs.append(csr)
        csls.append(pair_row(cs_all[last:last + 1, l0:l0 + 1], cs_all[last:last + 1, l0 + 1:l0 + 2]))
        lms.append(_decay_matrix(csc, csr, tril_pair))

    for p in pairs:
        gating(*p)
    bcs = [conv(xb_b, c * CHUNK, g * SSM_N, SSM_N, SSM_DINNER) for c, g in items]
    ccs = [conv(xb_c, c * CHUNK, g * SSM_N, SSM_N, SSM_DINNER + SSM_BC) for c, g in items]
    xss = [conv(xb_x, c * CHUNK, g * SSM_GW + pr * pw, pw, 0) for c, g, pr in pairs]
    cbs = [_dot_nt(cc, jnp.concatenate([bc, bc], axis=0)) for cc, bc in zip(ccs, bcs)]
    xdts = [xs * dt for xs, dt in zip(xss, dts)]
    ms = [cbs[i // npair] * lm for i, lm in enumerate(lms)]
    bds = [jnp.where(same_head, jnp.concatenate([x, x], axis=0), 0.0) for x in xdts]
    ylocs = []
    for (c, g, pr), m, bd, xs in zip(pairs, ms, bds, xss):
        lo = g * SSM_GW + pr * pw
        ylocs.append(_dot(m, bd) + d_ref[0:1, lo:lo + pw] * xs)
    ecss = [jnp.exp(csc) for csc in cscs]
    xds = [xdt * jnp.exp(csl - csc) for xdt, csl, csc in zip(xdts, csls, cscs)]
    dchs = [jnp.exp(csl) for csl in csls]

    def group_cat(vals, i):
        return jnp.concatenate(vals[i * npair:(i + 1) * npair], axis=1)

    outs = []
    for i, (c, g) in enumerate(items):
        st_g = st_ref[g]
        outs.append(group_cat(ylocs, i) + _dot(ccs[i], st_g) * group_cat(ecss, i))
        st_ref[g] = st_g * group_cat(dchs, i) + _dot_tn(bcs[i], group_cat(xds, i))
    for (c, g), y_g in zip(items, outs):
        r0 = c * CHUNK
        y_g = y_g * _silu(z_ref[r0:r0 + CHUNK, g * SSM_GW:(g + 1) * SSM_GW].astype(F32))
        y_ref[r0:r0 + CHUNK, g * SSM_GW:(g + 1) * SSM_GW] = _rmsnorm(
            y_g, nw_ref[0:1, g * SSM_GW:(g + 1) * SSM_GW]).astype(y_ref.dtype)

    xb_x[0:TAIL, :] = xb_x[STEP:STEP + TAIL, :]
    xb_b[0:TAIL, :] = xb_b[STEP:STEP + TAIL, :]
    xb_c[0:TAIL, :] = xb_c[STEP:STEP + TAIL, :]

    @pl.when(step == pl.num_programs(1) - 1)
    def _():
        for g in range(SSM_GROUPS):
            st_t = st_ref[g].T
            for rr in range(SSM_R):
                sfin_ref[0, SSM_R * g + rr] = st_t[rr * SSM_P:(rr + 1) * SSM_P, :]


def _ssd_prompt(proj, small, cw, cb, par, d_row, nw, bsz, seq):
    nsteps = seq // STEP
    row = lambda b, s: b * nsteps + s
    const = lambda b, s: (0, 0)
    return pl.pallas_call(
        _ssd_prompt_body,
        grid=(bsz, nsteps),
        in_specs=[pl.BlockSpec((STEP, SSM_DINNER), lambda b, s: (row(b, s), OFF_XS // SSM_DINNER)),
                  pl.BlockSpec((STEP, SSM_BC), lambda b, s: (row(b, s), OFF_B // SSM_BC)),
                  pl.BlockSpec((STEP, SSM_BC), lambda b, s: (row(b, s), OFF_C // SSM_BC)),
                  pl.BlockSpec((STEP, SSM_DINNER), lambda b, s: (row(b, s), OFF_Z // SSM_DINNER)),
                  pl.BlockSpec((STEP, SMALL_COLS), lambda b, s: (row(b, s), 0)),
                  pl.BlockSpec((CONV_W, SSM_CH), const),
                  pl.BlockSpec((1, SSM_CH), const),
                  pl.BlockSpec((8, SMALL_COLS), const),
                  pl.BlockSpec((1, SSM_DINNER), const),
                  pl.BlockSpec((1, SSM_DINNER), const)],
        out_specs=[pl.BlockSpec((STEP, SSM_DINNER), lambda b, s: (row(b, s), 0)),
                   pl.BlockSpec((1, SSM_HEADS, SSM_P, SSM_N), lambda b, s: (b, 0, 0, 0))],
        out_shape=[jax.ShapeDtypeStruct((bsz * seq, SSM_DINNER), BF16),
                   jax.ShapeDtypeStruct((bsz, SSM_HEADS, SSM_P, SSM_N), F32)],
        scratch_shapes=[pltpu.VMEM((TAIL + STEP, SSM_DINNER), F32),
                        pltpu.VMEM((TAIL + STEP, SSM_BC), F32),
                        pltpu.VMEM((TAIL + STEP, SSM_BC), F32),
                        pltpu.VMEM((SSM_GROUPS, SSM_N, SSM_GW), F32)],
        compiler_params=_params(2),
        name="ssd_prompt",
    )(proj, proj, proj, proj, small, cw, cb, par, d_row, nw)


def _gdn_sample_body(qkv_ref, gate_ref, sm_ref, cst_ref, s_ref, cw_ref, par_ref, nw_ref, o_ref, sout_ref):
    bt = qkv_ref.shape[0]
    sm = sm_ref[...]
    beta_all = _sigmoid(sm)
    eg_all = jnp.exp(-jnp.exp(par_ref[0:1, :]) * _softplus(sm + par_ref[1:2, :]))
    rowid = lax.broadcasted_iota(jnp.int32, (bt, GDN_DK), 0)
    nw = nw_ref[...]

    def conv(blk):
        lo = blk * 128
        acc = cw_ref[CONV_W - 1:CONV_W, lo:lo + 128] * qkv_ref[:, lo:lo + 128]
        for j in range(CONV_W - 1):
            acc = acc + cw_ref[j:j + 1, lo:lo + 128] * cst_ref[:, j * GDN_CH + lo:j * GDN_CH + lo + 128]
        return _silu(acc)

    for h in range(GDN_HEADS):
        q = conv(h)
        k = conv(GDN_HEADS + h)
        v = conv(2 * GDN_HEADS + h)
        q = q * lax.rsqrt(jnp.sum(q * q, axis=-1, keepdims=True) + EPS) * (GDN_DK ** -0.5)
        k = k * lax.rsqrt(jnp.sum(k * k, axis=-1, keepdims=True) + EPS)
        bcol = beta_all[:, LANE_BETA + h:LANE_BETA + h + 1]
        egcol = eg_all[:, LANE_GDEC + h:LANE_GDEC + h + 1]
        qs_rows, ks_rows = [], []
        for t in range(bt):
            lhs = jnp.where(rowid == 0, q[t:t + 1, :], jnp.where(rowid == 1, k[t:t + 1, :], 0.0))
            r = _dot(lhs, s_ref[t, h])
            qs_rows.append(r[0:1])
            ks_rows.append(r[1:2])
        q_s = jnp.concatenate(qs_rows, axis=0)
        k_s = jnp.concatenate(ks_rows, axis=0)
        v_new = bcol * v - (bcol * egcol) * k_s
        o = egcol * q_s + jnp.sum(q * k, axis=-1, keepdims=True) * v_new
        for t in range(bt):
            k_t = jnp.where(rowid == t, k, 0.0)
            sout_ref[t, h] = s_ref[t, h] * egcol[t:t + 1, :] + _dot_tn(k_t, v_new)
        gt = gate_ref[:, h * GDN_DV:(h + 1) * GDN_DV]
        o_ref[:, h * GDN_DV:(h + 1) * GDN_DV] = (_rmsnorm(o, nw) * _silu(gt)).astype(o_ref.dtype)


def _gdn_sample(proj, small, cstate, state, layer, cw, par, nw):
    t = proj.shape[0]
    bt = SAMPLE_BT
    blk0 = layer * (t // bt)
    const = lambda i: (0, 0)
    return pl.pallas_call(
        _gdn_sample_body,
        grid=(t // bt,),
        in_specs=[pl.BlockSpec((bt, GDN_CH), lambda i: (i, OFF_QKV // GDN_CH)),
                  pl.BlockSpec((bt, GDN_V), lambda i: (i, OFF_GATE // GDN_V)),
                  pl.BlockSpec((bt, SMALL_COLS), lambda i: (i, 0)),
                  pl.BlockSpec((bt, (CONV_W - 1) * GDN_CH), lambda i: (blk0 + i, 0)),
                  pl.BlockSpec((bt, GDN_HEADS, GDN_DK, GDN_DV), lambda i: (blk0 + i, 0, 0, 0)),
                  pl.BlockSpec((CONV_W, GDN_CH), const),
                  pl.BlockSpec((8, SMALL_COLS), const),
                  pl.BlockSpec((1, GDN_DV), const)],
        out_specs=[pl.BlockSpec((bt, GDN_V), lambda i: (i, 0)),
                   pl.BlockSpec((bt, GDN_HEADS, GDN_DK, GDN_DV), lambda i: (i, 0, 0, 0))],
        out_shape=[jax.ShapeDtypeStruct((t, GDN_V), BF16),
                   jax.ShapeDtypeStruct((t,) + state.shape[1:], F32)],
        compiler_params=_params(1),
        name="gdn_sample",
    )(proj, proj, small, cstate, state, cw, par, nw)


def _ssd_sample_body(xs_ref, b_ref, c_ref, z_ref, sm_ref, cst_ref, s_ref, cw_ref, cb_ref, par_ref, d_ref, nw_ref,
                     y_ref, sout_ref):
    bt = xs_ref.shape[0]
    sm = sm_ref[...]
    dt_all = _softplus(sm + par_ref[0:1, :])
    e_all = jnp.exp(dt_all * (-jnp.exp(par_ref[1:2, :])))
    rowid_n = lax.broadcasted_iota(jnp.int32, (bt, SSM_N), 0)
    rowid_g = lax.broadcasted_iota(jnp.int32, (bt, SSM_GW), 0)

    def conv(x_ref, lo, width, woff):
        wl = woff + lo
        acc = cw_ref[CONV_W - 1:CONV_W, wl:wl + width] * x_ref[:, lo:lo + width] + cb_ref[0:1, wl:wl + width]
        for j in range(CONV_W - 1):
            acc = acc + cw_ref[j:j + 1, wl:wl + width] * cst_ref[:, j * SSM_CH + wl:j * SSM_CH + wl + width]
        return _silu(acc)

    for g in range(SSM_GROUPS):
        bc = conv(b_ref, g * SSM_N, SSM_N, SSM_DINNER)
        cc = conv(c_ref, g * SSM_N, SSM_N, SSM_DINNER + SSM_BC)
        xs_g = conv(xs_ref, g * SSM_GW, SSM_GW, 0)
        cb = jnp.sum(cc * bc, axis=-1, keepdims=True)
        yoff_rows = []
        for t in range(bt):
            c_t = jnp.where(rowid_n == 0, cc[t:t + 1, :], 0.0)
            s_tg = jnp.concatenate([s_ref[t, SSM_R * g + rr] for rr in range(SSM_R)], axis=0)
            yoff_rows.append(_dot_nt(c_t, s_tg)[0:1])
        yoff = jnp.concatenate(yoff_rows, axis=0)
        lanes = [LANE_DT + SSM_R * g + rr for rr in range(SSM_R)]
        dt_g = jnp.concatenate([jnp.broadcast_to(dt_all[:, l:l + 1], (bt, SSM_P)) for l in lanes], axis=1)
        e_g = jnp.concatenate([jnp.broadcast_to(e_all[:, l:l + 1], (bt, SSM_P)) for l in lanes], axis=1)
        xdt = xs_g * dt_g
        y = cb * xdt + yoff * e_g + d_ref[0:1, g * SSM_GW:(g + 1) * SSM_GW] * xs_g
        for t in range(bt):
            outer = _dot_tn(jnp.where(rowid_g == t, xdt, 0.0), bc)
            for rr in range(SSM_R):
                h = SSM_R * g + rr
                sout_ref[t, h] = (s_ref[t, h] * e_all[t:t + 1, lanes[rr]:lanes[rr] + 1]
                                  + outer[rr * SSM_P:(rr + 1) * SSM_P, :])
        y = y * _silu(z_ref[:, g * SSM_GW:(g + 1) * SSM_GW])
        y_ref[:, g * SSM_GW:(g + 1) * SSM_GW] = _rmsnorm(
            y, nw_ref[0:1, g * SSM_GW:(g + 1) * SSM_GW]).astype(y_ref.dtype)


def _ssd_sample(proj, small, cstate, state, layer, cw, cb, par, d_row, nw):
    t = proj.shape[0]
    bt = SAMPLE_BT
    blk0 = layer * (t // bt)
    const = lambda i: (0, 0)
    return pl.pallas_call(
        _ssd_sample_body,
        grid=(t // bt,),
        in_specs=[pl.BlockSpec((bt, SSM_DINNER), lambda i: (i, OFF_XS // SSM_DINNER)),
                  pl.BlockSpec((bt, SSM_BC), lambda i: (i, OFF_B // SSM_BC)),
                  pl.BlockSpec((bt, SSM_BC), lambda i: (i, OFF_C // SSM_BC)),
                  pl.BlockSpec((bt, SSM_DINNER), lambda i: (i, OFF_Z // SSM_DINNER)),
                  pl.BlockSpec((bt, SMALL_COLS), lambda i: (i, 0)),
                  pl.BlockSpec((bt, (CONV_W - 1) * SSM_CH), lambda i: (blk0 + i, 0)),
                  pl.BlockSpec((bt, SSM_HEADS, SSM_P, SSM_N), lambda i: (blk0 + i, 0, 0, 0)),
                  pl.BlockSpec((CONV_W, SSM_CH), const),
                  pl.BlockSpec((1, SSM_CH), const),
                  pl.BlockSpec((8, SMALL_COLS), const),
                  pl.BlockSpec((1, SSM_DINNER), const),
                  pl.BlockSpec((1, SSM_DINNER), const)],
        out_specs=[pl.BlockSpec((bt, SSM_DINNER), lambda i: (i, 0)),
                   pl.BlockSpec((bt, SSM_HEADS, SSM_P, SSM_N), lambda i: (i, 0, 0, 0))],
        out_shape=[jax.ShapeDtypeStruct((t, SSM_DINNER), BF16),
                   jax.ShapeDtypeStruct((t,) + state.shape[1:], F32)],
        compiler_params=_params(1),
        name="ssd_sample",
    )(proj, proj, proj, proj, small, cstate, state, cw, cb, par, d_row, nw)


def _merge_body(oa_ref, yb_ref, ga_ref, gb_ref, x_ref, wg_ref, ws_ref, wo_ref, o_ref):
    a = jnp.dot(oa_ref[...], wg_ref[...], preferred_element_type=F32)
    b = jnp.dot(yb_ref[...], ws_ref[...], preferred_element_type=F32)
    merged = _sigmoid(ga_ref[...].astype(F32)) * a + _sigmoid(gb_ref[...].astype(F32)) * b
    o_ref[...] = x_ref[...] + _dot(merged, wo_ref[...])


def _merge(o_a, y_b, proj, x, wg, ws, wo, *, tm):
    t = x.shape[0]
    tm = min(tm, t)
    assert t % tm == 0
    const = lambda i: (0, 0)
    return pl.pallas_call(
        _merge_body,
        grid=(t // tm,),
        in_specs=[pl.BlockSpec((tm, GDN_V), lambda i: (i, 0)),
                  pl.BlockSpec((tm, SSM_DINNER), lambda i: (i, 0)),
                  pl.BlockSpec((tm, D_MODEL), lambda i: (i, OFF_GA // D_MODEL)),
                  pl.BlockSpec((tm, D_MODEL), lambda i: (i, OFF_GB // D_MODEL)),
                  pl.BlockSpec((tm, D_MODEL), lambda i: (i, 0)),
                  pl.BlockSpec((GDN_V, D_MODEL), const),
                  pl.BlockSpec((SSM_DINNER, D_MODEL), const),
                  pl.BlockSpec((D_MODEL, D_MODEL), const)],
        out_specs=pl.BlockSpec((tm, D_MODEL), lambda i: (i, 0)),
        out_shape=jax.ShapeDtypeStruct((t, D_MODEL), F32),
        compiler_params=_params(1),
        name="merge_out",
    )(o_a, y_b, proj, proj, x, wg, ws, wo)


def _softmax_rows(s):
    e = jnp.exp(s - jnp.max(s, axis=-1, keepdims=True))
    return e / jnp.sum(e, axis=-1, keepdims=True)


def _xattn_prompt_body(x_ref, gx_ref, wq_ref, wo_ref, k_ref, v_ref, o_ref):
    x = x_ref[...]
    q = _dot(_rmsnorm(x, gx_ref[...]), wq_ref[...])
    outs = []
    for h in range(X_HEADS):
        sl = slice(h * X_HD, (h + 1) * X_HD)
        p = _softmax_rows(_dot_nt(q[:, sl], k_ref[:, sl]) * (X_HD ** -0.5))
        outs.append(_dot(p, v_ref[:, sl]))
    o_ref[...] = x + _dot(jnp.concatenate(outs, axis=1), wo_ref[...])


def _xattn_prompt(x, gx, wq, wo, mk, mv, seq, *, tm):
    t = x.shape[0]
    n_mem = mk.shape[0] // (t // seq)
    tm = min(tm, seq)
    assert seq % tm == 0
    per_seq = seq // tm
    const = lambda i: (0, 0)
    return pl.pallas_call(
        _xattn_prompt_body,
        grid=(t // tm,),
        in_specs=[pl.BlockSpec((tm, D_MODEL), lambda i: (i, 0)),
                  pl.BlockSpec((1, D_MODEL), const),
                  pl.BlockSpec((D_MODEL, D_MODEL), const),
                  pl.BlockSpec((D_MODEL, D_MODEL), const),
                  pl.BlockSpec((n_mem, D_MODEL), lambda i: (i // per_seq, 0)),
                  pl.BlockSpec((n_mem, D_MODEL), lambda i: (i // per_seq, 0))],
        out_specs=pl.BlockSpec((tm, D_MODEL), lambda i: (i, 0)),
        out_shape=jax.ShapeDtypeStruct((t, D_MODEL), F32),
        compiler_params=_params(1),
        name="xattn_prompt",
    )(x, gx.reshape(1, D_MODEL), wq, wo, mk, mv)


def _xattn_sample_body(x_ref, gx_ref, wq_ref, wo_ref, k_ref, v_ref, o_ref, q_scr, a_scr):
    i = pl.program_id(0)
    bt = k_ref.shape[0]

    @pl.when(i == 0)
    def _():
        q_scr[...] = _dot(_rmsnorm(x_ref[...], gx_ref[...]), wq_ref[...])

    nrow = k_ref.shape[1]
    sub = lax.broadcasted_iota(jnp.int32, (8, nrow), 0)
    lane = lax.broadcasted_iota(jnp.int32, (8, nrow), 1)
    own = (lane % 8) == sub
    lower = sub < X_HEADS
    toks = range(bt)
    q_rows = []
    for tt in toks:
        q_t = q_scr[pl.ds(i * bt + tt, 1), :]
        q_rows.append(jnp.concatenate(
            [q_t[:, h * X_HD + half * 128:h * X_HD + (half + 1) * 128]
             for half in range(X_HD // 128) for h in range(X_HEADS)], axis=0))
    es = [jnp.where(own, _dot_nt(q_rows[tt], k_ref[tt]), 0.0) for tt in toks]
    p2s = []
    for e in es:
        s = (e + pltpu.roll(pltpu.roll(e, X_HEADS, 0), nrow - X_HEADS, 1)) * (X_HD ** -0.5)
        s = jnp.where(own, s, -jnp.inf)
        p = jnp.exp(s - jnp.max(s, axis=-1, keepdims=True))
        p = jnp.where(lower, p / jnp.sum(p, axis=-1, keepdims=True), 0.0)
        p2s.append(p + pltpu.roll(pltpu.roll(p, X_HEADS, 0), X_HEADS, 1))
    o2s = [_dot(p2s[tt], v_ref[tt]) for tt in toks]
    for tt in toks:
        a_scr[pl.ds(i * bt + tt, 1), :] = jnp.concatenate(
            [o2s[tt][half * X_HEADS + h:half * X_HEADS + h + 1, :]
             for h in range(X_HEADS) for half in range(X_HD // 128)], axis=1)

    @pl.when(i == pl.num_programs(0) - 1)
    def _():
        o_ref[...] = x_ref[...] + _dot(a_scr[...], wo_ref[...])


def _xattn_sample(x, gx, wq, wo, ck, cv, layer):
    t = x.shape[0]
    n_mem = ck.shape[1]
    bt = XATTN_BT
    blk0 = layer * (t // bt)
    const = lambda i: (0, 0)
    return pl.pallas_call(
        _xattn_sample_body,
        grid=(t // bt,),
        in_specs=[pl.BlockSpec((t, D_MODEL), const),
                  pl.BlockSpec((1, D_MODEL), const),
                  pl.BlockSpec((D_MODEL, D_MODEL), const),
                  pl.BlockSpec((D_MODEL, D_MODEL), const),
                  pl.BlockSpec((bt,) + ck.shape[1:], lambda i: (blk0 + i, 0, 0)),
                  pl.BlockSpec((bt,) + cv.shape[1:], lambda i: (blk0 + i, 0, 0))],
        out_specs=pl.BlockSpec((t, D_MODEL), const),
        out_shape=jax.ShapeDtypeStruct((t, D_MODEL), F32),
        scratch_shapes=[pltpu.VMEM((t, D_MODEL), F32), pltpu.VMEM((t, D_MODEL), F32)],
        compiler_params=_params(1),
        name="xattn_sample",
    )(x, gx.reshape(1, D_MODEL), wq, wo, ck, cv)


def _mlp_body(x_ref, g_ref, wu_ref, wd_ref, gf_ref, y_ref, hn_ref, acc_ref, *, final_norm):
    j = pl.program_id(1)

    @pl.when(j == 0)
    def _():
        hn_ref[...] = _rmsnorm(x_ref[...], g_ref[...]).astype(BF16)
        acc_ref[...] = jnp.zeros_like(acc_ref)

    hf = jnp.dot(hn_ref[...], wu_ref[...], preferred_element_type=F32)
    acc_ref[...] += _dot(jnp.square(jnp.maximum(hf, 0.0)), wd_ref[...])

    @pl.when(j == pl.num_programs(1) - 1)
    def _():
        x_new = x_ref[...] + acc_ref[...]
        y_ref[...] = _rmsnorm(x_new, gf_ref[...]) if final_norm else x_new


def _mlp(x, g, wu, wd, gf, *, tm, tf, final_norm):
    t = x.shape[0]
    tm = min(tm, t)
    assert t % tm == 0 and D_FF % tf == 0
    const = lambda i, j: (0, 0)
    return pl.pallas_call(
        functools.partial(_mlp_body, final_norm=final_norm),
        grid=(t // tm, D_FF // tf),
        in_specs=[pl.BlockSpec((tm, D_MODEL), lambda i, j: (i, 0)),
                  pl.BlockSpec((1, D_MODEL), const),
                  pl.BlockSpec((D_MODEL, tf), lambda i, j: (0, j)),
                  pl.BlockSpec((tf, D_MODEL), lambda i, j: (j, 0)),
                  pl.BlockSpec((1, D_MODEL), const)],
        out_specs=pl.BlockSpec((tm, D_MODEL), lambda i, j: (i, 0)),
        out_shape=jax.ShapeDtypeStruct((t, D_MODEL), F32),
        scratch_shapes=[pltpu.VMEM((tm, D_MODEL), BF16), pltpu.VMEM((tm, D_MODEL), F32)],
        compiler_params=_params(2),
        name="mlp",
    )(x, g.reshape(1, D_MODEL), wu, wd, gf.reshape(1, D_MODEL))


def _lane_row(vec, lane0):
    return jnp.zeros((SMALL_COLS,), F32).at[lane0:lane0 + vec.shape[0]].set(vec.astype(F32))


def _param_rows(*rows):
    out = jnp.zeros((8, SMALL_COLS), F32)
    for i, r in enumerate(rows):
        out = out.at[i].set(r)
    return out


def kernel(x_prompt, x_sample, mem_prompt, state_gdn_conv, state_gdn, state_ssm_conv, state_ssm, cache_mem_k, cache_mem_v, g_mix, w_in, gdn_conv_w, gdn_A_log, gdn_dt_bias, gdn_norm_w, w_gdn_up, ssm_conv_w, ssm_conv_b, ssm_dt_bias, ssm_A_log, ssm_D, ssm_norm_w, w_ssm_up, w_out, g_mem, w_mk, w_mv, g_x, w_cq, w_co, g_ff, w_ff_up, w_ff_down, g_final):
    bp, seq, _ = x_prompt.shape
    bs, dec_seq, _ = x_sample.shape
    depth = w_in.shape[0]
    n_mem = mem_prompt.shape[1]
    assert seq % STEP == 0 and dec_seq == 1 and bs % SAMPLE_BT == 0 and bs % XATTN_BT == 0
    tp = bp * seq

    xp = x_prompt.reshape(tp, D_MODEL)
    xs = x_sample.reshape(bs, D_MODEL)
    mem = mem_prompt.reshape(bp * n_mem, D_MODEL)
    gst_all = state_gdn_conv.astype(F32).reshape(depth * bs, (CONV_W - 1) * GDN_CH)
    sst_all = state_ssm_conv.astype(F32).reshape(depth * bs, (CONV_W - 1) * SSM_CH)
    sg_all = state_gdn.astype(F32).reshape(depth * bs, GDN_HEADS, GDN_DK, GDN_DV)
    ss_all = state_ssm.astype(F32).reshape(depth * bs, SSM_HEADS, SSM_P, SSM_N)
    def cache_rows(c):
        c = c.astype(F32).reshape(depth * bs, n_mem, X_HEADS, X_HD // 128, 128)
        return c.transpose(0, 1, 3, 2, 4).reshape(depth * bs, n_mem * X_HEADS * (X_HD // 128), 128)

    ck_all = cache_rows(cache_mem_k)
    cv_all = cache_rows(cache_mem_v)
    outs ={k: [] for k in ("p_gc", "p_g", "p_sc", "p_s", "p_mk", "p_mv", "s_gc", "s_g", "s_sc", "s_s")}

    starts = [0]
    for width in IN_SPLITS:
        starts.append(starts[-1] + width)
    col = lambda i: slice(starts[i], starts[i + 1])

    for l in range(depth):
        wl = w_in[l]
        w_parts = [wl[:, starts[0]:starts[2]].astype(BF16), wl[:, starts[4]:starts[6]].astype(BF16),
                   wl[:, starts[7]:starts[9]].astype(BF16)]
        w_small = jnp.concatenate(
            [wl[:, col(2)], wl[:, col(3)], wl[:, col(6)],
             jnp.zeros((D_MODEL, SMALL_COLS - 2 * GDN_HEADS - SSM_HEADS), F32)], axis=1).astype(BF16)
        gdn_par = _param_rows(_lane_row(gdn_A_log[l], LANE_GDEC), _lane_row(gdn_dt_bias[l], LANE_GDEC))
        ssm_par = _param_rows(_lane_row(ssm_dt_bias[l], LANE_DT), _lane_row(ssm_A_log[l], LANE_DT))
        gdn_nw = gdn_norm_w[l].reshape(1, GDN_DV).astype(F32)
        ssm_nw = ssm_norm_w[l].reshape(1, SSM_DINNER).astype(F32)
        d_row = jnp.repeat(ssm_D[l].astype(F32), SSM_P).reshape(1, SSM_DINNER)
        gcw = gdn_conv_w[l].astype(F32)
        scw = ssm_conv_w[l].astype(F32)
        scb = ssm_conv_b[l].reshape(1, SSM_CH).astype(F32)
        wg, ws, wo = w_gdn_up[l].astype(BF16), w_ssm_up[l].astype(BF16), w_out[l].astype(BF16)
        wq, wco = w_cq[l].astype(BF16), w_co[l].astype(BF16)
        wu, wd = w_ff_up[l].astype(BF16), w_ff_down[l].astype(BF16)
        last = l == depth - 1

        mk = _rms_mm(mem, g_mem[l], w_mk[l].astype(BF16), tm=1024, tn=1024)
        mv = _rms_mm(mem, g_mem[l], w_mv[l].astype(BF16), tm=1024, tn=1024)
        outs["p_mk"].append(mk.reshape(bp, n_mem, X_HEADS, X_HD))
        outs["p_mv"].append(mv.reshape(bp, n_mem, X_HEADS, X_HD))

        proj, small = _in_proj(xp, g_mix[l], w_parts, w_small, tm=2048, tn=1024, out_dtype=BF16)
        o_a, p_g = _gdn_prompt(proj, small, gcw, gdn_par, gdn_nw, bp, seq)
        y_b, p_s = _ssd_prompt(proj, small, scw, scb, ssm_par, d_row, ssm_nw, bp, seq)
        proj3 = proj.reshape(bp, seq, MAIN_COLS)
        outs["p_gc"].append(proj3[:, seq - (CONV_W - 1):, OFF_QKV:OFF_QKV + GDN_CH].astype(F32))
        outs["p_sc"].append(proj3[:, seq - (CONV_W - 1):, OFF_XS:OFF_XS + SSM_CH].astype(F32))
        outs["p_g"].append(p_g)
        outs["p_s"].append(p_s)
        xp = _merge(o_a, y_b, proj, xp, wg, ws, wo, tm=512)
        xp = _xattn_prompt(xp, g_x[l], wq, wco, mk, mv, seq, tm=1024)
        xp = _mlp(xp, g_ff[l], wu, wd, g_final, tm=1024, tf=1024, final_norm=last)

        proj, small = _in_proj(xs, g_mix[l], w_parts, w_small, tm=128, tn=2048, out_dtype=F32)
        o_a, s_g = _gdn_sample(proj, small, gst_all, sg_all, l, gcw, gdn_par, gdn_nw)
        y_b, s_s = _ssd_sample(proj, small, sst_all, ss_all, l, scw, scb, ssm_par, d_row, ssm_nw)
        outs["s_gc"].append(jnp.concatenate(
            [state_gdn_conv[l, :, 1:].astype(F32), proj[:, None, OFF_QKV:OFF_QKV + GDN_CH]], axis=1))
        outs["s_sc"].append(jnp.concatenate(
            [state_ssm_conv[l, :, 1:].astype(F32), proj[:, None, OFF_XS:OFF_XS + SSM_CH]], axis=1))
        outs["s_g"].append(s_g)
        outs["s_s"].append(s_s)
        xs = _merge(o_a, y_b, proj, xs, wg, ws, wo, tm=128)
        xs = _xattn_sample(xs, g_x[l], wq, wco, ck_all, cv_all, l)
        xs = _mlp(xs, g_ff[l], wu, wd, g_final, tm=128, tf=2048, final_norm=last)

    stack = lambda k: jnp.stack(outs[k])
    return (xp.reshape(bp, seq, D_MODEL), xs.reshape(bs, dec_seq, D_MODEL),
            stack("p_gc"), stack("p_g"), stack("p_sc"), stack("p_s"), stack("p_mk"), stack("p_mv"),
            stack("s_gc"), stack("s_g"), stack("s_sc"), stack("s_s"))
```

```python
import functools

import jax
import jax.numpy as jnp
from jax import lax
from jax.experimental import pallas as pl
from jax.experimental.pallas import tpu as pltpu

F32 = jnp.float32
BF16 = jnp.bfloat16

EPS = 1e-6
CHUNK = 64
CONV_W = 4
D_MODEL = 1024
GDN_HEADS = 8
GDN_DK = 128
GDN_DV = 128
GDN_QK = GDN_HEADS * GDN_DK
GDN_V = GDN_HEADS * GDN_DV
GDN_CH = 2 * GDN_QK + GDN_V
SSM_DINNER = 2 * D_MODEL
SSM_P = 64
SSM_HEADS = SSM_DINNER // SSM_P
SSM_GROUPS = 8
SSM_R = SSM_HEADS // SSM_GROUPS
SSM_N = 128
SSM_BC = SSM_GROUPS * SSM_N
SSM_CH = SSM_DINNER + 2 * SSM_BC
SSM_GW = SSM_R * SSM_P
X_HEADS = 4
X_HD = D_MODEL // X_HEADS
D_FF = 4 * D_MODEL
IN_SPLITS = (GDN_CH, GDN_V, GDN_HEADS, GDN_HEADS, SSM_DINNER, SSM_CH, SSM_HEADS, D_MODEL, D_MODEL)

MAIN_COLS = GDN_CH + GDN_V + SSM_DINNER + SSM_CH + 2 * D_MODEL
OFF_QKV, OFF_GATE, OFF_Z = 0, GDN_CH, GDN_CH + GDN_V
OFF_XS = OFF_Z + SSM_DINNER
OFF_B = OFF_XS + SSM_DINNER
OFF_C = OFF_B + SSM_BC
OFF_GA = OFF_C + SSM_BC
OFF_GB = OFF_GA + D_MODEL
SMALL_COLS = 128
LANE_BETA, LANE_GDEC, LANE_DT = 0, GDN_HEADS, 2 * GDN_HEADS

STEP = 4 * CHUNK
GDN_SET = 2
TAIL = 8
SAMPLE_BT = 8
GDN_SAMPLE_BT = 16
XATTN_BT = 8
VMEM_LIMIT = 54 * 1024 * 1024


def _params(n_axes):
    return pltpu.CompilerParams(dimension_semantics=("arbitrary",) * n_axes, vmem_limit_bytes=VMEM_LIMIT)


def _sigmoid(x):
    return 0.5 * jnp.tanh(0.5 * x) + 0.5


def _silu(x):
    h = 0.5 * x
    return h + h * jnp.tanh(h)


def _softplus(x):
    return jnp.maximum(x, 0.0) + jnp.log1p(jnp.exp(-jnp.abs(x)))


def _rmsnorm(x, g):
    xf = x.astype(F32)
    return xf * lax.rsqrt(jnp.mean(xf * xf, axis=-1, keepdims=True) + EPS) * g


def _dot(a, b):
    return jnp.dot(a.astype(BF16), b.astype(BF16), preferred_element_type=F32)


def _dot_nt(a, b):
    return lax.dot_general(a.astype(BF16), b.astype(BF16), (((1,), (1,)), ((), ())), preferred_element_type=F32)


def _dot_tn(a, b):
    return lax.dot_general(a.astype(BF16), b.astype(BF16), (((0,), (0,)), ((), ())), preferred_element_type=F32)


def _chunk_cumsum(x):
    n = x.shape[0]
    r = lax.broadcasted_iota(jnp.int32, (n, n), 0)
    c = lax.broadcasted_iota(jnp.int32, (n, n), 1)
    tri = jnp.where((r >= c) & ((r // CHUNK) == (c // CHUNK)), 1.0, 0.0).astype(BF16)
    h1 = x.astype(BF16)
    r1 = x - h1.astype(F32)
    h2 = r1.astype(BF16)
    h3 = (r1 - h2.astype(F32)).astype(BF16)
    d = functools.partial(jnp.dot, preferred_element_type=F32)
    return d(tri, h1) + (d(tri, h2) + d(tri, h3))


def _chunk_masks():
    r = lax.broadcasted_iota(jnp.int32, (CHUNK, CHUNK), 0)
    c = lax.broadcasted_iota(jnp.int32, (CHUNK, CHUNK), 1)
    return r >= c, r > c, jnp.where(r == c, 1.0, 0.0).astype(F32)


def _decay_matrix(col, row, tril):
    return jnp.where(tril, jnp.exp(jnp.where(tril, col - row, 0.0)), 0.0)


INV_BASE = 8


def _mm_shared(lhs_list, b):
    m = lhs_list[0].shape[0]
    out = _dot(jnp.concatenate(lhs_list, axis=0), b)
    return [out[i * m:(i + 1) * m] for i in range(len(lhs_list))]


def _inv_unit_lower_levels(a_list, eye, out):
    size = a_list[0].shape[0]
    r = lax.broadcasted_iota(jnp.int32, (size, size), 0)
    c = lax.broadcasted_iota(jnp.int32, (size, size), 1)
    same = (r // INV_BASE) == (c // INV_BASE)
    n = [jnp.where(same, -a, 0.0) for a in a_list]
    t = [eye + ni for ni in n]
    p = [_dot(ni, ni) for ni in n]
    yield
    terms = 2
    while terms < INV_BASE:
        terms *= 2
        if terms < INV_BASE:
            res = [_mm_shared([pi, ti], pi) for pi, ti in zip(p, t)]
            t = [ti + ri[1] for ti, ri in zip(t, res)]
            p = [ri[0] for ri in res]
        else:
            t = [ti + _dot(ti, pi) for pi, ti in zip(p, t)]
        yield
    b = INV_BASE
    while b < size:
        low = ((r // (2 * b)) == (c // (2 * b))) & ((r // b) % 2 == 1) & ((c // b) % 2 == 0)
        y = [_dot(jnp.where(low, a, 0.0), ti) for a, ti in zip(a_list, t)]
        yield
        t = [ti - _dot(ti, yi) for ti, yi in zip(t, y)]
        yield
        b *= 2
    out.extend(t)


def _interleave(gen, thunks, gen_steps):
    thunks = list(thunks)
    per_step = -(-len(thunks) // gen_steps)
    for _ in gen:
        for th in thunks[:per_step]:
            th()
        thunks = thunks[per_step:]
    for th in thunks:
        th()


def _rms_mm_body(x_ref, g_ref, w_ref, o_ref, hn_ref):
    @pl.when(pl.program_id(1) == 0)
    def _():
        hn_ref[...] = _rmsnorm(x_ref[...], g_ref[...]).astype(BF16)

    o_ref[...] = jnp.dot(hn_ref[...], w_ref[...], preferred_element_type=F32).astype(o_ref.dtype)


def _rms_mm(x, g, w, *, tm, tn, out_dtype=F32):
    t, k = x.shape
    n = w.shape[1]
    tm, tn = min(tm, t), min(tn, n)
    assert t % tm == 0 and n % tn == 0
    return pl.pallas_call(
        _rms_mm_body,
        grid=(t // tm, n // tn),
        in_specs=[pl.BlockSpec((tm, k), lambda i, j: (i, 0)),
                  pl.BlockSpec((1, k), lambda i, j: (0, 0)),
                  pl.BlockSpec((k, tn), lambda i, j: (0, j))],
        out_specs=pl.BlockSpec((tm, tn), lambda i, j: (i, j)),
        out_shape=jax.ShapeDtypeStruct((t, n), out_dtype),
        scratch_shapes=[pltpu.VMEM((tm, k), BF16)],
        compiler_params=_params(2),
        name="rms_matmul",
    )(x, g.reshape(1, k), w)


def _in_proj_body(x_ref, g_ref, w_ref, ws_ref, o_ref, os_ref, hn_ref):
    @pl.when(pl.program_id(1) == 0)
    def _():
        hn_ref[...] = _rmsnorm(x_ref[...], g_ref[...]).astype(BF16)
        os_ref[...] = jnp.dot(hn_ref[...], ws_ref[...], preferred_element_type=F32)

    o_ref[...] = jnp.dot(hn_ref[...], w_ref[...], preferred_element_type=F32).astype(o_ref.dtype)


def _in_proj(x, g, w_main, w_small, *, tm, tn, out_dtype):
    t, k = x.shape
    n, ns = w_main.shape[1], w_small.shape[1]
    tm = min(tm, t)
    assert t % tm == 0 and n % tn == 0
    return pl.pallas_call(
        _in_proj_body,
        grid=(t // tm, n // tn),
        in_specs=[pl.BlockSpec((tm, k), lambda i, j: (i, 0)),
                  pl.BlockSpec((1, k), lambda i, j: (0, 0)),
                  pl.BlockSpec((k, tn), lambda i, j: (0, j)),
                  pl.BlockSpec((k, ns), lambda i, j: (0, 0))],
        out_specs=[pl.BlockSpec((tm, tn), lambda i, j: (i, j)),
                   pl.BlockSpec((tm, ns), lambda i, j: (i, 0))],
        out_shape=[jax.ShapeDtypeStruct((t, n), out_dtype),
                   jax.ShapeDtypeStruct((t, ns), F32)],
        scratch_shapes=[pltpu.VMEM((tm, k), BF16)],
        compiler_params=_params(2),
        name="in_proj",
    )(x, g.reshape(1, k), w_main, w_small)


def _gdn_prompt_body(qkv_ref, gate_ref, sm_ref, cw_ref, par_ref, nw_ref, o_ref, sfin_ref, xbuf, s_ref):
    step = pl.program_id(1)

    @pl.when(step == 0)
    def _():
        xbuf[0:TAIL, :] = jnp.zeros((TAIL, GDN_CH), F32)
        s_ref[...] = jnp.zeros_like(s_ref)

    xbuf[TAIL:TAIL + STEP, :] = qkv_ref[...].astype(F32)
    sm = sm_ref[...]
    beta_all = _sigmoid(sm)
    g_all = -jnp.exp(par_ref[0:1, :]) * _softplus(sm + par_ref[1:2, :])
    gc_all = _chunk_cumsum(g_all)
    gc_t = gc_all.T
    tril, strict, eye = _chunk_masks()
    nw = nw_ref[...]

    def conv(blk, r0):
        lo = blk * 128
        acc = cw_ref[CONV_W - 1:CONV_W, lo:lo + 128] * xbuf[pl.ds(TAIL + r0, CHUNK), lo:lo + 128]
        for j in range(CONV_W - 1):
            acc = acc + cw_ref[j:j + 1, lo:lo + 128] * xbuf[pl.ds(TAIL - (CONV_W - 1) + j + r0, CHUNK), lo:lo + 128]
        return _silu(acc)

    def pre_item(c, h):
        r0 = c * CHUNK
        q = conv(h, r0)
        k = conv(GDN_HEADS + h, r0)
        v = conv(2 * GDN_HEADS + h, r0)
        q = q * lax.rsqrt(jnp.sum(q * q, axis=-1, keepdims=True) + EPS) * (GDN_DK ** -0.5)
        k = k * lax.rsqrt(jnp.sum(k * k, axis=-1, keepdims=True) + EPS)
        lg = LANE_GDEC + h
        bcol = beta_all[r0:r0 + CHUNK, LANE_BETA + h:LANE_BETA + h + 1]
        gcol = gc_all[r0:r0 + CHUNK, lg:lg + 1]
        grow = gc_t[lg:lg + 1, r0:r0 + CHUNK]
        glast = gc_all[r0 + CHUNK - 1:r0 + CHUNK, lg:lg + 1]
        decay = _decay_matrix(gcol, grow, tril)
        kb = k * bcol
        egc = jnp.exp(gcol)
        return dict(
            a=jnp.where(strict, _dot_nt(kb, k) * decay, 0.0),
            rhs=jnp.concatenate([v * bcol, kb * egc], axis=1),
            aqk=_dot_nt(q, k) * decay,
            qg=q * egc,
            kd=k * jnp.exp(glast - gcol),
            dch=jnp.exp(glast))

    heads = range(GDN_HEADS)
    states = [s_ref[h] for h in heads]

    def recurrence_thunks(chunks, pre_s, sols):
        box = {}

        def round_ws(i):
            box["ws"] = [_dot(jnp.concatenate([sols[i][h][:, GDN_DV:], pre_s[i][h]["qg"]], axis=0), states[h])
                         for h in heads]

        def round_out(i):
            box["vn"] = [sols[i][h][:, :GDN_DV] - box["ws"][h][:CHUNK] for h in heads]
            box["o", i] = [box["ws"][h][CHUNK:] + _dot(pre_s[i][h]["aqk"], box["vn"][h]) for h in heads]

        def round_state(i):
            for h in heads:
                states[h] = states[h] * pre_s[i][h]["dch"] + _dot_tn(pre_s[i][h]["kd"], box["vn"][h])

        def store(i, h):
            r0 = chunks[i] * CHUNK
            gt = gate_ref[r0:r0 + CHUNK, h * GDN_DV:(h + 1) * GDN_DV].astype(F32)
            o_ref[r0:r0 + CHUNK, h * GDN_DV:(h + 1) * GDN_DV] = (
                _rmsnorm(box["o", i][h], nw) * _silu(gt)).astype(o_ref.dtype)

        ths = []
        for i in range(len(chunks)):
            ths += [functools.partial(f, i) for f in (round_ws, round_out, round_state)]
        return ths + [functools.partial(store, i, h) for i in range(len(chunks)) for h in heads]

    sets = [list(range(s, s + GDN_SET)) for s in range(0, STEP // CHUNK, GDN_SET)]
    doublings, merges = INV_BASE.bit_length() - 2, (CHUNK // INV_BASE).bit_length() - 1
    inv_steps = 1 + doublings + 2 * merges
    pre = {0: [[pre_item(c, h) for h in heads] for c in sets[0]]}
    pending = []
    for s, chunks in enumerate(sets):
        ahead = []
        if s + 1 < len(sets):
            pre[s + 1] = [[] for _ in sets[s + 1]]
            ahead = [functools.partial(lambda ss, i, c, h: pre[ss][i].append(pre_item(c, h)), s + 1, i, c, h)
                     for i, c in enumerate(sets[s + 1]) for h in heads]
        mixed = [th for pair in zip(pending, ahead) for th in pair]
        mixed += pending[len(ahead):] + ahead[len(pending):]
        tinv = []
        _interleave(_inv_unit_lower_levels([p["a"] for pc in pre[s] for p in pc], eye, tinv), mixed, inv_steps)
        sols = [[_dot(tinv[i * GDN_HEADS + h], pre[s][i][h]["rhs"]) for h in heads] for i in range(len(chunks))]
        pending = recurrence_thunks(chunks, pre[s], sols)
    for th in pending:
        th()
    for h in heads:
        s_ref[h] = states[h]

    xbuf[0:TAIL, :] = xbuf[STEP:STEP + TAIL, :]

    @pl.when(step == pl.num_programs(1) - 1)
    def _():
        sfin_ref[0] = s_ref[...]


def _gdn_prompt(proj, small, cw, par, nw, bsz, seq):
    nsteps = seq // STEP
    row = lambda b, s: b * nsteps + s
    return pl.pallas_call(
        _gdn_prompt_body,
        grid=(bsz, nsteps),
        in_specs=[pl.BlockSpec((STEP, GDN_CH), lambda b, s: (row(b, s), OFF_QKV // GDN_CH)),
                  pl.BlockSpec((STEP, GDN_V), lambda b, s: (row(b, s), OFF_GATE // GDN_V)),
                  pl.BlockSpec((STEP, SMALL_COLS), lambda b, s: (row(b, s), 0)),
                  pl.BlockSpec((CONV_W, GDN_CH), lambda b, s: (0, 0)),
                  pl.BlockSpec((8, SMALL_COLS), lambda b, s: (0, 0)),
                  pl.BlockSpec((1, GDN_DV), lambda b, s: (0, 0))],
        out_specs=[pl.BlockSpec((STEP, GDN_V), lambda b, s: (row(b, s), 0)),
                   pl.BlockSpec((1, GDN_HEADS, GDN_DK, GDN_DV), lambda b, s: (b, 0, 0, 0))],
        out_shape=[jax.ShapeDtypeStruct((bsz * seq, GDN_V), BF16),
                   jax.ShapeDtypeStruct((bsz, GDN_HEADS, GDN_DK, GDN_DV), F32)],
        scratch_shapes=[pltpu.VMEM((TAIL + STEP, GDN_CH), F32),
                        pltpu.VMEM((GDN_HEADS, GDN_DK, GDN_DV), F32)],
        compiler_params=_params(2),
        name="gdn_prompt",
    )(proj, proj, small, cw, par, nw)


def _ssd_prompt_body(xs_ref, b_ref, c_ref, z_ref, sm_ref, cw_ref, cb_ref, par_ref, d_ref, nw_ref,
                     y_ref, sfin_ref, xb_x, xb_b, xb_c, st_ref):
    step = pl.program_id(1)

    @pl.when(step == 0)
    def _():
        xb_x[0:TAIL, :] = jnp.zeros((TAIL, SSM_DINNER), F32)
        xb_b[0:TAIL, :] = jnp.zeros((TAIL, SSM_BC), F32)
        xb_c[0:TAIL, :] = jnp.zeros((TAIL, SSM_BC), F32)
        st_ref[...] = jnp.zeros_like(st_ref)

    xb_x[TAIL:TAIL + STEP, :] = xs_ref[...].astype(F32)
    xb_b[TAIL:TAIL + STEP, :] = b_ref[...].astype(F32)
    xb_c[TAIL:TAIL + STEP, :] = c_ref[...].astype(F32)

    def conv(buf, r0, lo, width, woff):
        wl = woff + lo
        acc = cw_ref[CONV_W - 1:CONV_W, wl:wl + width] * buf[pl.ds(TAIL + r0, CHUNK), lo:lo + width]
        acc = acc + cb_ref[0:1, wl:wl + width]
        for j in range(CONV_W - 1):
            acc = acc + cw_ref[j:j + 1, wl:wl + width] * buf[pl.ds(TAIL - (CONV_W - 1) + j + r0, CHUNK), lo:lo + width]
        return _silu(acc)

    sm = sm_ref[...]
    dt_all = _softplus(sm + par_ref[0:1, :])
    cs_all = _chunk_cumsum(dt_all * (-jnp.exp(par_ref[1:2, :])))
    cs_t = cs_all.T
    tril, _, _ = _chunk_masks()

    assert SSM_P == CHUNK
    pw = 2 * SSM_P
    npair = SSM_R // 2
    items = [(c, g) for c in range(STEP // CHUNK) for g in range(SSM_GROUPS)]
    pairs = [(c, g, pr) for c, g in items for pr in range(npair)]
    lane_id = lax.broadcasted_iota(jnp.int32, (CHUNK, pw), 1)
    first = lane_id < SSM_P
    first_row = first[0:1]
    r2 = lax.broadcasted_iota(jnp.int32, (pw, pw), 0)
    c2 = lax.broadcasted_iota(jnp.int32, (pw, pw), 1)
    same_head = (r2 < SSM_P) == (c2 < SSM_P)
    tril_pair = lax.broadcasted_iota(jnp.int32, (CHUNK, pw), 0) >= (lane_id % SSM_P)

    def pair_cols(src, c, g, pr):
        l0 = LANE_DT + SSM_R * g + 2 * pr
        r0 = c * CHUNK
        return jnp.where(first, src[r0:r0 + CHUNK, l0:l0 + 1], src[r0:r0 + CHUNK, l0 + 1:l0 + 2])

    def pair_row(row0, row1):
        return jnp.where(first_row, row0, row1)

    dts, cscs, csrs, csls, lms = [], [], [], [], []

    def gating(c, g, pr):
        l0 = LANE_DT + SSM_R * g + 2 * pr
        r0 = c * CHUNK
        seg = (r0 // pw) * pw
        t0 = cs_t[l0:l0 + 1, seg:seg + pw]
        t1 = cs_t[l0 + 1:l0 + 2, seg:seg + pw]
        if r0 == seg:
            csr = pair_row(t0, pltpu.roll(t1, CHUNK, 1))
        else:
            csr = pair_row(pltpu.roll(t0, CHUNK, 1), t1)
        last = r0 + CHUNK - 1
        csc = pair_cols(cs_all, c, g, pr)
        dts.append(pair_cols(dt_all, c, g, pr))
        cscs.append(csc)
        csrs.append(csr)
        csls.append(pair_row(cs_all[last:last + 1, l0:l0 + 1], cs_all[last:last + 1, l0 + 1:l0 + 2]))
        lms.append(_decay_matrix(csc, csr, tril_pair))

    for p in pairs:
        gating(*p)
    bcs = [conv(xb_b, c * CHUNK, g * SSM_N, SSM_N, SSM_DINNER) for c, g in items]
    ccs = [conv(xb_c, c * CHUNK, g * SSM_N, SSM_N, SSM_DINNER + SSM_BC) for c, g in items]
    xss = [conv(xb_x, c * CHUNK, g * SSM_GW + pr * pw, pw, 0) for c, g, pr in pairs]
    cbs = [_dot_nt(cc, jnp.concatenate([bc, bc], axis=0)) for cc, bc in zip(ccs, bcs)]
    xdts = [xs * dt for xs, dt in zip(xss, dts)]
    ms = [cbs[i // npair] * lm for i, lm in enumerate(lms)]
    bds = [jnp.where(same_head, jnp.concatenate([x, x], axis=0), 0.0) for x in xdts]
    ylocs = []
    for (c, g, pr), m, bd, xs in zip(pairs, ms, bds, xss):
        lo = g * SSM_GW + pr * pw
        ylocs.append(_dot(m, bd) + d_ref[0:1, lo:lo + pw] * xs)
    ecss = [jnp.exp(csc) for csc in cscs]
    xds = [xdt * jnp.exp(csl - csc) for xdt, csl, csc in zip(xdts, csls, cscs)]
    dchs = [jnp.exp(csl) for csl in csls]

    def group_cat(vals, i):
        return jnp.concatenate(vals[i * npair:(i + 1) * npair], axis=1)

    outs = []
    for i, (c, g) in enumerate(items):
        st_g = st_ref[g]
        outs.append(group_cat(ylocs, i) + _dot(ccs[i], st_g) * group_cat(ecss, i))
        st_ref[g] = st_g * group_cat(dchs, i) + _dot_tn(bcs[i], group_cat(xds, i))
    for (c, g), y_g in zip(items, outs):
        r0 = c * CHUNK
        y_g = y_g * _silu(z_ref[r0:r0 + CHUNK, g * SSM_GW:(g + 1) * SSM_GW].astype(F32))
        y_ref[r0:r0 + CHUNK, g * SSM_GW:(g + 1) * SSM_GW] = _rmsnorm(
            y_g, nw_ref[0:1, g * SSM_GW:(g + 1) * SSM_GW]).astype(y_ref.dtype)

    xb_x[0:TAIL, :] = xb_x[STEP:STEP + TAIL, :]
    xb_b[0:TAIL, :] = xb_b[STEP:STEP + TAIL, :]
    xb_c[0:TAIL, :] = xb_c[STEP:STEP + TAIL, :]

    @pl.when(step == pl.num_programs(1) - 1)
    def _():
        for g in range(SSM_GROUPS):
            st_t = st_ref[g].T
            for rr in range(SSM_R):
                sfin_ref[0, SSM_R * g + rr] = st_t[rr * SSM_P:(rr + 1) * SSM_P, :]


def _ssd_prompt(proj, small, cw, cb, par, d_row, nw, bsz, seq):
    nsteps = seq // STEP
    row = lambda b, s: b * nsteps + s
    const = lambda b, s: (0, 0)
    return pl.pallas_call(
        _ssd_prompt_body,
        grid=(bsz, nsteps),
        in_specs=[pl.BlockSpec((STEP, SSM_DINNER), lambda b, s: (row(b, s), OFF_XS // SSM_DINNER)),
                  pl.BlockSpec((STEP, SSM_BC), lambda b, s: (row(b, s), OFF_B // SSM_BC)),
                  pl.BlockSpec((STEP, SSM_BC), lambda b, s: (row(b, s), OFF_C // SSM_BC)),
                  pl.BlockSpec((STEP, SSM_DINNER), lambda b, s: (row(b, s), OFF_Z // SSM_DINNER)),
                  pl.BlockSpec((STEP, SMALL_COLS), lambda b, s: (row(b, s), 0)),
                  pl.BlockSpec((CONV_W, SSM_CH), const),
                  pl.BlockSpec((1, SSM_CH), const),
                  pl.BlockSpec((8, SMALL_COLS), const),
                  pl.BlockSpec((1, SSM_DINNER), const),
                  pl.BlockSpec((1, SSM_DINNER), const)],
        out_specs=[pl.BlockSpec((STEP, SSM_DINNER), lambda b, s: (row(b, s), 0)),
                   pl.BlockSpec((1, SSM_HEADS, SSM_P, SSM_N), lambda b, s: (b, 0, 0, 0))],
        out_shape=[jax.ShapeDtypeStruct((bsz * seq, SSM_DINNER), BF16),
                   jax.ShapeDtypeStruct((bsz, SSM_HEADS, SSM_P, SSM_N), F32)],
        scratch_shapes=[pltpu.VMEM((TAIL + STEP, SSM_DINNER), F32),
                        pltpu.VMEM((TAIL + STEP, SSM_BC), F32),
                        pltpu.VMEM((TAIL + STEP, SSM_BC), F32),
                        pltpu.VMEM((SSM_GROUPS, SSM_N, SSM_GW), F32)],
        compiler_params=_params(2),
        name="ssd_prompt",
    )(proj, proj, proj, proj, small, cw, cb, par, d_row, nw)


def _gdn_sample_body(qkv_ref, gate_ref, sm_ref, cst_ref, s_ref, cw_ref, par_ref, nw_ref, o_ref, sout_ref):
    bt = qkv_ref.shape[0]
    sm = sm_ref[...]
    beta_all = _sigmoid(sm)
    eg_all = jnp.exp(-jnp.exp(par_ref[0:1, :]) * _softplus(sm + par_ref[1:2, :]))
    rowid = lax.broadcasted_iota(jnp.int32, (bt, GDN_DK), 0)
    nw = nw_ref[...]

    def conv(blk):
        lo = blk * 128
        acc = cw_ref[CONV_W - 1:CONV_W, lo:lo + 128] * qkv_ref[:, lo:lo + 128]
        for j in range(CONV_W - 1):
            acc = acc + cw_ref[j:j + 1, lo:lo + 128] * cst_ref[:, j * GDN_CH + lo:j * GDN_CH + lo + 128]
        return _silu(acc)

    for h in range(GDN_HEADS):
        q = conv(h)
        k = conv(GDN_HEADS + h)
        v = conv(2 * GDN_HEADS + h)
        q = q * lax.rsqrt(jnp.sum(q * q, axis=-1, keepdims=True) + EPS) * (GDN_DK ** -0.5)
        k = k * lax.rsqrt(jnp.sum(k * k, axis=-1, keepdims=True) + EPS)
        bcol = beta_all[:, LANE_BETA + h:LANE_BETA + h + 1]
        egcol = eg_all[:, LANE_GDEC + h:LANE_GDEC + h + 1]
        qs_rows, ks_rows = [], []
        for t in range(bt):
            lhs = jnp.where(rowid == 0, q[t:t + 1, :], jnp.where(rowid == 1, k[t:t + 1, :], 0.0))
            r = _dot(lhs, s_ref[t, h])
            qs_rows.append(r[0:1])
            ks_rows.append(r[1:2])
        q_s = jnp.concatenate(qs_rows, axis=0)
        k_s = jnp.concatenate(ks_rows, axis=0)
        v_new = bcol * v - (bcol * egcol) * k_s
        o = egcol * q_s + jnp.sum(q * k, axis=-1, keepdims=True) * v_new
        for t in range(bt):
            k_t = jnp.where(rowid == t, k, 0.0)
            sout_ref[t, h] = s_ref[t, h] * egcol[t:t + 1, :] + _dot_tn(k_t, v_new)
        gt = gate_ref[:, h * GDN_DV:(h + 1) * GDN_DV]
        o_ref[:, h * GDN_DV:(h + 1) * GDN_DV] = (_rmsnorm(o, nw) * _silu(gt)).astype(o_ref.dtype)


def _gdn_sample(proj, small, cstate, state, layer, cw, par, nw):
    t = proj.shape[0]
    bt = GDN_SAMPLE_BT
    blk0 = layer * (t // bt)
    const = lambda i: (0, 0)
    return pl.pallas_call(
        _gdn_sample_body,
        grid=(t // bt,),
        in_specs=[pl.BlockSpec((bt, GDN_CH), lambda i: (i, OFF_QKV // GDN_CH)),
                  pl.BlockSpec((bt, GDN_V), lambda i: (i, OFF_GATE // GDN_V)),
                  pl.BlockSpec((bt, SMALL_COLS), lambda i: (i, 0)),
                  pl.BlockSpec((bt, (CONV_W - 1) * GDN_CH), lambda i: (blk0 + i, 0)),
                  pl.BlockSpec((bt, GDN_HEADS, GDN_DK, GDN_DV), lambda i: (blk0 + i, 0, 0, 0)),
                  pl.BlockSpec((CONV_W, GDN_CH), const),
                  pl.BlockSpec((8, SMALL_COLS), const),
                  pl.BlockSpec((1, GDN_DV), const)],
        out_specs=[pl.BlockSpec((bt, GDN_V), lambda i: (i, 0)),
                   pl.BlockSpec((bt, GDN_HEADS, GDN_DK, GDN_DV), lambda i: (i, 0, 0, 0))],
        out_shape=[jax.ShapeDtypeStruct((t, GDN_V), BF16),
                   jax.ShapeDtypeStruct((t,) + state.shape[1:], F32)],
        compiler_params=_params(1),
        name="gdn_sample",
    )(proj, proj, small, cstate, state, cw, par, nw)


def _ssd_sample_body(xs_ref, b_ref, c_ref, z_ref, sm_ref, cst_ref, s_ref, cw_ref, cb_ref, par_ref, d_ref, nw_ref,
                     y_ref, sout_ref):
    bt = xs_ref.shape[0]
    sm = sm_ref[...]
    dt_all = _softplus(sm + par_ref[0:1, :])
    e_all = jnp.exp(dt_all * (-jnp.exp(par_ref[1:2, :])))
    rowid_n = lax.broadcasted_iota(jnp.int32, (bt, SSM_N), 0)
    rowid_g = lax.broadcasted_iota(jnp.int32, (bt, SSM_GW), 0)

    def conv(x_ref, lo, width, woff):
        wl = woff + lo
        acc = cw_ref[CONV_W - 1:CONV_W, wl:wl + width] * x_ref[:, lo:lo + width] + cb_ref[0:1, wl:wl + width]
        for j in range(CONV_W - 1):
            acc = acc + cw_ref[j:j + 1, wl:wl + width] * cst_ref[:, j * SSM_CH + wl:j * SSM_CH + wl + width]
        return _silu(acc)

    for g in range(SSM_GROUPS):
        bc = conv(b_ref, g * SSM_N, SSM_N, SSM_DINNER)
        cc = conv(c_ref, g * SSM_N, SSM_N, SSM_DINNER + SSM_BC)
        xs_g = conv(xs_ref, g * SSM_GW, SSM_GW, 0)
        cb = jnp.sum(cc * bc, axis=-1, keepdims=True)
        yoff_rows = []
        for t in range(bt):
            c_t = jnp.where(rowid_n == 0, cc[t:t + 1, :], 0.0)
            s_tg = jnp.concatenate([s_ref[t, SSM_R * g + rr] for rr in range(SSM_R)], axis=0)
            yoff_rows.append(_dot_nt(c_t, s_tg)[0:1])
        yoff = jnp.concatenate(yoff_rows, axis=0)
        lanes = [LANE_DT + SSM_R * g + rr for rr in range(SSM_R)]
        dt_g = jnp.concatenate([jnp.broadcast_to(dt_all[:, l:l + 1], (bt, SSM_P)) for l in lanes], axis=1)
        e_g = jnp.concatenate([jnp.broadcast_to(e_all[:, l:l + 1], (bt, SSM_P)) for l in lanes], axis=1)
        xdt = xs_g * dt_g
        y = cb * xdt + yoff * e_g + d_ref[0:1, g * SSM_GW:(g + 1) * SSM_GW] * xs_g
        for t in range(bt):
            outer = _dot_tn(jnp.where(rowid_g == t, xdt, 0.0), bc)
            for rr in range(SSM_R):
                h = SSM_R * g + rr
                sout_ref[t, h] = (s_ref[t, h] * e_all[t:t + 1, lanes[rr]:lanes[rr] + 1]
                                  + outer[rr * SSM_P:(rr + 1) * SSM_P, :])
        y = y * _silu(z_ref[:, g * SSM_GW:(g + 1) * SSM_GW])
        y_ref[:, g * SSM_GW:(g + 1) * SSM_GW] = _rmsnorm(
            y, nw_ref[0:1, g * SSM_GW:(g + 1) * SSM_GW]).astype(y_ref.dtype)


def _ssd_sample(proj, small, cstate, state, layer, cw, cb, par, d_row, nw):
    t = proj.shape[0]
    bt = SAMPLE_BT
    blk0 = layer * (t // bt)
    const = lambda i: (0, 0)
    return pl.pallas_call(
        _ssd_sample_body,
        grid=(t // bt,),
        in_specs=[pl.BlockSpec((bt, SSM_DINNER), lambda i: (i, OFF_XS // SSM_DINNER)),
                  pl.BlockSpec((bt, SSM_BC), lambda i: (i, OFF_B // SSM_BC)),
                  pl.BlockSpec((bt, SSM_BC), lambda i: (i, OFF_C // SSM_BC)),
                  pl.BlockSpec((bt, SSM_DINNER), lambda i: (i, OFF_Z // SSM_DINNER)),
                  pl.BlockSpec((bt, SMALL_COLS), lambda i: (i, 0)),
                  pl.BlockSpec((bt, (CONV_W - 1) * SSM_CH), lambda i: (blk0 + i, 0)),
                  pl.BlockSpec((bt, SSM_HEADS, SSM_P, SSM_N), lambda i: (blk0 + i, 0, 0, 0)),
                  pl.BlockSpec((CONV_W, SSM_CH), const),
                  pl.BlockSpec((1, SSM_CH), const),
                  pl.BlockSpec((8, SMALL_COLS), const),
                  pl.BlockSpec((1, SSM_DINNER), const),
                  pl.BlockSpec((1, SSM_DINNER), const)],
        out_specs=[pl.BlockSpec((bt, SSM_DINNER), lambda i: (i, 0)),
                   pl.BlockSpec((bt, SSM_HEADS, SSM_P, SSM_N), lambda i: (i, 0, 0, 0))],
        out_shape=[jax.ShapeDtypeStruct((t, SSM_DINNER), BF16),
                   jax.ShapeDtypeStruct((t,) + state.shape[1:], F32)],
        compiler_params=_params(1),
        name="ssd_sample",
    )(proj, proj, proj, proj, small, cstate, state, cw, cb, par, d_row, nw)


def _merge_body(oa_ref, yb_ref, ga_ref, gb_ref, x_ref, wg_ref, ws_ref, wo_ref, o_ref):
    a = jnp.dot(oa_ref[...], wg_ref[...], preferred_element_type=F32)
    b = jnp.dot(yb_ref[...], ws_ref[...], preferred_element_type=F32)
    merged = _sigmoid(ga_ref[...].astype(F32)) * a + _sigmoid(gb_ref[...].astype(F32)) * b
    o_ref[...] = x_ref[...] + _dot(merged, wo_ref[...])


def _merge(o_a, y_b, proj, x, wg, ws, wo, *, tm):
    t = x.shape[0]
    tm = min(tm, t)
    assert t % tm == 0
    const = lambda i: (0, 0)
    return pl.pallas_call(
        _merge_body,
        grid=(t // tm,),
        in_specs=[pl.BlockSpec((tm, GDN_V), lambda i: (i, 0)),
                  pl.BlockSpec((tm, SSM_DINNER), lambda i: (i, 0)),
                  pl.BlockSpec((tm, D_MODEL), lambda i: (i, OFF_GA // D_MODEL)),
                  pl.BlockSpec((tm, D_MODEL), lambda i: (i, OFF_GB // D_MODEL)),
                  pl.BlockSpec((tm, D_MODEL), lambda i: (i, 0)),
                  pl.BlockSpec((GDN_V, D_MODEL), const),
                  pl.BlockSpec((SSM_DINNER, D_MODEL), const),
                  pl.BlockSpec((D_MODEL, D_MODEL), const)],
        out_specs=pl.BlockSpec((tm, D_MODEL), lambda i: (i, 0)),
        out_shape=jax.ShapeDtypeStruct((t, D_MODEL), F32),
        compiler_params=_params(1),
        name="merge_out",
    )(o_a, y_b, proj, proj, x, wg, ws, wo)


def _softmax_rows(s):
    e = jnp.exp(s - jnp.max(s, axis=-1, keepdims=True))
    return e / jnp.sum(e, axis=-1, keepdims=True)


def _xattn_prompt_body(x_ref, gx_ref, wq_ref, wo_ref, k_ref, v_ref, o_ref):
    x = x_ref[...]
    q = _dot(_rmsnorm(x, gx_ref[...]), wq_ref[...])
    outs = []
    for h in range(X_HEADS):
        sl = slice(h * X_HD, (h + 1) * X_HD)
        p = _softmax_rows(_dot_nt(q[:, sl], k_ref[:, sl]) * (X_HD ** -0.5))
        outs.append(_dot(p, v_ref[:, sl]))
    o_ref[...] = x + _dot(jnp.concatenate(outs, axis=1), wo_ref[...])


def _xattn_prompt(x, gx, wq, wo, mk, mv, seq, *, tm):
    t = x.shape[0]
    n_mem = mk.shape[0] // (t // seq)
    tm = min(tm, seq)
    assert seq % tm == 0
    per_seq = seq // tm
    const = lambda i: (0, 0)
    return pl.pallas_call(
        _xattn_prompt_body,
        grid=(t // tm,),
        in_specs=[pl.BlockSpec((tm, D_MODEL), lambda i: (i, 0)),
                  pl.BlockSpec((1, D_MODEL), const),
                  pl.BlockSpec((D_MODEL, D_MODEL), const),
                  pl.BlockSpec((D_MODEL, D_MODEL), const),
                  pl.BlockSpec((n_mem, D_MODEL), lambda i: (i // per_seq, 0)),
                  pl.BlockSpec((n_mem, D_MODEL), lambda i: (i // per_seq, 0))],
        out_specs=pl.BlockSpec((tm, D_MODEL), lambda i: (i, 0)),
        out_shape=jax.ShapeDtypeStruct((t, D_MODEL), F32),
        compiler_params=_params(1),
        name="xattn_prompt",
    )(x, gx.reshape(1, D_MODEL), wq, wo, mk, mv)


def _xattn_sample_body(x_ref, gx_ref, wq_ref, wo_ref, k_ref, v_ref, o_ref, q_scr, a_scr):
    i = pl.program_id(0)
    bt = k_ref.shape[0]

    @pl.when(i == 0)
    def _():
        q_scr[...] = _dot(_rmsnorm(x_ref[...], gx_ref[...]), wq_ref[...])

    nrow = k_ref.shape[1]
    sub = lax.broadcasted_iota(jnp.int32, (8, nrow), 0)
    lane = lax.broadcasted_iota(jnp.int32, (8, nrow), 1)
    own = (lane % 8) == sub
    lower = sub < X_HEADS
    toks = range(bt)
    q_rows = []
    for tt in toks:
        q_t = q_scr[pl.ds(i * bt + tt, 1), :]
        q_rows.append(jnp.concatenate(
            [q_t[:, h * X_HD + half * 128:h * X_HD + (half + 1) * 128]
             for half in range(X_HD // 128) for h in range(X_HEADS)], axis=0))
    es = [jnp.where(own, _dot_nt(q_rows[tt], k_ref[tt]), 0.0) for tt in toks]
    p2s = []
    for e in es:
        s = (e + pltpu.roll(pltpu.roll(e, X_HEADS, 0), nrow - X_HEADS, 1)) * (X_HD ** -0.5)
        s = jnp.where(own, s, -jnp.inf)
        p = jnp.exp(s - jnp.max(s, axis=-1, keepdims=True))
        p = jnp.where(lower, p / jnp.sum(p, axis=-1, keepdims=True), 0.0)
        p2s.append(p + pltpu.roll(pltpu.roll(p, X_HEADS, 0), X_HEADS, 1))
    o2s = [_dot(p2s[tt], v_ref[tt]) for tt in toks]
    for tt in toks:
        a_scr[pl.ds(i * bt + tt, 1), :] = jnp.concatenate(
            [o2s[tt][half * X_HEADS + h:half * X_HEADS + h + 1, :]
             for h in range(X_HEADS) for half in range(X_HD // 128)], axis=1)

    @pl.when(i == pl.num_programs(0) - 1)
    def _():
        o_ref[...] = x_ref[...] + _dot(a_scr[...], wo_ref[...])


def _xattn_sample(x, gx, wq, wo, ck, cv, layer):
    t = x.shape[0]
    n_mem = ck.shape[1]
    bt = XATTN_BT
    blk0 = layer * (t // bt)
    const = lambda i: (0, 0)
    return pl.pallas_call(
        _xattn_sample_body,
        grid=(t // bt,),
        in_specs=[pl.BlockSpec((t, D_MODEL), const),
                  pl.BlockSpec((1, D_MODEL), const),
                  pl.BlockSpec((D_MODEL, D_MODEL), const),
                  pl.BlockSpec((D_MODEL, D_MODEL), const),
                  pl.BlockSpec((bt,) + ck.shape[1:], lambda i: (blk0 + i, 0, 0)),
                  pl.BlockSpec((bt,) + cv.shape[1:], lambda i: (blk0 + i, 0, 0))],
        out_specs=pl.BlockSpec((t, D_MODEL), const),
        out_shape=jax.ShapeDtypeStruct((t, D_MODEL), F32),
        scratch_shapes=[pltpu.VMEM((t, D_MODEL), F32), pltpu.VMEM((t, D_MODEL), F32)],
        compiler_params=_params(1),
        name="xattn_sample",
    )(x, gx.reshape(1, D_MODEL), wq, wo, ck, cv)


def _mlp_body(x_ref, g_ref, wu_ref, wd_ref, gf_ref, y_ref, hn_ref, acc_ref, *, final_norm):
    j = pl.program_id(1)

    @pl.when(j == 0)
    def _():
        hn_ref[...] = _rmsnorm(x_ref[...], g_ref[...]).astype(BF16)
        acc_ref[...] = jnp.zeros_like(acc_ref)

    hf = jnp.dot(hn_ref[...], wu_ref[...], preferred_element_type=F32)
    acc_ref[...] += _dot(jnp.square(jnp.maximum(hf, 0.0)), wd_ref[...])

    @pl.when(j == pl.num_programs(1) - 1)
    def _():
        x_new = x_ref[...] + acc_ref[...]
        y_ref[...] = _rmsnorm(x_new, gf_ref[...]) if final_norm else x_new


def _mlp(x, g, wu, wd, gf, *, tm, tf, final_norm):
    t = x.shape[0]
    tm = min(tm, t)
    assert t % tm == 0 and D_FF % tf == 0
    const = lambda i, j: (0, 0)
    return pl.pallas_call(
        functools.partial(_mlp_body, final_norm=final_norm),
        grid=(t // tm, D_FF // tf),
        in_specs=[pl.BlockSpec((tm, D_MODEL), lambda i, j: (i, 0)),
                  pl.BlockSpec((1, D_MODEL), const),
                  pl.BlockSpec((D_MODEL, tf), lambda i, j: (0, j)),
                  pl.BlockSpec((tf, D_MODEL), lambda i, j: (j, 0)),
                  pl.BlockSpec((1, D_MODEL), const)],
        out_specs=pl.BlockSpec((tm, D_MODEL), lambda i, j: (i, 0)),
        out_shape=jax.ShapeDtypeStruct((t, D_MODEL), F32),
        scratch_shapes=[pltpu.VMEM((tm, D_MODEL), BF16), pltpu.VMEM((tm, D_MODEL), F32)],
        compiler_params=_params(2),
        name="mlp",
    )(x, g.reshape(1, D_MODEL), wu, wd, gf.reshape(1, D_MODEL))


def _lane_row(vec, lane0):
    return jnp.zeros((SMALL_COLS,), F32).at[lane0:lane0 + vec.shape[0]].set(vec.astype(F32))


def _param_rows(*rows):
    out = jnp.zeros((8, SMALL_COLS), F32)
    for i, r in enumerate(rows):
        out = out.at[i].set(r)
    return out


def kernel(x_prompt, x_sample, mem_prompt, state_gdn_conv, state_gdn, state_ssm_conv, state_ssm, cache_mem_k, cache_mem_v, g_mix, w_in, gdn_conv_w, gdn_A_log, gdn_dt_bias, gdn_norm_w, w_gdn_up, ssm_conv_w, ssm_conv_b, ssm_dt_bias, ssm_A_log, ssm_D, ssm_norm_w, w_ssm_up, w_out, g_mem, w_mk, w_mv, g_x, w_cq, w_co, g_ff, w_ff_up, w_ff_down, g_final):
    bp, seq, _ = x_prompt.shape
    bs, dec_seq, _ = x_sample.shape
    depth = w_in.shape[0]
    n_mem = mem_prompt.shape[1]
    assert seq % STEP == 0 and dec_seq == 1
    assert bs % SAMPLE_BT == 0 and bs % GDN_SAMPLE_BT == 0 and bs % XATTN_BT == 0
    tp = bp * seq

    xp = x_prompt.reshape(tp, D_MODEL)
    xs = x_sample.reshape(bs, D_MODEL)
    mem = mem_prompt.reshape(bp * n_mem, D_MODEL)
    gst_all = state_gdn_conv.astype(F32).reshape(depth * bs, (CONV_W - 1) * GDN_CH)
    sst_all = state_ssm_conv.astype(F32).reshape(depth * bs, (CONV_W - 1) * SSM_CH)
    sg_all = state_gdn.astype(F32).reshape(depth * bs, GDN_HEADS, GDN_DK, GDN_DV)
    ss_all = state_ssm.astype(F32).reshape(depth * bs, SSM_HEADS, SSM_P, SSM_N)
    def cache_rows(c):
        c = c.astype(F32).reshape(depth * bs, n_mem, X_HEADS, X_HD // 128, 128)
        return c.transpose(0, 1, 3, 2, 4).reshape(depth * bs, n_mem * X_HEADS * (X_HD // 128), 128)

    ck_all = cache_rows(cache_mem_k)
    cv_all = cache_rows(cache_mem_v)
    outs ={k: [] for k in ("p_gc", "p_g", "p_sc", "p_s", "p_mk", "p_mv", "s_gc", "s_g", "s_sc", "s_s")}

    starts = [0]
    for width in IN_SPLITS:
        starts.append(starts[-1] + width)
    col = lambda i: slice(starts[i], starts[i + 1])

    for l in range(depth):
        wl = w_in[l]
        w_main = jnp.concatenate([wl[:, starts[0]:starts[2]], wl[:, starts[4]:starts[6]],
                                  wl[:, starts[7]:starts[9]]], axis=1).astype(BF16)
        w_small = jnp.concatenate(
            [wl[:, col(2)], wl[:, col(3)], wl[:, col(6)],
             jnp.zeros((D_MODEL, SMALL_COLS - 2 * GDN_HEADS - SSM_HEADS), F32)], axis=1).astype(BF16)
        gdn_par = _param_rows(_lane_row(gdn_A_log[l], LANE_GDEC), _lane_row(gdn_dt_bias[l], LANE_GDEC))
        ssm_par = _param_rows(_lane_row(ssm_dt_bias[l], LANE_DT), _lane_row(ssm_A_log[l], LANE_DT))
        gdn_nw = gdn_norm_w[l].reshape(1, GDN_DV).astype(F32)
        ssm_nw = ssm_norm_w[l].reshape(1, SSM_DINNER).astype(F32)
        d_row = jnp.repeat(ssm_D[l].astype(F32), SSM_P).reshape(1, SSM_DINNER)
        gcw = gdn_conv_w[l].astype(F32)
        scw = ssm_conv_w[l].astype(F32)
        scb = ssm_conv_b[l].reshape(1, SSM_CH).astype(F32)
        wg, ws, wo = w_gdn_up[l].astype(BF16), w_ssm_up[l].astype(BF16), w_out[l].astype(BF16)
        wq, wco = w_cq[l].astype(BF16), w_co[l].astype(BF16)
        wu, wd = w_ff_up[l].astype(BF16), w_ff_down[l].astype(BF16)
        last = l == depth - 1

        mk = _rms_mm(mem, g_mem[l], w_mk[l].astype(BF16), tm=1024, tn=1024)
        mv = _rms_mm(mem, g_mem[l], w_mv[l].astype(BF16), tm=1024, tn=1024)
        outs["p_mk"].append(mk.reshape(bp, n_mem, X_HEADS, X_HD))
        outs["p_mv"].append(mv.reshape(bp, n_mem, X_HEADS, X_HD))

        proj, small = _in_proj(xp, g_mix[l], w_main, w_small, tm=2048, tn=1536, out_dtype=BF16)
        o_a, p_g = _gdn_prompt(proj, small, gcw, gdn_par, gdn_nw, bp, seq)
        y_b, p_s = _ssd_prompt(proj, small, scw, scb, ssm_par, d_row, ssm_nw, bp, seq)
        proj3 = proj.reshape(bp, seq, MAIN_COLS)
        outs["p_gc"].append(proj3[:, seq - (CONV_W - 1):, OFF_QKV:OFF_QKV + GDN_CH].astype(F32))
        outs["p_sc"].append(proj3[:, seq - (CONV_W - 1):, OFF_XS:OFF_XS + SSM_CH].astype(F32))
        outs["p_g"].append(p_g)
        outs["p_s"].append(p_s)
        xp = _merge(o_a, y_b, proj, xp, wg, ws, wo, tm=512)
        xp = _xattn_prompt(xp, g_x[l], wq, wco, mk, mv, seq, tm=1024)
        xp = _mlp(xp, g_ff[l], wu, wd, g_final, tm=1024, tf=1024, final_norm=last)

        proj, small = _in_proj(xs, g_mix[l], w_main, w_small, tm=128, tn=2048, out_dtype=F32)
        o_a, s_g = _gdn_sample(proj, small, gst_all, sg_all, l, gcw, gdn_par, gdn_nw)
        y_b, s_s = _ssd_sample(proj, small, sst_all, ss_all, l, scw, scb, ssm_par, d_row, ssm_nw)
        outs["s_gc"].append(jnp.concatenate(
            [state_gdn_conv[l, :, 1:].astype(F32), proj[:, None, OFF_QKV:OFF_QKV + GDN_CH]], axis=1))
        outs["s_sc"].append(jnp.concatenate(
            [state_ssm_conv[l, :, 1:].astype(F32), proj[:, None, OFF_XS:OFF_XS + SSM_CH]], axis=1))
        outs["s_g"].append(s_g)
        outs["s_s"].append(s_s)
        xs = _merge(o_a, y_b, proj, xs, wg, ws, wo, tm=128)
        xs = _xattn_sample(xs, g_x[l], wq, wco, ck_all, cv_all, l)
        xs = _mlp(xs, g_ff[l], wu, wd, g_final, tm=128, tf=2048, final_norm=last)

    stack = lambda k: jnp.stack(outs[k])
    return (xp.reshape(bp, seq, D_MODEL), xs.reshape(bs, dec_seq, D_MODEL),
            stack("p_gc"), stack("p_g"), stack("p_sc"), stack("p_s"), stack("p_mk"), stack("p_mv"),
            stack("s_gc"), stack("s_g"), stack("s_sc"), stack("s_s"))
```

```python
import functools

import jax
import jax.numpy as jnp
from jax import lax
from jax.experimental import pallas as pl
from jax.experimental.pallas import tpu as pltpu

F32 = jnp.float32
BF16 = jnp.bfloat16

EPS = 1e-6
CHUNK = 64
CONV_W = 4
D_MODEL = 1024
GDN_HEADS = 8
GDN_DK = 128
GDN_DV = 128
GDN_QK = GDN_HEADS * GDN_DK
GDN_V = GDN_HEADS * GDN_DV
GDN_CH = 2 * GDN_QK + GDN_V
SSM_DINNER = 2 * D_MODEL
SSM_P = 64
SSM_HEADS = SSM_DINNER // SSM_P
SSM_GROUPS = 8
SSM_R = SSM_HEADS // SSM_GROUPS
SSM_N = 128
SSM_BC = SSM_GROUPS * SSM_N
SSM_CH = SSM_DINNER + 2 * SSM_BC
SSM_GW = SSM_R * SSM_P
X_HEADS = 4
X_HD = D_MODEL // X_HEADS
D_FF = 4 * D_MODEL
IN_SPLITS = (GDN_CH, GDN_V, GDN_HEADS, GDN_HEADS, SSM_DINNER, SSM_CH, SSM_HEADS, D_MODEL, D_MODEL)

MAIN_COLS = GDN_CH + GDN_V + SSM_DINNER + SSM_CH + 2 * D_MODEL
OFF_QKV, OFF_GATE, OFF_Z = 0, GDN_CH, GDN_CH + GDN_V
OFF_XS = OFF_Z + SSM_DINNER
OFF_B = OFF_XS + SSM_DINNER
OFF_C = OFF_B + SSM_BC
OFF_GA = OFF_C + SSM_BC
OFF_GB = OFF_GA + D_MODEL
LANES = 128
SUBLANES = 8
X_HALVES = X_HD // LANES
X_ROWS = X_HEADS * X_HALVES
SMALL_COLS = LANES
LANE_BETA, LANE_GDEC, LANE_DT = 0, GDN_HEADS, 2 * GDN_HEADS

STEP = 4 * CHUNK
GDN_SET = 2
TAIL = SUBLANES
SAMPLE_BT = 8
GDN_SAMPLE_BT = 16
XATTN_BT = 8
VMEM_LIMIT = 54 * 1024 * 1024
PROMPT_TILES = dict(kv=(1024, 1024), in_proj=(2048, 1536), merge=512, xattn=1024, mlp=(1024, 1024))
SAMPLE_TILES = dict(in_proj=(128, 2048), merge=128, mlp=(128, 2048))


def _params(n_axes):
    return pltpu.CompilerParams(dimension_semantics=("arbitrary",) * n_axes, vmem_limit_bytes=VMEM_LIMIT)


def _sigmoid(x):
    return 0.5 * jnp.tanh(0.5 * x) + 0.5


def _silu(x):
    h = 0.5 * x
    return h + h * jnp.tanh(h)


def _softplus(x):
    return jnp.maximum(x, 0.0) + jnp.log1p(jnp.exp(-jnp.abs(x)))


def _rmsnorm(x, g):
    xf = x.astype(F32)
    return xf * lax.rsqrt(jnp.mean(xf * xf, axis=-1, keepdims=True) + EPS) * g


def _dot(a, b):
    return jnp.dot(a.astype(BF16), b.astype(BF16), preferred_element_type=F32)


def _dot_nt(a, b):
    return lax.dot_general(a.astype(BF16), b.astype(BF16), (((1,), (1,)), ((), ())), preferred_element_type=F32)


def _dot_tn(a, b):
    return lax.dot_general(a.astype(BF16), b.astype(BF16), (((0,), (0,)), ((), ())), preferred_element_type=F32)


def _chunk_cumsum(x):
    n = x.shape[0]
    r = lax.broadcasted_iota(jnp.int32, (n, n), 0)
    c = lax.broadcasted_iota(jnp.int32, (n, n), 1)
    tri = jnp.where((r >= c) & ((r // CHUNK) == (c // CHUNK)), 1.0, 0.0).astype(BF16)
    h1 = x.astype(BF16)
    r1 = x - h1.astype(F32)
    h2 = r1.astype(BF16)
    h3 = (r1 - h2.astype(F32)).astype(BF16)
    d = functools.partial(jnp.dot, preferred_element_type=F32)
    return d(tri, h1) + (d(tri, h2) + d(tri, h3))


def _chunk_masks():
    r = lax.broadcasted_iota(jnp.int32, (CHUNK, CHUNK), 0)
    c = lax.broadcasted_iota(jnp.int32, (CHUNK, CHUNK), 1)
    return r >= c, r > c, jnp.where(r == c, 1.0, 0.0).astype(F32)


def _decay_matrix(col, row, tril):
    return jnp.where(tril, jnp.exp(jnp.where(tril, col - row, 0.0)), 0.0)


INV_BASE = 8


def _mm_shared(lhs_list, b):
    m = lhs_list[0].shape[0]
    out = _dot(jnp.concatenate(lhs_list, axis=0), b)
    return [out[i * m:(i + 1) * m] for i in range(len(lhs_list))]


def _inv_unit_lower_levels(a_list, eye, out):
    size = a_list[0].shape[0]
    r = lax.broadcasted_iota(jnp.int32, (size, size), 0)
    c = lax.broadcasted_iota(jnp.int32, (size, size), 1)
    same = (r // INV_BASE) == (c // INV_BASE)
    n = [jnp.where(same, -a, 0.0) for a in a_list]
    t = [eye + ni for ni in n]
    p = [_dot(ni, ni) for ni in n]
    yield
    terms = 2
    while terms < INV_BASE:
        terms *= 2
        if terms < INV_BASE:
            res = [_mm_shared([pi, ti], pi) for pi, ti in zip(p, t)]
            t = [ti + ri[1] for ti, ri in zip(t, res)]
            p = [ri[0] for ri in res]
        else:
            t = [ti + _dot(ti, pi) for pi, ti in zip(p, t)]
        yield
    b = INV_BASE
    while b < size:
        low = ((r // (2 * b)) == (c // (2 * b))) & ((r // b) % 2 == 1) & ((c // b) % 2 == 0)
        y = [_dot(jnp.where(low, a, 0.0), ti) for a, ti in zip(a_list, t)]
        yield
        t = [ti - _dot(ti, yi) for ti, yi in zip(t, y)]
        yield
        b *= 2
    out.extend(t)


def _interleave(gen, thunks, gen_steps):
    thunks = list(thunks)
    per_step = -(-len(thunks) // gen_steps)
    for _ in gen:
        for th in thunks[:per_step]:
            th()
        thunks = thunks[per_step:]
    for th in thunks:
        th()


def _rms_mm_body(x_ref, g_ref, w_ref, o_ref, hn_ref):
    @pl.when(pl.program_id(1) == 0)
    def _():
        hn_ref[...] = _rmsnorm(x_ref[...], g_ref[...]).astype(BF16)

    o_ref[...] = jnp.dot(hn_ref[...], w_ref[...], preferred_element_type=F32).astype(o_ref.dtype)


def _rms_mm(x, g, w, *, tm, tn, out_dtype=F32):
    t, k = x.shape
    n = w.shape[1]
    tm, tn = min(tm, t), min(tn, n)
    assert t % tm == 0 and n % tn == 0
    return pl.pallas_call(
        _rms_mm_body,
        grid=(t // tm, n // tn),
        in_specs=[pl.BlockSpec((tm, k), lambda i, j: (i, 0)),
                  pl.BlockSpec((1, k), lambda i, j: (0, 0)),
                  pl.BlockSpec((k, tn), lambda i, j: (0, j))],
        out_specs=pl.BlockSpec((tm, tn), lambda i, j: (i, j)),
        out_shape=jax.ShapeDtypeStruct((t, n), out_dtype),
        scratch_shapes=[pltpu.VMEM((tm, k), BF16)],
        compiler_params=_params(2),
        name="rms_matmul",
    )(x, g.reshape(1, k), w)


def _in_proj_body(x_ref, g_ref, w_ref, ws_ref, o_ref, os_ref, hn_ref):
    @pl.when(pl.program_id(1) == 0)
    def _():
        hn_ref[...] = _rmsnorm(x_ref[...], g_ref[...]).astype(BF16)
        os_ref[...] = jnp.dot(hn_ref[...], ws_ref[...], preferred_element_type=F32)

    o_ref[...] = jnp.dot(hn_ref[...], w_ref[...], preferred_element_type=F32).astype(o_ref.dtype)


def _in_proj(x, g, w_main, w_small, *, tm, tn, out_dtype):
    t, k = x.shape
    n, ns = w_main.shape[1], w_small.shape[1]
    tm = min(tm, t)
    assert t % tm == 0 and n % tn == 0
    return pl.pallas_call(
        _in_proj_body,
        grid=(t // tm, n // tn),
        in_specs=[pl.BlockSpec((tm, k), lambda i, j: (i, 0)),
                  pl.BlockSpec((1, k), lambda i, j: (0, 0)),
                  pl.BlockSpec((k, tn), lambda i, j: (0, j)),
                  pl.BlockSpec((k, ns), lambda i, j: (0, 0))],
        out_specs=[pl.BlockSpec((tm, tn), lambda i, j: (i, j)),
                   pl.BlockSpec((tm, ns), lambda i, j: (i, 0))],
        out_shape=[jax.ShapeDtypeStruct((t, n), out_dtype),
                   jax.ShapeDtypeStruct((t, ns), F32)],
        scratch_shapes=[pltpu.VMEM((tm, k), BF16)],
        compiler_params=_params(2),
        name="in_proj",
    )(x, g.reshape(1, k), w_main, w_small)


def _gdn_prompt_body(qkv_ref, gate_ref, sm_ref, cw_ref, par_ref, nw_ref, o_ref, sfin_ref, xbuf, s_ref):
    step = pl.program_id(1)

    @pl.when(step == 0)
    def _():
        xbuf[0:TAIL, :] = jnp.zeros((TAIL, GDN_CH), F32)
        s_ref[...] = jnp.zeros_like(s_ref)

    xbuf[TAIL:TAIL + STEP, :] = qkv_ref[...].astype(F32)
    sm = sm_ref[...]
    beta_all = _sigmoid(sm)
    g_all = -jnp.exp(par_ref[0:1, :]) * _softplus(sm + par_ref[1:2, :])
    gc_all = _chunk_cumsum(g_all)
    gc_t = gc_all.T
    tril, strict, eye = _chunk_masks()
    nw = nw_ref[...]

    def conv(blk, r0):
        ln = slice(blk * LANES, (blk + 1) * LANES)
        acc = cw_ref[CONV_W - 1:CONV_W, ln] * xbuf[pl.ds(TAIL + r0, CHUNK), ln]
        for j in range(CONV_W - 1):
            acc = acc + cw_ref[j:j + 1, ln] * xbuf[pl.ds(TAIL - (CONV_W - 1) + j + r0, CHUNK), ln]
        return _silu(acc)

    def pre_item(c, h):
        r0 = c * CHUNK
        q = conv(h, r0)
        k = conv(GDN_HEADS + h, r0)
        v = conv(2 * GDN_HEADS + h, r0)
        q = q * lax.rsqrt(jnp.sum(q * q, axis=-1, keepdims=True) + EPS) * (GDN_DK ** -0.5)
        k = k * lax.rsqrt(jnp.sum(k * k, axis=-1, keepdims=True) + EPS)
        lg = LANE_GDEC + h
        bcol = beta_all[r0:r0 + CHUNK, LANE_BETA + h:LANE_BETA + h + 1]
        gcol = gc_all[r0:r0 + CHUNK, lg:lg + 1]
        grow = gc_t[lg:lg + 1, r0:r0 + CHUNK]
        glast = gc_all[r0 + CHUNK - 1:r0 + CHUNK, lg:lg + 1]
        decay = _decay_matrix(gcol, grow, tril)
        kb = k * bcol
        egc = jnp.exp(gcol)
        kk = _dot_nt(jnp.concatenate([kb, q], axis=0), k)
        return dict(
            a=jnp.where(strict, kk[:CHUNK] * decay, 0.0),
            rhs=jnp.concatenate([v * bcol, kb * egc], axis=1),
            aqk=kk[CHUNK:] * decay,
            qg=q * egc,
            kd=k * jnp.exp(glast - gcol),
            dch=jnp.exp(glast))

    heads = range(GDN_HEADS)
    states = [s_ref[h] for h in heads]

    def recurrence_thunks(chunks, pre_s, sols):
        box = {}

        def round_ws(i):
            box["ws"] = [_dot(jnp.concatenate([sols[i][h][:, GDN_DV:], pre_s[i][h]["qg"]], axis=0), states[h])
                         for h in heads]

        def round_out(i):
            box["vn"] = [sols[i][h][:, :GDN_DV] - box["ws"][h][:CHUNK] for h in heads]
            box["o", i] = [box["ws"][h][CHUNK:] + _dot(pre_s[i][h]["aqk"], box["vn"][h]) for h in heads]

        def round_state(i):
            for h in heads:
                states[h] = states[h] * pre_s[i][h]["dch"] + _dot_tn(pre_s[i][h]["kd"], box["vn"][h])

        def store(i, h):
            r0 = chunks[i] * CHUNK
            gt = gate_ref[r0:r0 + CHUNK, h * GDN_DV:(h + 1) * GDN_DV].astype(F32)
            o_ref[r0:r0 + CHUNK, h * GDN_DV:(h + 1) * GDN_DV] = (
                _rmsnorm(box["o", i][h], nw) * _silu(gt)).astype(o_ref.dtype)

        ths = []
        for i in range(len(chunks)):
            ths += [functools.partial(f, i) for f in (round_ws, round_out, round_state)]
        return ths + [functools.partial(store, i, h) for i in range(len(chunks)) for h in heads]

    sets = [list(range(s, s + GDN_SET)) for s in range(0, STEP // CHUNK, GDN_SET)]
    doublings, merges = INV_BASE.bit_length() - 2, (CHUNK // INV_BASE).bit_length() - 1
    inv_steps = 1 + doublings + 2 * merges
    pre = {0: [[pre_item(c, h) for h in heads] for c in sets[0]]}
    pending = []
    for s, chunks in enumerate(sets):
        ahead = []
        if s + 1 < len(sets):
            pre[s + 1] = [[] for _ in sets[s + 1]]
            ahead = [functools.partial(lambda ss, i, c, h: pre[ss][i].append(pre_item(c, h)), s + 1, i, c, h)
                     for i, c in enumerate(sets[s + 1]) for h in heads]
        mixed = [th for pair in zip(pending, ahead) for th in pair]
        mixed += pending[len(ahead):] + ahead[len(pending):]
        tinv = []
        _interleave(_inv_unit_lower_levels([p["a"] for pc in pre[s] for p in pc], eye, tinv), mixed, inv_steps)
        sols = [[_dot(tinv[i * GDN_HEADS + h], pre[s][i][h]["rhs"]) for h in heads] for i in range(len(chunks))]
        pending = recurrence_thunks(chunks, pre[s], sols)
    for th in pending:
        th()
    for h in heads:
        s_ref[h] = states[h]

    xbuf[0:TAIL, :] = xbuf[STEP:STEP + TAIL, :]

    @pl.when(step == pl.num_programs(1) - 1)
    def _():
        sfin_ref[0] = s_ref[...]


def _gdn_prompt(proj, small, cw, par, nw, bsz, seq):
    nsteps = seq // STEP
    row = lambda b, s: b * nsteps + s
    return pl.pallas_call(
        _gdn_prompt_body,
        grid=(bsz, nsteps),
        in_specs=[pl.BlockSpec((STEP, GDN_CH), lambda b, s: (row(b, s), OFF_QKV // GDN_CH)),
                  pl.BlockSpec((STEP, GDN_V), lambda b, s: (row(b, s), OFF_GATE // GDN_V)),
                  pl.BlockSpec((STEP, SMALL_COLS), lambda b, s: (row(b, s), 0)),
                  pl.BlockSpec((CONV_W, GDN_CH), lambda b, s: (0, 0)),
                  pl.BlockSpec((SUBLANES, SMALL_COLS), lambda b, s: (0, 0)),
                  pl.BlockSpec((1, GDN_DV), lambda b, s: (0, 0))],
        out_specs=[pl.BlockSpec((STEP, GDN_V), lambda b, s: (row(b, s), 0)),
                   pl.BlockSpec((1, GDN_HEADS, GDN_DK, GDN_DV), lambda b, s: (b, 0, 0, 0))],
        out_shape=[jax.ShapeDtypeStruct((bsz * seq, GDN_V), BF16),
                   jax.ShapeDtypeStruct((bsz, GDN_HEADS, GDN_DK, GDN_DV), F32)],
        scratch_shapes=[pltpu.VMEM((TAIL + STEP, GDN_CH), F32),
                        pltpu.VMEM((GDN_HEADS, GDN_DK, GDN_DV), F32)],
        compiler_params=_params(2),
        name="gdn_prompt",
    )(proj, proj, small, cw, par, nw)


def _ssd_prompt_body(xs_ref, b_ref, c_ref, z_ref, sm_ref, cw_ref, cb_ref, par_ref, d_ref, nw_ref,
                     y_ref, sfin_ref, xb_x, xb_b, xb_c, st_ref):
    step = pl.program_id(1)

    @pl.when(step == 0)
    def _():
        xb_x[0:TAIL, :] = jnp.zeros((TAIL, SSM_DINNER), F32)
        xb_b[0:TAIL, :] = jnp.zeros((TAIL, SSM_BC), F32)
        xb_c[0:TAIL, :] = jnp.zeros((TAIL, SSM_BC), F32)
        st_ref[...] = jnp.zeros_like(st_ref)

    xb_x[TAIL:TAIL + STEP, :] = xs_ref[...].astype(F32)
    xb_b[TAIL:TAIL + STEP, :] = b_ref[...].astype(F32)
    xb_c[TAIL:TAIL + STEP, :] = c_ref[...].astype(F32)

    def conv(buf, r0, lo, width, woff):
        wl = woff + lo
        acc = cw_ref[CONV_W - 1:CONV_W, wl:wl + width] * buf[pl.ds(TAIL + r0, CHUNK), lo:lo + width]
        acc = acc + cb_ref[0:1, wl:wl + width]
        for j in range(CONV_W - 1):
            acc = acc + cw_ref[j:j + 1, wl:wl + width] * buf[pl.ds(TAIL - (CONV_W - 1) + j + r0, CHUNK), lo:lo + width]
        return _silu(acc)

    sm = sm_ref[...]
    dt_all = _softplus(sm + par_ref[0:1, :])
    cs_all = _chunk_cumsum(dt_all * (-jnp.exp(par_ref[1:2, :])))
    cs_t = cs_all.T
    tril, _, _ = _chunk_masks()

    assert SSM_P == CHUNK
    pw = 2 * SSM_P
    npair = SSM_R // 2
    items = [(c, g) for c in range(STEP // CHUNK) for g in range(SSM_GROUPS)]
    pairs = [(c, g, pr) for c, g in items for pr in range(npair)]
    lane_id = lax.broadcasted_iota(jnp.int32, (CHUNK, pw), 1)
    first = lane_id < SSM_P
    first_row = first[0:1]
    r2 = lax.broadcasted_iota(jnp.int32, (pw, pw), 0)
    c2 = lax.broadcasted_iota(jnp.int32, (pw, pw), 1)
    same_head = (r2 < SSM_P) == (c2 < SSM_P)
    tril_pair = lax.broadcasted_iota(jnp.int32, (CHUNK, pw), 0) >= (lane_id % SSM_P)

    def pair_cols(src, c, g, pr):
        l0 = LANE_DT + SSM_R * g + 2 * pr
        r0 = c * CHUNK
        return jnp.where(first, src[r0:r0 + CHUNK, l0:l0 + 1], src[r0:r0 + CHUNK, l0 + 1:l0 + 2])

    def pair_row(row0, row1):
        return jnp.where(first_row, row0, row1)

    dts, cscs, csrs, csls, lms = [], [], [], [], []

    def gating(c, g, pr):
        l0 = LANE_DT + SSM_R * g + 2 * pr
        r0 = c * CHUNK
        seg = (r0 // pw) * pw
        t0 = cs_t[l0:l0 + 1, seg:seg + pw]
        t1 = cs_t[l0 + 1:l0 + 2, seg:seg + pw]
        if r0 == seg:
            csr = pair_row(t0, pltpu.roll(t1, CHUNK, 1))
        else:
            csr = pair_row(pltpu.roll(t0, CHUNK, 1), t1)
        last = r0 + CHUNK - 1
        csc = pair_cols(cs_all, c, g, pr)
        dts.append(pair_cols(dt_all, c, g, pr))
        cscs.append(csc)
        csrs.append(csr)
        csls.append(pair_row(cs_all[last:last + 1, l0:l0 + 1], cs_all[last:last + 1, l0 + 1:l0 + 2]))
        lms.append(_decay_matrix(csc, csr, tril_pair))

    for p in pairs:
        gating(*p)
    bcs = [conv(xb_b, c * CHUNK, g * SSM_N, SSM_N, SSM_DINNER) for c, g in items]
    ccs = [conv(xb_c, c * CHUNK, g * SSM_N, SSM_N, SSM_DINNER + SSM_BC) for c, g in items]
    xss = [conv(xb_x, c * CHUNK, g * SSM_GW + pr * pw, pw, 0) for c, g, pr in pairs]
    cbs = [_dot_nt(cc, jnp.concatenate([bc, bc], axis=0)) for cc, bc in zip(ccs, bcs)]
    xdts = [xs * dt for xs, dt in zip(xss, dts)]
    ms = [cbs[i // npair] * lm for i, lm in enumerate(lms)]
    bds = [jnp.where(same_head, jnp.concatenate([x, x], axis=0), 0.0) for x in xdts]
    ylocs = []
    for (c, g, pr), m, bd, xs in zip(pairs, ms, bds, xss):
        lo = g * SSM_GW + pr * pw
        ylocs.append(_dot(m, bd) + d_ref[0:1, lo:lo + pw] * xs)
    ecss = [jnp.exp(csc) for csc in cscs]
    xds = [xdt * jnp.exp(csl - csc) for xdt, csl, csc in zip(xdts, csls, cscs)]
    dchs = [jnp.exp(csl) for csl in csls]

    def group_cat(vals, i):
        return jnp.concatenate(vals[i * npair:(i + 1) * npair], axis=1)

    outs = []
    for i, (c, g) in enumerate(items):
        st_g = st_ref[g]
        outs.append(group_cat(ylocs, i) + _dot(ccs[i], st_g) * group_cat(ecss, i))
        st_ref[g] = st_g * group_cat(dchs, i) + _dot_tn(bcs[i], group_cat(xds, i))
    for (c, g), y_g in zip(items, outs):
        r0 = c * CHUNK
        y_g = y_g * _silu(z_ref[r0:r0 + CHUNK, g * SSM_GW:(g + 1) * SSM_GW].astype(F32))
        y_ref[r0:r0 + CHUNK, g * SSM_GW:(g + 1) * SSM_GW] = _rmsnorm(
            y_g, nw_ref[0:1, g * SSM_GW:(g + 1) * SSM_GW]).astype(y_ref.dtype)

    xb_x[0:TAIL, :] = xb_x[STEP:STEP + TAIL, :]
    xb_b[0:TAIL, :] = xb_b[STEP:STEP + TAIL, :]
    xb_c[0:TAIL, :] = xb_c[STEP:STEP + TAIL, :]

    @pl.when(step == pl.num_programs(1) - 1)
    def _():
        for g in range(SSM_GROUPS):
            st_t = st_ref[g].T
            for rr in range(SSM_R):
                sfin_ref[0, SSM_R * g + rr] = st_t[rr * SSM_P:(rr + 1) * SSM_P, :]


def _ssd_prompt(proj, small, cw, cb, par, d_row, nw, bsz, seq):
    nsteps = seq // STEP
    row = lambda b, s: b * nsteps + s
    const = lambda b, s: (0, 0)
    return pl.pallas_call(
        _ssd_prompt_body,
        grid=(bsz, nsteps),
        in_specs=[pl.BlockSpec((STEP, SSM_DINNER), lambda b, s: (row(b, s), OFF_XS // SSM_DINNER)),
                  pl.BlockSpec((STEP, SSM_BC), lambda b, s: (row(b, s), OFF_B // SSM_BC)),
                  pl.BlockSpec((STEP, SSM_BC), lambda b, s: (row(b, s), OFF_C // SSM_BC)),
                  pl.BlockSpec((STEP, SSM_DINNER), lambda b, s: (row(b, s), OFF_Z // SSM_DINNER)),
                  pl.BlockSpec((STEP, SMALL_COLS), lambda b, s: (row(b, s), 0)),
                  pl.BlockSpec((CONV_W, SSM_CH), const),
                  pl.BlockSpec((1, SSM_CH), const),
                  pl.BlockSpec((SUBLANES, SMALL_COLS), const),
                  pl.BlockSpec((1, SSM_DINNER), const),
                  pl.BlockSpec((1, SSM_DINNER), const)],
        out_specs=[pl.BlockSpec((STEP, SSM_DINNER), lambda b, s: (row(b, s), 0)),
                   pl.BlockSpec((1, SSM_HEADS, SSM_P, SSM_N), lambda b, s: (b, 0, 0, 0))],
        out_shape=[jax.ShapeDtypeStruct((bsz * seq, SSM_DINNER), BF16),
                   jax.ShapeDtypeStruct((bsz, SSM_HEADS, SSM_P, SSM_N), F32)],
        scratch_shapes=[pltpu.VMEM((TAIL + STEP, SSM_DINNER), F32),
                        pltpu.VMEM((TAIL + STEP, SSM_BC), F32),
                        pltpu.VMEM((TAIL + STEP, SSM_BC), F32),
                        pltpu.VMEM((SSM_GROUPS, SSM_N, SSM_GW), F32)],
        compiler_params=_params(2),
        name="ssd_prompt",
    )(proj, proj, proj, proj, small, cw, cb, par, d_row, nw)


def _gdn_sample_body(qkv_ref, gate_ref, sm_ref, cst_ref, s_ref, cw_ref, par_ref, nw_ref, o_ref, sout_ref):
    bt = qkv_ref.shape[0]
    sm = sm_ref[...]
    beta_all = _sigmoid(sm)
    eg_all = jnp.exp(-jnp.exp(par_ref[0:1, :]) * _softplus(sm + par_ref[1:2, :]))
    rowid = lax.broadcasted_iota(jnp.int32, (bt, GDN_DK), 0)
    nw = nw_ref[...]

    def conv(blk):
        lo = blk * LANES
        acc = cw_ref[CONV_W - 1:CONV_W, lo:lo + LANES] * qkv_ref[:, lo:lo + LANES]
        for j in range(CONV_W - 1):
            acc = acc + cw_ref[j:j + 1, lo:lo + LANES] * cst_ref[:, j * GDN_CH + lo:j * GDN_CH + lo + LANES]
        return _silu(acc)

    for h in range(GDN_HEADS):
        q = conv(h)
        k = conv(GDN_HEADS + h)
        v = conv(2 * GDN_HEADS + h)
        q = q * lax.rsqrt(jnp.sum(q * q, axis=-1, keepdims=True) + EPS) * (GDN_DK ** -0.5)
        k = k * lax.rsqrt(jnp.sum(k * k, axis=-1, keepdims=True) + EPS)
        bcol = beta_all[:, LANE_BETA + h:LANE_BETA + h + 1]
        egcol = eg_all[:, LANE_GDEC + h:LANE_GDEC + h + 1]
        qs_rows, ks_rows = [], []
        for t in range(bt):
            lhs = jnp.where(rowid == 0, q[t:t + 1, :], jnp.where(rowid == 1, k[t:t + 1, :], 0.0))
            r = _dot(lhs, s_ref[t, h])
            qs_rows.append(r[0:1])
            ks_rows.append(r[1:2])
        q_s = jnp.concatenate(qs_rows, axis=0)
        k_s = jnp.concatenate(ks_rows, axis=0)
        v_new = bcol * v - (bcol * egcol) * k_s
        o = egcol * q_s + jnp.sum(q * k, axis=-1, keepdims=True) * v_new
        for t in range(bt):
            k_t = jnp.where(rowid == t, k, 0.0)
            sout_ref[t, h] = s_ref[t, h] * egcol[t:t + 1, :] + _dot_tn(k_t, v_new)
        gt = gate_ref[:, h * GDN_DV:(h + 1) * GDN_DV]
        o_ref[:, h * GDN_DV:(h + 1) * GDN_DV] = (_rmsnorm(o, nw) * _silu(gt)).astype(o_ref.dtype)


def _gdn_sample(proj, small, cstate, state, layer, cw, par, nw):
    t = proj.shape[0]
    bt = GDN_SAMPLE_BT
    blk0 = layer * (t // bt)
    const = lambda i: (0, 0)
    return pl.pallas_call(
        _gdn_sample_body,
        grid=(t // bt,),
        in_specs=[pl.BlockSpec((bt, GDN_CH), lambda i: (i, OFF_QKV // GDN_CH)),
                  pl.BlockSpec((bt, GDN_V), lambda i: (i, OFF_GATE // GDN_V)),
                  pl.BlockSpec((bt, SMALL_COLS), lambda i: (i, 0)),
                  pl.BlockSpec((bt, (CONV_W - 1) * GDN_CH), lambda i: (blk0 + i, 0)),
                  pl.BlockSpec((bt, GDN_HEADS, GDN_DK, GDN_DV), lambda i: (blk0 + i, 0, 0, 0)),
                  pl.BlockSpec((CONV_W, GDN_CH), const),
                  pl.BlockSpec((SUBLANES, SMALL_COLS), const),
                  pl.BlockSpec((1, GDN_DV), const)],
        out_specs=[pl.BlockSpec((bt, GDN_V), lambda i: (i, 0)),
                   pl.BlockSpec((bt, GDN_HEADS, GDN_DK, GDN_DV), lambda i: (i, 0, 0, 0))],
        out_shape=[jax.ShapeDtypeStruct((t, GDN_V), BF16),
                   jax.ShapeDtypeStruct((t,) + state.shape[1:], F32)],
        compiler_params=_params(1),
        name="gdn_sample",
    )(proj, proj, small, cstate, state, cw, par, nw)


def _ssd_sample_body(xs_ref, b_ref, c_ref, z_ref, sm_ref, cst_ref, s_ref, cw_ref, cb_ref, par_ref, d_ref, nw_ref,
                     y_ref, sout_ref):
    bt = xs_ref.shape[0]
    sm = sm_ref[...]
    dt_all = _softplus(sm + par_ref[0:1, :])
    e_all = jnp.exp(dt_all * (-jnp.exp(par_ref[1:2, :])))
    rowid_n = lax.broadcasted_iota(jnp.int32, (bt, SSM_N), 0)
    rowid_g = lax.broadcasted_iota(jnp.int32, (bt, SSM_GW), 0)

    def conv(x_ref, lo, width, woff):
        wl = woff + lo
        acc = cw_ref[CONV_W - 1:CONV_W, wl:wl + width] * x_ref[:, lo:lo + width] + cb_ref[0:1, wl:wl + width]
        for j in range(CONV_W - 1):
            acc = acc + cw_ref[j:j + 1, wl:wl + width] * cst_ref[:, j * SSM_CH + wl:j * SSM_CH + wl + width]
        return _silu(acc)

    for g in range(SSM_GROUPS):
        bc = conv(b_ref, g * SSM_N, SSM_N, SSM_DINNER)
        cc = conv(c_ref, g * SSM_N, SSM_N, SSM_DINNER + SSM_BC)
        xs_g = conv(xs_ref, g * SSM_GW, SSM_GW, 0)
        cb = jnp.sum(cc * bc, axis=-1, keepdims=True)
        yoff_rows = []
        for t in range(bt):
            c_t = jnp.where(rowid_n == 0, cc[t:t + 1, :], 0.0)
            s_tg = jnp.concatenate([s_ref[t, SSM_R * g + rr] for rr in range(SSM_R)], axis=0)
            yoff_rows.append(_dot_nt(c_t, s_tg)[0:1])
        yoff = jnp.concatenate(yoff_rows, axis=0)
        lanes = [LANE_DT + SSM_R * g + rr for rr in range(SSM_R)]
        dt_g = jnp.concatenate([jnp.broadcast_to(dt_all[:, l:l + 1], (bt, SSM_P)) for l in lanes], axis=1)
        e_g = jnp.concatenate([jnp.broadcast_to(e_all[:, l:l + 1], (bt, SSM_P)) for l in lanes], axis=1)
        xdt = xs_g * dt_g
        y = cb * xdt + yoff * e_g + d_ref[0:1, g * SSM_GW:(g + 1) * SSM_GW] * xs_g
        for t in range(bt):
            outer = _dot_tn(jnp.where(rowid_g == t, xdt, 0.0), bc)
            for rr in range(SSM_R):
                h = SSM_R * g + rr
                sout_ref[t, h] = (s_ref[t, h] * e_all[t:t + 1, lanes[rr]:lanes[rr] + 1]
                                  + outer[rr * SSM_P:(rr + 1) * SSM_P, :])
        y = y * _silu(z_ref[:, g * SSM_GW:(g + 1) * SSM_GW])
        y_ref[:, g * SSM_GW:(g + 1) * SSM_GW] = _rmsnorm(
            y, nw_ref[0:1, g * SSM_GW:(g + 1) * SSM_GW]).astype(y_ref.dtype)


def _ssd_sample(proj, small, cstate, state, layer, cw, cb, par, d_row, nw):
    t = proj.shape[0]
    bt = SAMPLE_BT
    blk0 = layer * (t // bt)
    const = lambda i: (0, 0)
    return pl.pallas_call(
        _ssd_sample_body,
        grid=(t // bt,),
        in_specs=[pl.BlockSpec((bt, SSM_DINNER), lambda i: (i, OFF_XS // SSM_DINNER)),
                  pl.BlockSpec((bt, SSM_BC), lambda i: (i, OFF_B // SSM_BC)),
                  pl.BlockSpec((bt, SSM_BC), lambda i: (i, OFF_C // SSM_BC)),
                  pl.BlockSpec((bt, SSM_DINNER), lambda i: (i, OFF_Z // SSM_DINNER)),
                  pl.BlockSpec((bt, SMALL_COLS), lambda i: (i, 0)),
                  pl.BlockSpec((bt, (CONV_W - 1) * SSM_CH), lambda i: (blk0 + i, 0)),
                  pl.BlockSpec((bt, SSM_HEADS, SSM_P, SSM_N), lambda i: (blk0 + i, 0, 0, 0)),
                  pl.BlockSpec((CONV_W, SSM_CH), const),
                  pl.BlockSpec((1, SSM_CH), const),
                  pl.BlockSpec((SUBLANES, SMALL_COLS), const),
                  pl.BlockSpec((1, SSM_DINNER), const),
                  pl.BlockSpec((1, SSM_DINNER), const)],
        out_specs=[pl.BlockSpec((bt, SSM_DINNER), lambda i: (i, 0)),
                   pl.BlockSpec((bt, SSM_HEADS, SSM_P, SSM_N), lambda i: (i, 0, 0, 0))],
        out_shape=[jax.ShapeDtypeStruct((t, SSM_DINNER), BF16),
                   jax.ShapeDtypeStruct((t,) + state.shape[1:], F32)],
        compiler_params=_params(1),
        name="ssd_sample",
    )(proj, proj, proj, proj, small, cstate, state, cw, cb, par, d_row, nw)


def _merge_body(oa_ref, yb_ref, ga_ref, gb_ref, x_ref, wg_ref, ws_ref, wo_ref, o_ref):
    a = jnp.dot(oa_ref[...], wg_ref[...], preferred_element_type=F32)
    b = jnp.dot(yb_ref[...], ws_ref[...], preferred_element_type=F32)
    merged = _sigmoid(ga_ref[...].astype(F32)) * a + _sigmoid(gb_ref[...].astype(F32)) * b
    o_ref[...] = x_ref[...] + _dot(merged, wo_ref[...])


def _merge(o_a, y_b, proj, x, wg, ws, wo, *, tm):
    t = x.shape[0]
    tm = min(tm, t)
    assert t % tm == 0
    const = lambda i: (0, 0)
    return pl.pallas_call(
        _merge_body,
        grid=(t // tm,),
        in_specs=[pl.BlockSpec((tm, GDN_V), lambda i: (i, 0)),
                  pl.BlockSpec((tm, SSM_DINNER), lambda i: (i, 0)),
                  pl.BlockSpec((tm, D_MODEL), lambda i: (i, OFF_GA // D_MODEL)),
                  pl.BlockSpec((tm, D_MODEL), lambda i: (i, OFF_GB // D_MODEL)),
                  pl.BlockSpec((tm, D_MODEL), lambda i: (i, 0)),
                  pl.BlockSpec((GDN_V, D_MODEL), const),
                  pl.BlockSpec((SSM_DINNER, D_MODEL), const),
                  pl.BlockSpec((D_MODEL, D_MODEL), const)],
        out_specs=pl.BlockSpec((tm, D_MODEL), lambda i: (i, 0)),
        out_shape=jax.ShapeDtypeStruct((t, D_MODEL), F32),
        compiler_params=_params(1),
        name="merge_out",
    )(o_a, y_b, proj, proj, x, wg, ws, wo)


def _softmax_rows(s):
    e = jnp.exp(s - jnp.max(s, axis=-1, keepdims=True))
    return e / jnp.sum(e, axis=-1, keepdims=True)


def _xattn_prompt_body(x_ref, gx_ref, wq_ref, wo_ref, k_ref, v_ref, o_ref):
    x = x_ref[...]
    q = _dot(_rmsnorm(x, gx_ref[...]), wq_ref[...])
    outs = []
    for h in range(X_HEADS):
        sl = slice(h * X_HD, (h + 1) * X_HD)
        p = _softmax_rows(_dot_nt(q[:, sl], k_ref[:, sl]) * (X_HD ** -0.5))
        outs.append(_dot(p, v_ref[:, sl]))
    o_ref[...] = x + _dot(jnp.concatenate(outs, axis=1), wo_ref[...])


def _xattn_prompt(x, gx, wq, wo, mk, mv, seq, *, tm):
    t = x.shape[0]
    n_mem = mk.shape[0] // (t // seq)
    tm = min(tm, seq)
    assert seq % tm == 0
    per_seq = seq // tm
    const = lambda i: (0, 0)
    return pl.pallas_call(
        _xattn_prompt_body,
        grid=(t // tm,),
        in_specs=[pl.BlockSpec((tm, D_MODEL), lambda i: (i, 0)),
                  pl.BlockSpec((1, D_MODEL), const),
                  pl.BlockSpec((D_MODEL, D_MODEL), const),
                  pl.BlockSpec((D_MODEL, D_MODEL), const),
                  pl.BlockSpec((n_mem, D_MODEL), lambda i: (i // per_seq, 0)),
                  pl.BlockSpec((n_mem, D_MODEL), lambda i: (i // per_seq, 0))],
        out_specs=pl.BlockSpec((tm, D_MODEL), lambda i: (i, 0)),
        out_shape=jax.ShapeDtypeStruct((t, D_MODEL), F32),
        compiler_params=_params(1),
        name="xattn_prompt",
    )(x, gx.reshape(1, D_MODEL), wq, wo, mk, mv)


def _xattn_sample_body(x_ref, gx_ref, wq_ref, wo_ref, k_ref, v_ref, o_ref, q_scr, a_scr):
    i = pl.program_id(0)
    bt = k_ref.shape[0]

    @pl.when(i == 0)
    def _():
        q_scr[...] = _dot(_rmsnorm(x_ref[...], gx_ref[...]), wq_ref[...])

    assert X_ROWS == SUBLANES and X_HALVES == 2
    nrow = k_ref.shape[1]
    sub = lax.broadcasted_iota(jnp.int32, (X_ROWS, nrow), 0)
    lane = lax.broadcasted_iota(jnp.int32, (X_ROWS, nrow), 1)
    own = (lane % X_ROWS) == sub
    lower = sub < X_HEADS
    toks = range(bt)
    q_rows = []
    for tt in toks:
        q_t = q_scr[pl.ds(i * bt + tt, 1), :]
        q_rows.append(jnp.concatenate(
            [q_t[:, h * X_HD + half * LANES:h * X_HD + (half + 1) * LANES]
             for half in range(X_HALVES) for h in range(X_HEADS)], axis=0))
    es = [jnp.where(own, _dot_nt(q_rows[tt], k_ref[tt]), 0.0) for tt in toks]
    p2s = []
    for e in es:
        s = (e + pltpu.roll(pltpu.roll(e, X_HEADS, 0), nrow - X_HEADS, 1)) * (X_HD ** -0.5)
        s = jnp.where(own, s, -jnp.inf)
        p = jnp.exp(s - jnp.max(s, axis=-1, keepdims=True))
        p = jnp.where(lower, p / jnp.sum(p, axis=-1, keepdims=True), 0.0)
        p2s.append(p + pltpu.roll(pltpu.roll(p, X_HEADS, 0), X_HEADS, 1))
    o2s = [_dot(p2s[tt], v_ref[tt]) for tt in toks]
    for tt in toks:
        a_scr[pl.ds(i * bt + tt, 1), :] = jnp.concatenate(
            [o2s[tt][half * X_HEADS + h:half * X_HEADS + h + 1, :]
             for h in range(X_HEADS) for half in range(X_HALVES)], axis=1)

    @pl.when(i == pl.num_programs(0) - 1)
    def _():
        o_ref[...] = x_ref[...] + _dot(a_scr[...], wo_ref[...])


def _xattn_sample(x, gx, wq, wo, ck, cv, layer):
    t = x.shape[0]
    n_mem = ck.shape[1]
    bt = XATTN_BT
    blk0 = layer * (t // bt)
    const = lambda i: (0, 0)
    return pl.pallas_call(
        _xattn_sample_body,
        grid=(t // bt,),
        in_specs=[pl.BlockSpec((t, D_MODEL), const),
                  pl.BlockSpec((1, D_MODEL), const),
                  pl.BlockSpec((D_MODEL, D_MODEL), const),
                  pl.BlockSpec((D_MODEL, D_MODEL), const),
                  pl.BlockSpec((bt,) + ck.shape[1:], lambda i: (blk0 + i, 0, 0)),
                  pl.BlockSpec((bt,) + cv.shape[1:], lambda i: (blk0 + i, 0, 0))],
        out_specs=pl.BlockSpec((t, D_MODEL), const),
        out_shape=jax.ShapeDtypeStruct((t, D_MODEL), F32),
        scratch_shapes=[pltpu.VMEM((t, D_MODEL), F32), pltpu.VMEM((t, D_MODEL), F32)],
        compiler_params=_params(1),
        name="xattn_sample",
    )(x, gx.reshape(1, D_MODEL), wq, wo, ck, cv)


def _mlp_body(x_ref, g_ref, wu_ref, wd_ref, gf_ref, y_ref, hn_ref, acc_ref, *, final_norm):
    j = pl.program_id(1)

    @pl.when(j == 0)
    def _():
        hn_ref[...] = _rmsnorm(x_ref[...], g_ref[...]).astype(BF16)
        acc_ref[...] = jnp.zeros_like(acc_ref)

    hf = jnp.dot(hn_ref[...], wu_ref[...], preferred_element_type=F32)
    acc_ref[...] += _dot(jnp.square(jnp.maximum(hf, 0.0)), wd_ref[...])

    @pl.when(j == pl.num_programs(1) - 1)
    def _():
        x_new = x_ref[...] + acc_ref[...]
        y_ref[...] = _rmsnorm(x_new, gf_ref[...]) if final_norm else x_new


def _mlp(x, g, wu, wd, gf, *, tm, tf, final_norm):
    t = x.shape[0]
    tm = min(tm, t)
    assert t % tm == 0 and D_FF % tf == 0
    const = lambda i, j: (0, 0)
    return pl.pallas_call(
        functools.partial(_mlp_body, final_norm=final_norm),
        grid=(t // tm, D_FF // tf),
        in_specs=[pl.BlockSpec((tm, D_MODEL), lambda i, j: (i, 0)),
                  pl.BlockSpec((1, D_MODEL), const),
                  pl.BlockSpec((D_MODEL, tf), lambda i, j: (0, j)),
                  pl.BlockSpec((tf, D_MODEL), lambda i, j: (j, 0)),
                  pl.BlockSpec((1, D_MODEL), const)],
        out_specs=pl.BlockSpec((tm, D_MODEL), lambda i, j: (i, 0)),
        out_shape=jax.ShapeDtypeStruct((t, D_MODEL), F32),
        scratch_shapes=[pltpu.VMEM((tm, D_MODEL), BF16), pltpu.VMEM((tm, D_MODEL), F32)],
        compiler_params=_params(2),
        name="mlp",
    )(x, g.reshape(1, D_MODEL), wu, wd, gf.reshape(1, D_MODEL))


def _lane_row(vec, lane0):
    return jnp.zeros((SMALL_COLS,), F32).at[lane0:lane0 + vec.shape[0]].set(vec.astype(F32))


def _param_rows(*rows):
    out = jnp.zeros((SUBLANES, SMALL_COLS), F32)
    for i, r in enumerate(rows):
        out = out.at[i].set(r)
    return out


def kernel(x_prompt, x_sample, mem_prompt, state_gdn_conv, state_gdn, state_ssm_conv, state_ssm, cache_mem_k, cache_mem_v, g_mix, w_in, gdn_conv_w, gdn_A_log, gdn_dt_bias, gdn_norm_w, w_gdn_up, ssm_conv_w, ssm_conv_b, ssm_dt_bias, ssm_A_log, ssm_D, ssm_norm_w, w_ssm_up, w_out, g_mem, w_mk, w_mv, g_x, w_cq, w_co, g_ff, w_ff_up, w_ff_down, g_final):
    bp, seq, _ = x_prompt.shape
    bs, dec_seq, _ = x_sample.shape
    depth = w_in.shape[0]
    n_mem = mem_prompt.shape[1]
    assert seq % STEP == 0 and dec_seq == 1
    assert bs % SAMPLE_BT == 0 and bs % GDN_SAMPLE_BT == 0 and bs % XATTN_BT == 0
    tp = bp * seq

    xp = x_prompt.reshape(tp, D_MODEL)
    xs = x_sample.reshape(bs, D_MODEL)
    mem = mem_prompt.reshape(bp * n_mem, D_MODEL)
    gst_all = state_gdn_conv.astype(F32).reshape(depth * bs, (CONV_W - 1) * GDN_CH)
    sst_all = state_ssm_conv.astype(F32).reshape(depth * bs, (CONV_W - 1) * SSM_CH)
    sg_all = state_gdn.astype(F32).reshape(depth * bs, GDN_HEADS, GDN_DK, GDN_DV)
    ss_all = state_ssm.astype(F32).reshape(depth * bs, SSM_HEADS, SSM_P, SSM_N)
    def cache_rows(c):
        c = c.astype(F32).reshape(depth * bs, n_mem, X_HEADS, X_HALVES, LANES)
        return c.transpose(0, 1, 3, 2, 4).reshape(depth * bs, n_mem * X_ROWS, LANES)

    ck_all = cache_rows(cache_mem_k)
    cv_all = cache_rows(cache_mem_v)
    outs ={k: [] for k in ("p_gc", "p_g", "p_sc", "p_s", "p_mk", "p_mv", "s_gc", "s_g", "s_sc", "s_s")}

    starts = [0]
    for width in IN_SPLITS:
        starts.append(starts[-1] + width)
    col = lambda i: slice(starts[i], starts[i + 1])

    for l in range(depth):
        wl = w_in[l]
        w_main = jnp.concatenate([wl[:, starts[0]:starts[2]], wl[:, starts[4]:starts[6]],
                                  wl[:, starts[7]:starts[9]]], axis=1).astype(BF16)
        w_small = jnp.concatenate(
            [wl[:, col(2)], wl[:, col(3)], wl[:, col(6)],
             jnp.zeros((D_MODEL, SMALL_COLS - 2 * GDN_HEADS - SSM_HEADS), F32)], axis=1).astype(BF16)
        gdn_par = _param_rows(_lane_row(gdn_A_log[l], LANE_GDEC), _lane_row(gdn_dt_bias[l], LANE_GDEC))
        ssm_par = _param_rows(_lane_row(ssm_dt_bias[l], LANE_DT), _lane_row(ssm_A_log[l], LANE_DT))
        gdn_nw = gdn_norm_w[l].reshape(1, GDN_DV).astype(F32)
        ssm_nw = ssm_norm_w[l].reshape(1, SSM_DINNER).astype(F32)
        d_row = jnp.repeat(ssm_D[l].astype(F32), SSM_P).reshape(1, SSM_DINNER)
        gcw = gdn_conv_w[l].astype(F32)
        scw = ssm_conv_w[l].astype(F32)
        scb = ssm_conv_b[l].reshape(1, SSM_CH).astype(F32)
        wg, ws, wo = w_gdn_up[l].astype(BF16), w_ssm_up[l].astype(BF16), w_out[l].astype(BF16)
        wq, wco = w_cq[l].astype(BF16), w_co[l].astype(BF16)
        wu, wd = w_ff_up[l].astype(BF16), w_ff_down[l].astype(BF16)
        last = l == depth - 1

        pt, st = PROMPT_TILES, SAMPLE_TILES
        mk = _rms_mm(mem, g_mem[l], w_mk[l].astype(BF16), tm=pt["kv"][0], tn=pt["kv"][1])
        mv = _rms_mm(mem, g_mem[l], w_mv[l].astype(BF16), tm=pt["kv"][0], tn=pt["kv"][1])
        outs["p_mk"].append(mk.reshape(bp, n_mem, X_HEADS, X_HD))
        outs["p_mv"].append(mv.reshape(bp, n_mem, X_HEADS, X_HD))

        proj, small = _in_proj(xp, g_mix[l], w_main, w_small, tm=pt["in_proj"][0], tn=pt["in_proj"][1],
                               out_dtype=BF16)
        o_a, p_g = _gdn_prompt(proj, small, gcw, gdn_par, gdn_nw, bp, seq)
        y_b, p_s = _ssd_prompt(proj, small, scw, scb, ssm_par, d_row, ssm_nw, bp, seq)
        proj3 = proj.reshape(bp, seq, MAIN_COLS)
        outs["p_gc"].append(proj3[:, seq - (CONV_W - 1):, OFF_QKV:OFF_QKV + GDN_CH].astype(F32))
        outs["p_sc"].append(proj3[:, seq - (CONV_W - 1):, OFF_XS:OFF_XS + SSM_CH].astype(F32))
        outs["p_g"].append(p_g)
        outs["p_s"].append(p_s)
        xp = _merge(o_a, y_b, proj, xp, wg, ws, wo, tm=pt["merge"])
        xp = _xattn_prompt(xp, g_x[l], wq, wco, mk, mv, seq, tm=pt["xattn"])
        xp = _mlp(xp, g_ff[l], wu, wd, g_final, tm=pt["mlp"][0], tf=pt["mlp"][1], final_norm=last)

        proj, small = _in_proj(xs, g_mix[l], w_main, w_small, tm=st["in_proj"][0], tn=st["in_proj"][1],
                               out_dtype=F32)
        o_a, s_g = _gdn_sample(proj, small, gst_all, sg_all, l, gcw, gdn_par, gdn_nw)
        y_b, s_s = _ssd_sample(proj, small, sst_all, ss_all, l, scw, scb, ssm_par, d_row, ssm_nw)
        outs["s_gc"].append(jnp.concatenate(
            [state_gdn_conv[l, :, 1:].astype(F32), proj[:, None, OFF_QKV:OFF_QKV + GDN_CH]], axis=1))
        outs["s_sc"].append(jnp.concatenate(
            [state_ssm_conv[l, :, 1:].astype(F32), proj[:, None, OFF_XS:OFF_XS + SSM_CH]], axis=1))
        outs["s_g"].append(s_g)
        outs["s_s"].append(s_s)
        xs = _merge(o_a, y_b, proj, xs, wg, ws, wo, tm=st["merge"])
        xs = _xattn_sample(xs, g_x[l], wq, wco, ck_all, cv_all, l)
        xs = _mlp(xs, g_ff[l], wu, wd, g_final, tm=st["mlp"][0], tf=st["mlp"][1], final_norm=last)

    stack = lambda k: jnp.stack(outs[k])
    return (xp.reshape(bp, seq, D_MODEL), xs.reshape(bs, dec_seq, D_MODEL),
            stack("p_gc"), stack("p_g"), stack("p_sc"), stack("p_s"), stack("p_mk"), stack("p_mv"),
            stack("s_gc"), stack("s_g"), stack("s_sc"), stack("s_s"))
```

```python
import functools

import jax
import jax.numpy as jnp
from jax import lax
from jax.experimental import pallas as pl
from jax.experimental.pallas import tpu as pltpu

F32 = jnp.float32
BF16 = jnp.bfloat16

EPS = 1e-6
CHUNK = 64
CONV_W = 4
D_MODEL = 1024
GDN_HEADS = 8
GDN_DK = 128
GDN_DV = 128
GDN_QK = GDN_HEADS * GDN_DK
GDN_V = GDN_HEADS * GDN_DV
GDN_CH = 2 * GDN_QK + GDN_V
SSM_DINNER = 2 * D_MODEL
SSM_P = 64
SSM_HEADS = SSM_DINNER // SSM_P
SSM_GROUPS = 8
SSM_R = SSM_HEADS // SSM_GROUPS
SSM_N = 128
SSM_BC = SSM_GROUPS * SSM_N
SSM_CH = SSM_DINNER + 2 * SSM_BC
SSM_GW = SSM_R * SSM_P
X_HEADS = 4
X_HD = D_MODEL // X_HEADS
D_FF = 4 * D_MODEL
IN_SPLITS = (GDN_CH, GDN_V, GDN_HEADS, GDN_HEADS, SSM_DINNER, SSM_CH, SSM_HEADS, D_MODEL, D_MODEL)

MAIN_COLS = GDN_CH + GDN_V + SSM_DINNER + SSM_CH + 2 * D_MODEL
OFF_QKV, OFF_GATE, OFF_Z = 0, GDN_CH, GDN_CH + GDN_V
OFF_XS = OFF_Z + SSM_DINNER
OFF_B = OFF_XS + SSM_DINNER
OFF_C = OFF_B + SSM_BC
OFF_GA = OFF_C + SSM_BC
OFF_GB = OFF_GA + D_MODEL
LANES = 128
SUBLANES = 8
X_HALVES = X_HD // LANES
X_ROWS = X_HEADS * X_HALVES
SMALL_COLS = LANES
LANE_BETA, LANE_GDEC, LANE_DT = 0, GDN_HEADS, 2 * GDN_HEADS

STEP = 4 * CHUNK
GDN_SET = 2
TAIL = SUBLANES
SAMPLE_BT = 8
GDN_SAMPLE_BT = 16
XATTN_BT = 8
VMEM_LIMIT = 54 * 1024 * 1024
PROMPT_TILES = dict(kv=(1024, 1024), in_proj=(2048, 1536), merge=512, xattn=1024, mlp=(1024, 1024))
SAMPLE_TILES = dict(in_proj=(128, 2048), merge=128, mlp=(128, 2048))


def _params(n_axes):
    return pltpu.CompilerParams(dimension_semantics=("arbitrary",) * n_axes, vmem_limit_bytes=VMEM_LIMIT)


def _sigmoid(x):
    return 0.5 * jnp.tanh(0.5 * x) + 0.5


def _silu(x):
    h = 0.5 * x
    return h + h * jnp.tanh(h)


def _softplus(x):
    return jnp.maximum(x, 0.0) + jnp.log1p(jnp.exp(-jnp.abs(x)))


def _rmsnorm(x, g):
    xf = x.astype(F32)
    return xf * lax.rsqrt(jnp.mean(xf * xf, axis=-1, keepdims=True) + EPS) * g


def _dot(a, b):
    return jnp.dot(a.astype(BF16), b.astype(BF16), preferred_element_type=F32)


def _dot_nt(a, b):
    return lax.dot_general(a.astype(BF16), b.astype(BF16), (((1,), (1,)), ((), ())), preferred_element_type=F32)


def _dot_tn(a, b):
    return lax.dot_general(a.astype(BF16), b.astype(BF16), (((0,), (0,)), ((), ())), preferred_element_type=F32)


def _chunk_cumsum(x):
    n = x.shape[0]
    r = lax.broadcasted_iota(jnp.int32, (n, n), 0)
    c = lax.broadcasted_iota(jnp.int32, (n, n), 1)
    tri = jnp.where((r >= c) & ((r // CHUNK) == (c // CHUNK)), 1.0, 0.0).astype(BF16)
    h1 = x.astype(BF16)
    r1 = x - h1.astype(F32)
    h2 = r1.astype(BF16)
    h3 = (r1 - h2.astype(F32)).astype(BF16)
    d = functools.partial(jnp.dot, preferred_element_type=F32)
    return d(tri, h1) + (d(tri, h2) + d(tri, h3))


def _chunk_masks():
    r = lax.broadcasted_iota(jnp.int32, (CHUNK, CHUNK), 0)
    c = lax.broadcasted_iota(jnp.int32, (CHUNK, CHUNK), 1)
    return r >= c, r > c, jnp.where(r == c, 1.0, 0.0).astype(F32)


def _decay_matrix(col, row, tril):
    return jnp.where(tril, jnp.exp(jnp.where(tril, col - row, 0.0)), 0.0)


INV_BASE = 8


def _mm_shared(lhs_list, b):
    m = lhs_list[0].shape[0]
    out = _dot(jnp.concatenate(lhs_list, axis=0), b)
    return [out[i * m:(i + 1) * m] for i in range(len(lhs_list))]


def _inv_unit_lower_levels(a_list, eye, out):
    size = a_list[0].shape[0]
    r = lax.broadcasted_iota(jnp.int32, (size, size), 0)
    c = lax.broadcasted_iota(jnp.int32, (size, size), 1)
    same = (r // INV_BASE) == (c // INV_BASE)
    n = [jnp.where(same, -a, 0.0) for a in a_list]
    t = [eye + ni for ni in n]
    p = [_dot(ni, ni) for ni in n]
    yield
    terms = 2
    while terms < INV_BASE:
        terms *= 2
        if terms < INV_BASE:
            res = [_mm_shared([pi, ti], pi) for pi, ti in zip(p, t)]
            t = [ti + ri[1] for ti, ri in zip(t, res)]
            p = [ri[0] for ri in res]
        else:
            t = [ti + _dot(ti, pi) for pi, ti in zip(p, t)]
        yield
    b = INV_BASE
    while b < size:
        low = ((r // (2 * b)) == (c // (2 * b))) & ((r // b) % 2 == 1) & ((c // b) % 2 == 0)
        y = [_dot(jnp.where(low, a, 0.0), ti) for a, ti in zip(a_list, t)]
        yield
        t = [ti - _dot(ti, yi) for ti, yi in zip(t, y)]
        yield
        b *= 2
    out.extend(t)


def _interleave(gen, thunks, gen_steps):
    thunks = list(thunks)
    per_step = -(-len(thunks) // gen_steps)
    for _ in gen:
        for th in thunks[:per_step]:
            th()
        thunks = thunks[per_step:]
    for th in thunks:
        th()


def _rms_mm_body(x_ref, g_ref, w_ref, o_ref, hn_ref):
    @pl.when(pl.program_id(1) == 0)
    def _():
        hn_ref[...] = _rmsnorm(x_ref[...], g_ref[...]).astype(BF16)

    o_ref[...] = jnp.dot(hn_ref[...], w_ref[...], preferred_element_type=F32).astype(o_ref.dtype)


def _rms_mm(x, g, w, *, tm, tn, out_dtype=F32):
    t, k = x.shape
    n = w.shape[1]
    tm, tn = min(tm, t), min(tn, n)
    assert t % tm == 0 and n % tn == 0
    return pl.pallas_call(
        _rms_mm_body,
        grid=(t // tm, n // tn),
        in_specs=[pl.BlockSpec((tm, k), lambda i, j: (i, 0)),
                  pl.BlockSpec((1, k), lambda i, j: (0, 0)),
                  pl.BlockSpec((k, tn), lambda i, j: (0, j))],
        out_specs=pl.BlockSpec((tm, tn), lambda i, j: (i, j)),
        out_shape=jax.ShapeDtypeStruct((t, n), out_dtype),
        scratch_shapes=[pltpu.VMEM((tm, k), BF16)],
        compiler_params=_params(2),
        name="rms_matmul",
    )(x, g.reshape(1, k), w)


def _in_proj_body(x_ref, g_ref, w_ref, ws_ref, o_ref, os_ref, hn_ref):
    @pl.when(pl.program_id(1) == 0)
    def _():
        hn_ref[...] = _rmsnorm(x_ref[...], g_ref[...]).astype(BF16)
        os_ref[...] = jnp.dot(hn_ref[...], ws_ref[...], preferred_element_type=F32)

    o_ref[...] = jnp.dot(hn_ref[...], w_ref[...], preferred_element_type=F32).astype(o_ref.dtype)


def _in_proj(x, g, w_main, w_small, *, tm, tn, out_dtype):
    t, k = x.shape
    n, ns = w_main.shape[1], w_small.shape[1]
    tm = min(tm, t)
    assert t % tm == 0 and n % tn == 0
    return pl.pallas_call(
        _in_proj_body,
        grid=(t // tm, n // tn),
        in_specs=[pl.BlockSpec((tm, k), lambda i, j: (i, 0)),
                  pl.BlockSpec((1, k), lambda i, j: (0, 0)),
                  pl.BlockSpec((k, tn), lambda i, j: (0, j)),
                  pl.BlockSpec((k, ns), lambda i, j: (0, 0))],
        out_specs=[pl.BlockSpec((tm, tn), lambda i, j: (i, j)),
                   pl.BlockSpec((tm, ns), lambda i, j: (i, 0))],
        out_shape=[jax.ShapeDtypeStruct((t, n), out_dtype),
                   jax.ShapeDtypeStruct((t, ns), F32)],
        scratch_shapes=[pltpu.VMEM((tm, k), BF16)],
        compiler_params=_params(2),
        name="in_proj",
    )(x, g.reshape(1, k), w_main, w_small)


def _gdn_prompt_body(qkv_ref, gate_ref, sm_ref, cw_ref, par_ref, nw_ref, o_ref, sfin_ref, xbuf, s_ref):
    step = pl.program_id(1)

    @pl.when(step == 0)
    def _():
        xbuf[0:TAIL, :] = jnp.zeros((TAIL, GDN_CH), F32)
        s_ref[...] = jnp.zeros_like(s_ref)

    xbuf[TAIL:TAIL + STEP, :] = qkv_ref[...].astype(F32)
    sm = sm_ref[...]
    beta_all = _sigmoid(sm)
    g_all = -jnp.exp(par_ref[0:1, :]) * _softplus(sm + par_ref[1:2, :])
    gc_all = _chunk_cumsum(g_all)
    gc_t = gc_all.T
    tril, strict, eye = _chunk_masks()
    nw = nw_ref[...]

    def conv(blk, r0):
        ln = slice(blk * LANES, (blk + 1) * LANES)
        acc = cw_ref[CONV_W - 1:CONV_W, ln] * xbuf[pl.ds(TAIL + r0, CHUNK), ln]
        for j in range(CONV_W - 1):
            acc = acc + cw_ref[j:j + 1, ln] * xbuf[pl.ds(TAIL - (CONV_W - 1) + j + r0, CHUNK), ln]
        return _silu(acc)

    def pre_item(c, h):
        r0 = c * CHUNK
        q = conv(h, r0)
        k = conv(GDN_HEADS + h, r0)
        v = conv(2 * GDN_HEADS + h, r0)
        q = q * lax.rsqrt(jnp.sum(q * q, axis=-1, keepdims=True) + EPS) * (GDN_DK ** -0.5)
        k = k * lax.rsqrt(jnp.sum(k * k, axis=-1, keepdims=True) + EPS)
        lg = LANE_GDEC + h
        bcol = beta_all[r0:r0 + CHUNK, LANE_BETA + h:LANE_BETA + h + 1]
        gcol = gc_all[r0:r0 + CHUNK, lg:lg + 1]
        grow = gc_t[lg:lg + 1, r0:r0 + CHUNK]
        glast = gc_all[r0 + CHUNK - 1:r0 + CHUNK, lg:lg + 1]
        decay = _decay_matrix(gcol, grow, tril)
        kb = k * bcol
        egc = jnp.exp(gcol)
        return dict(
            a=jnp.where(strict, _dot_nt(kb, k) * decay, 0.0),
            rhs=jnp.concatenate([v * bcol, kb * egc], axis=1),
            aqk=_dot_nt(q, k) * decay,
            qg=q * egc,
            kd=k * jnp.exp(glast - gcol),
            dch=jnp.exp(glast))

    heads = range(GDN_HEADS)
    states = [s_ref[h] for h in heads]

    def recurrence_thunks(chunks, pre_s, sols):
        box = {}

        def round_ws(i):
            box["ws"] = [_dot(jnp.concatenate([sols[i][h][:, GDN_DV:], pre_s[i][h]["qg"]], axis=0), states[h])
                         for h in heads]

        def round_out(i):
            box["vn"] = [sols[i][h][:, :GDN_DV] - box["ws"][h][:CHUNK] for h in heads]
            box["o", i] = [box["ws"][h][CHUNK:] + _dot(pre_s[i][h]["aqk"], box["vn"][h]) for h in heads]

        def round_state(i):
            for h in heads:
                states[h] = states[h] * pre_s[i][h]["dch"] + _dot_tn(pre_s[i][h]["kd"], box["vn"][h])

        def store(i, h):
            r0 = chunks[i] * CHUNK
            gt = gate_ref[r0:r0 + CHUNK, h * GDN_DV:(h + 1) * GDN_DV].astype(F32)
            o_ref[r0:r0 + CHUNK, h * GDN_DV:(h + 1) * GDN_DV] = (
                _rmsnorm(box["o", i][h], nw) * _silu(gt)).astype(o_ref.dtype)

        ths = []
        for i in range(len(chunks)):
            ths += [functools.partial(f, i) for f in (round_ws, round_out, round_state)]
        return ths + [functools.partial(store, i, h) for i in range(len(chunks)) for h in heads]

    sets = [list(range(s, s + GDN_SET)) for s in range(0, STEP // CHUNK, GDN_SET)]
    doublings, merges = INV_BASE.bit_length() - 2, (CHUNK // INV_BASE).bit_length() - 1
    inv_steps = 1 + doublings + 2 * merges
    pre = {0: [[pre_item(c, h) for h in heads] for c in sets[0]]}
    pending = []
    for s, chunks in enumerate(sets):
        ahead = []
        if s + 1 < len(sets):
            pre[s + 1] = [[] for _ in sets[s + 1]]
            ahead = [functools.partial(lambda ss, i, c, h: pre[ss][i].append(pre_item(c, h)), s + 1, i, c, h)
                     for i, c in enumerate(sets[s + 1]) for h in heads]
        mixed = [th for pair in zip(pending, ahead) for th in pair]
        mixed += pending[len(ahead):] + ahead[len(pending):]
        tinv = []
        _interleave(_inv_unit_lower_levels([p["a"] for pc in pre[s] for p in pc], eye, tinv), mixed, inv_steps)
        sols = [[_dot(tinv[i * GDN_HEADS + h], pre[s][i][h]["rhs"]) for h in heads] for i in range(len(chunks))]
        pending = recurrence_thunks(chunks, pre[s], sols)
    for th in pending:
        th()
    for h in heads:
        s_ref[h] = states[h]

    xbuf[0:TAIL, :] = xbuf[STEP:STEP + TAIL, :]

    @pl.when(step == pl.num_programs(1) - 1)
    def _():
        sfin_ref[0] = s_ref[...]


def _gdn_prompt(proj, small, cw, par, nw, bsz, seq):
    nsteps = seq // STEP
    row = lambda b, s: b * nsteps + s
    return pl.pallas_call(
        _gdn_prompt_body,
        grid=(bsz, nsteps),
        in_specs=[pl.BlockSpec((STEP, GDN_CH), lambda b, s: (row(b, s), OFF_QKV // GDN_CH)),
                  pl.BlockSpec((STEP, GDN_V), lambda b, s: (row(b, s), OFF_GATE // GDN_V)),
                  pl.BlockSpec((STEP, SMALL_COLS), lambda b, s: (row(b, s), 0)),
                  pl.BlockSpec((CONV_W, GDN_CH), lambda b, s: (0, 0)),
                  pl.BlockSpec((SUBLANES, SMALL_COLS), lambda b, s: (0, 0)),
                  pl.BlockSpec((1, GDN_DV), lambda b, s: (0, 0))],
        out_specs=[pl.BlockSpec((STEP, GDN_V), lambda b, s: (row(b, s), 0)),
                   pl.BlockSpec((1, GDN_HEADS, GDN_DK, GDN_DV), lambda b, s: (b, 0, 0, 0))],
        out_shape=[jax.ShapeDtypeStruct((bsz * seq, GDN_V), BF16),
                   jax.ShapeDtypeStruct((bsz, GDN_HEADS, GDN_DK, GDN_DV), F32)],
        scratch_shapes=[pltpu.VMEM((TAIL + STEP, GDN_CH), F32),
                        pltpu.VMEM((GDN_HEADS, GDN_DK, GDN_DV), F32)],
        compiler_params=_params(2),
        name="gdn_prompt",
    )(proj, proj, small, cw, par, nw)


def _ssd_prompt_body(xs_ref, b_ref, c_ref, z_ref, sm_ref, cw_ref, cb_ref, par_ref, d_ref, nw_ref,
                     y_ref, sfin_ref, xb_x, xb_b, xb_c, st_ref):
    step = pl.program_id(1)

    @pl.when(step == 0)
    def _():
        xb_x[0:TAIL, :] = jnp.zeros((TAIL, SSM_DINNER), F32)
        xb_b[0:TAIL, :] = jnp.zeros((TAIL, SSM_BC), F32)
        xb_c[0:TAIL, :] = jnp.zeros((TAIL, SSM_BC), F32)
        st_ref[...] = jnp.zeros_like(st_ref)

    xb_x[TAIL:TAIL + STEP, :] = xs_ref[...].astype(F32)
    xb_b[TAIL:TAIL + STEP, :] = b_ref[...].astype(F32)
    xb_c[TAIL:TAIL + STEP, :] = c_ref[...].astype(F32)

    def conv(buf, r0, lo, width, woff):
        wl = woff + lo
        acc = cw_ref[CONV_W - 1:CONV_W, wl:wl + width] * buf[pl.ds(TAIL + r0, CHUNK), lo:lo + width]
        acc = acc + cb_ref[0:1, wl:wl + width]
        for j in range(CONV_W - 1):
            acc = acc + cw_ref[j:j + 1, wl:wl + width] * buf[pl.ds(TAIL - (CONV_W - 1) + j + r0, CHUNK), lo:lo + width]
        return _silu(acc)

    sm = sm_ref[...]
    dt_all = _softplus(sm + par_ref[0:1, :])
    cs_all = _chunk_cumsum(dt_all * (-jnp.exp(par_ref[1:2, :])))
    cs_t = cs_all.T
    tril, _, _ = _chunk_masks()

    assert SSM_P == CHUNK
    pw = 2 * SSM_P
    npair = SSM_R // 2
    items = [(c, g) for c in range(STEP // CHUNK) for g in range(SSM_GROUPS)]
    pairs = [(c, g, pr) for c, g in items for pr in range(npair)]
    lane_id = lax.broadcasted_iota(jnp.int32, (CHUNK, pw), 1)
    first = lane_id < SSM_P
    first_row = first[0:1]
    r2 = lax.broadcasted_iota(jnp.int32, (pw, pw), 0)
    c2 = lax.broadcasted_iota(jnp.int32, (pw, pw), 1)
    same_head = (r2 < SSM_P) == (c2 < SSM_P)
    tril_pair = lax.broadcasted_iota(jnp.int32, (CHUNK, pw), 0) >= (lane_id % SSM_P)

    def pair_cols(src, c, g, pr):
        l0 = LANE_DT + SSM_R * g + 2 * pr
        r0 = c * CHUNK
        return jnp.where(first, src[r0:r0 + CHUNK, l0:l0 + 1], src[r0:r0 + CHUNK, l0 + 1:l0 + 2])

    def pair_row(row0, row1):
        return jnp.where(first_row, row0, row1)

    dts, cscs, csrs, csls, lms = [], [], [], [], []

    def gating(c, g, pr):
        l0 = LANE_DT + SSM_R * g + 2 * pr
        r0 = c * CHUNK
        seg = (r0 // pw) * pw
        t0 = cs_t[l0:l0 + 1, seg:seg + pw]
        t1 = cs_t[l0 + 1:l0 + 2, seg:seg + pw]
        if r0 == seg:
            csr = pair_row(t0, pltpu.roll(t1, CHUNK, 1))
        else:
            csr = pair_row(pltpu.roll(t0, CHUNK, 1), t1)
        last = r0 + CHUNK - 1
        csc = pair_cols(cs_all, c, g, pr)
        dts.append(pair_cols(dt_all, c, g, pr))
        cscs.append(csc)
        csrs.append(csr)
        csls.append(pair_row(cs_all[last:last + 1, l0:l0 + 1], cs_all[last:last + 1, l0 + 1:l0 + 2]))
        lms.append(_decay_matrix(csc, csr, tril_pair))

    for p in pairs:
        gating(*p)
    bcs = [conv(xb_b, c * CHUNK, g * SSM_N, SSM_N, SSM_DINNER) for c, g in items]
    ccs = [conv(xb_c, c * CHUNK, g * SSM_N, SSM_N, SSM_DINNER + SSM_BC) for c, g in items]
    xss = [conv(xb_x, c * CHUNK, g * SSM_GW + pr * pw, pw, 0) for c, g, pr in pairs]
    cbs = [_dot_nt(cc, jnp.concatenate([bc, bc], axis=0)) for cc, bc in zip(ccs, bcs)]
    xdts = [xs * dt for xs, dt in zip(xss, dts)]
    ms = [cbs[i // npair] * lm for i, lm in enumerate(lms)]
    bds = [jnp.where(same_head, jnp.concatenate([x, x], axis=0), 0.0) for x in xdts]
    ylocs = []
    for (c, g, pr), m, bd, xs in zip(pairs, ms, bds, xss):
        lo = g * SSM_GW + pr * pw
        ylocs.append(_dot(m, bd) + d_ref[0:1, lo:lo + pw] * xs)
    ecss = [jnp.exp(csc) for csc in cscs]
    xds = [xdt * jnp.exp(csl - csc) for xdt, csl, csc in zip(xdts, csls, cscs)]
    dchs = [jnp.exp(csl) for csl in csls]

    def group_cat(vals, i):
        return jnp.concatenate(vals[i * npair:(i + 1) * npair], axis=1)

    outs = []
    for i, (c, g) in enumerate(items):
        st_g = st_ref[g]
        outs.append(group_cat(ylocs, i) + _dot(ccs[i], st_g) * group_cat(ecss, i))
        st_ref[g] = st_g * group_cat(dchs, i) + _dot_tn(bcs[i], group_cat(xds, i))
    for (c, g), y_g in zip(items, outs):
        r0 = c * CHUNK
        y_g = y_g * _silu(z_ref[r0:r0 + CHUNK, g * SSM_GW:(g + 1) * SSM_GW].astype(F32))
        y_ref[r0:r0 + CHUNK, g * SSM_GW:(g + 1) * SSM_GW] = _rmsnorm(
            y_g, nw_ref[0:1, g * SSM_GW:(g + 1) * SSM_GW]).astype(y_ref.dtype)

    xb_x[0:TAIL, :] = xb_x[STEP:STEP + TAIL, :]
    xb_b[0:TAIL, :] = xb_b[STEP:STEP + TAIL, :]
    xb_c[0:TAIL, :] = xb_c[STEP:STEP + TAIL, :]

    @pl.when(step == pl.num_programs(1) - 1)
    def _():
        for g in range(SSM_GROUPS):
            st_t = st_ref[g].T
            for rr in range(SSM_R):
                sfin_ref[0, SSM_R * g + rr] = st_t[rr * SSM_P:(rr + 1) * SSM_P, :]


def _ssd_prompt(proj, small, cw, cb, par, d_row, nw, bsz, seq):
    nsteps = seq // STEP
    row = lambda b, s: b * nsteps + s
    const = lambda b, s: (0, 0)
    return pl.pallas_call(
        _ssd_prompt_body,
        grid=(bsz, nsteps),
        in_specs=[pl.BlockSpec((STEP, SSM_DINNER), lambda b, s: (row(b, s), OFF_XS // SSM_DINNER)),
                  pl.BlockSpec((STEP, SSM_BC), lambda b, s: (row(b, s), OFF_B // SSM_BC)),
                  pl.BlockSpec((STEP, SSM_BC), lambda b, s: (row(b, s), OFF_C // SSM_BC)),
                  pl.BlockSpec((STEP, SSM_DINNER), lambda b, s: (row(b, s), OFF_Z // SSM_DINNER)),
                  pl.BlockSpec((STEP, SMALL_COLS), lambda b, s: (row(b, s), 0)),
                  pl.BlockSpec((CONV_W, SSM_CH), const),
                  pl.BlockSpec((1, SSM_CH), const),
                  pl.BlockSpec((SUBLANES, SMALL_COLS), const),
                  pl.BlockSpec((1, SSM_DINNER), const),
                  pl.BlockSpec((1, SSM_DINNER), const)],
        out_specs=[pl.BlockSpec((STEP, SSM_DINNER), lambda b, s: (row(b, s), 0)),
                   pl.BlockSpec((1, SSM_HEADS, SSM_P, SSM_N), lambda b, s: (b, 0, 0, 0))],
        out_shape=[jax.ShapeDtypeStruct((bsz * seq, SSM_DINNER), BF16),
                   jax.ShapeDtypeStruct((bsz, SSM_HEADS, SSM_P, SSM_N), F32)],
        scratch_shapes=[pltpu.VMEM((TAIL + STEP, SSM_DINNER), F32),
                        pltpu.VMEM((TAIL + STEP, SSM_BC), F32),
                        pltpu.VMEM((TAIL + STEP, SSM_BC), F32),
                        pltpu.VMEM((SSM_GROUPS, SSM_N, SSM_GW), F32)],
        compiler_params=_params(2),
        name="ssd_prompt",
    )(proj, proj, proj, proj, small, cw, cb, par, d_row, nw)


def _gdn_sample_body(qkv_ref, gate_ref, sm_ref, cst_ref, s_ref, cw_ref, par_ref, nw_ref, o_ref, sout_ref):
    bt = qkv_ref.shape[0]
    sm = sm_ref[...]
    beta_all = _sigmoid(sm)
    eg_all = jnp.exp(-jnp.exp(par_ref[0:1, :]) * _softplus(sm + par_ref[1:2, :]))
    rowid = lax.broadcasted_iota(jnp.int32, (bt, GDN_DK), 0)
    nw = nw_ref[...]

    def conv(blk):
        lo = blk * LANES
        acc = cw_ref[CONV_W - 1:CONV_W, lo:lo + LANES] * qkv_ref[:, lo:lo + LANES]
        for j in range(CONV_W - 1):
            acc = acc + cw_ref[j:j + 1, lo:lo + LANES] * cst_ref[:, j * GDN_CH + lo:j * GDN_CH + lo + LANES]
        return _silu(acc)

    for h in range(GDN_HEADS):
        q = conv(h)
        k = conv(GDN_HEADS + h)
        v = conv(2 * GDN_HEADS + h)
        q = q * lax.rsqrt(jnp.sum(q * q, axis=-1, keepdims=True) + EPS) * (GDN_DK ** -0.5)
        k = k * lax.rsqrt(jnp.sum(k * k, axis=-1, keepdims=True) + EPS)
        bcol = beta_all[:, LANE_BETA + h:LANE_BETA + h + 1]
        egcol = eg_all[:, LANE_GDEC + h:LANE_GDEC + h + 1]
        qs_rows, ks_rows = [], []
        for t in range(bt):
            lhs = jnp.where(rowid == 0, q[t:t + 1, :], jnp.where(rowid == 1, k[t:t + 1, :], 0.0))
            r = _dot(lhs, s_ref[t, h])
            qs_rows.append(r[0:1])
            ks_rows.append(r[1:2])
        q_s = jnp.concatenate(qs_rows, axis=0)
        k_s = jnp.concatenate(ks_rows, axis=0)
        v_new = bcol * v - (bcol * egcol) * k_s
        o = egcol * q_s + jnp.sum(q * k, axis=-1, keepdims=True) * v_new
        for t in range(bt):
            k_t = jnp.where(rowid == t, k, 0.0)
            sout_ref[t, h] = s_ref[t, h] * egcol[t:t + 1, :] + _dot_tn(k_t, v_new)
        gt = gate_ref[:, h * GDN_DV:(h + 1) * GDN_DV]
        o_ref[:, h * GDN_DV:(h + 1) * GDN_DV] = (_rmsnorm(o, nw) * _silu(gt)).astype(o_ref.dtype)


def _gdn_sample(proj, small, cstate, state, layer, cw, par, nw):
    t = proj.shape[0]
    bt = GDN_SAMPLE_BT
    blk0 = layer * (t // bt)
    const = lambda i: (0, 0)
    return pl.pallas_call(
        _gdn_sample_body,
        grid=(t // bt,),
        in_specs=[pl.BlockSpec((bt, GDN_CH), lambda i: (i, OFF_QKV // GDN_CH)),
                  pl.BlockSpec((bt, GDN_V), lambda i: (i, OFF_GATE // GDN_V)),
                  pl.BlockSpec((bt, SMALL_COLS), lambda i: (i, 0)),
                  pl.BlockSpec((bt, (CONV_W - 1) * GDN_CH), lambda i: (blk0 + i, 0)),
                  pl.BlockSpec((bt, GDN_HEADS, GDN_DK, GDN_DV), lambda i: (blk0 + i, 0, 0, 0)),
                  pl.BlockSpec((CONV_W, GDN_CH), const),
                  pl.BlockSpec((SUBLANES, SMALL_COLS), const),
                  pl.BlockSpec((1, GDN_DV), const)],
        out_specs=[pl.BlockSpec((bt, GDN_V), lambda i: (i, 0)),
                   pl.BlockSpec((bt, GDN_HEADS, GDN_DK, GDN_DV), lambda i: (i, 0, 0, 0))],
        out_shape=[jax.ShapeDtypeStruct((t, GDN_V), BF16),
                   jax.ShapeDtypeStruct((t,) + state.shape[1:], F32)],
        compiler_params=_params(1),
        name="gdn_sample",
    )(proj, proj, small, cstate, state, cw, par, nw)


def _ssd_sample_body(xs_ref, b_ref, c_ref, z_ref, sm_ref, cst_ref, s_ref, cw_ref, cb_ref, par_ref, d_ref, nw_ref,
                     y_ref, sout_ref):
    bt = xs_ref.shape[0]
    sm = sm_ref[...]
    dt_all = _softplus(sm + par_ref[0:1, :])
    e_all = jnp.exp(dt_all * (-jnp.exp(par_ref[1:2, :])))
    rowid_n = lax.broadcasted_iota(jnp.int32, (bt, SSM_N), 0)
    rowid_g = lax.broadcasted_iota(jnp.int32, (bt, SSM_GW), 0)

    def conv(x_ref, lo, width, woff):
        wl = woff + lo
        acc = cw_ref[CONV_W - 1:CONV_W, wl:wl + width] * x_ref[:, lo:lo + width] + cb_ref[0:1, wl:wl + width]
        for j in range(CONV_W - 1):
            acc = acc + cw_ref[j:j + 1, wl:wl + width] * cst_ref[:, j * SSM_CH + wl:j * SSM_CH + wl + width]
        return _silu(acc)

    for g in range(SSM_GROUPS):
        bc = conv(b_ref, g * SSM_N, SSM_N, SSM_DINNER)
        cc = conv(c_ref, g * SSM_N, SSM_N, SSM_DINNER + SSM_BC)
        xs_g = conv(xs_ref, g * SSM_GW, SSM_GW, 0)
        cb = jnp.sum(cc * bc, axis=-1, keepdims=True)
        yoff_rows = []
        for t in range(bt):
            c_t = jnp.where(rowid_n == 0, cc[t:t + 1, :], 0.0)
            s_tg = jnp.concatenate([s_ref[t, SSM_R * g + rr] for rr in range(SSM_R)], axis=0)
            yoff_rows.append(_dot_nt(c_t, s_tg)[0:1])
        yoff = jnp.concatenate(yoff_rows, axis=0)
        lanes = [LANE_DT + SSM_R * g + rr for rr in range(SSM_R)]
        dt_g = jnp.concatenate([jnp.broadcast_to(dt_all[:, l:l + 1], (bt, SSM_P)) for l in lanes], axis=1)
        e_g = jnp.concatenate([jnp.broadcast_to(e_all[:, l:l + 1], (bt, SSM_P)) for l in lanes], axis=1)
        xdt = xs_g * dt_g
        y = cb * xdt + yoff * e_g + d_ref[0:1, g * SSM_GW:(g + 1) * SSM_GW] * xs_g
        for t in range(bt):
            outer = _dot_tn(jnp.where(rowid_g == t, xdt, 0.0), bc)
            for rr in range(SSM_R):
                h = SSM_R * g + rr
                sout_ref[t, h] = (s_ref[t, h] * e_all[t:t + 1, lanes[rr]:lanes[rr] + 1]
                                  + outer[rr * SSM_P:(rr + 1) * SSM_P, :])
        y = y * _silu(z_ref[:, g * SSM_GW:(g + 1) * SSM_GW])
        y_ref[:, g * SSM_GW:(g + 1) * SSM_GW] = _rmsnorm(
            y, nw_ref[0:1, g * SSM_GW:(g + 1) * SSM_GW]).astype(y_ref.dtype)


def _ssd_sample(proj, small, cstate, state, layer, cw, cb, par, d_row, nw):
    t = proj.shape[0]
    bt = SAMPLE_BT
    blk0 = layer * (t // bt)
    const = lambda i: (0, 0)
    return pl.pallas_call(
        _ssd_sample_body,
        grid=(t // bt,),
        in_specs=[pl.BlockSpec((bt, SSM_DINNER), lambda i: (i, OFF_XS // SSM_DINNER)),
                  pl.BlockSpec((bt, SSM_BC), lambda i: (i, OFF_B // SSM_BC)),
                  pl.BlockSpec((bt, SSM_BC), lambda i: (i, OFF_C // SSM_BC)),
                  pl.BlockSpec((bt, SSM_DINNER), lambda i: (i, OFF_Z // SSM_DINNER)),
                  pl.BlockSpec((bt, SMALL_COLS), lambda i: (i, 0)),
                  pl.BlockSpec((bt, (CONV_W - 1) * SSM_CH), lambda i: (blk0 + i, 0)),
                  pl.BlockSpec((bt, SSM_HEADS, SSM_P, SSM_N), lambda i: (blk0 + i, 0, 0, 0)),
                  pl.BlockSpec((CONV_W, SSM_CH), const),
                  pl.BlockSpec((1, SSM_CH), const),
                  pl.BlockSpec((SUBLANES, SMALL_COLS), const),
                  pl.BlockSpec((1, SSM_DINNER), const),
                  pl.BlockSpec((1, SSM_DINNER), const)],
        out_specs=[pl.BlockSpec((bt, SSM_DINNER), lambda i: (i, 0)),
                   pl.BlockSpec((bt, SSM_HEADS, SSM_P, SSM_N), lambda i: (i, 0, 0, 0))],
        out_shape=[jax.ShapeDtypeStruct((t, SSM_DINNER), BF16),
                   jax.ShapeDtypeStruct((t,) + state.shape[1:], F32)],
        compiler_params=_params(1),
        name="ssd_sample",
    )(proj, proj, proj, proj, small, cstate, state, cw, cb, par, d_row, nw)


def _merge_body(oa_ref, yb_ref, ga_ref, gb_ref, x_ref, wg_ref, ws_ref, wo_ref, o_ref):
    a = jnp.dot(oa_ref[...], wg_ref[...], preferred_element_type=F32)
    b = jnp.dot(yb_ref[...], ws_ref[...], preferred_element_type=F32)
    merged = _sigmoid(ga_ref[...].astype(F32)) * a + _sigmoid(gb_ref[...].astype(F32)) * b
    o_ref[...] = x_ref[...] + _dot(merged, wo_ref[...])


def _merge(o_a, y_b, proj, x, wg, ws, wo, *, tm):
    t = x.shape[0]
    tm = min(tm, t)
    assert t % tm == 0
    const = lambda i: (0, 0)
    return pl.pallas_call(
        _merge_body,
        grid=(t // tm,),
        in_specs=[pl.BlockSpec((tm, GDN_V), lambda i: (i, 0)),
                  pl.BlockSpec((tm, SSM_DINNER), lambda i: (i, 0)),
                  pl.BlockSpec((tm, D_MODEL), lambda i: (i, OFF_GA // D_MODEL)),
                  pl.BlockSpec((tm, D_MODEL), lambda i: (i, OFF_GB // D_MODEL)),
                  pl.BlockSpec((tm, D_MODEL), lambda i: (i, 0)),
                  pl.BlockSpec((GDN_V, D_MODEL), const),
                  pl.BlockSpec((SSM_DINNER, D_MODEL), const),
                  pl.BlockSpec((D_MODEL, D_MODEL), const)],
        out_specs=pl.BlockSpec((tm, D_MODEL), lambda i: (i, 0)),
        out_shape=jax.ShapeDtypeStruct((t, D_MODEL), F32),
        compiler_params=_params(1),
        name="merge_out",
    )(o_a, y_b, proj, proj, x, wg, ws, wo)


def _softmax_rows(s):
    e = jnp.exp(s - jnp.max(s, axis=-1, keepdims=True))
    return e / jnp.sum(e, axis=-1, keepdims=True)


def _xattn_prompt_body(x_ref, gx_ref, wq_ref, wo_ref, k_ref, v_ref, o_ref):
    x = x_ref[...]
    q = _dot(_rmsnorm(x, gx_ref[...]), wq_ref[...])
    outs = []
    for h in range(X_HEADS):
        sl = slice(h * X_HD, (h + 1) * X_HD)
        p = _softmax_rows(_dot_nt(q[:, sl], k_ref[:, sl]) * (X_HD ** -0.5))
        outs.append(_dot(p, v_ref[:, sl]))
    o_ref[...] = x + _dot(jnp.concatenate(outs, axis=1), wo_ref[...])


def _xattn_prompt(x, gx, wq, wo, mk, mv, seq, *, tm):
    t = x.shape[0]
    n_mem = mk.shape[0] // (t // seq)
    tm = min(tm, seq)
    assert seq % tm == 0
    per_seq = seq // tm
    const = lambda i: (0, 0)
    return pl.pallas_call(
        _xattn_prompt_body,
        grid=(t // tm,),
        in_specs=[pl.BlockSpec((tm, D_MODEL), lambda i: (i, 0)),
                  pl.BlockSpec((1, D_MODEL), const),
                  pl.BlockSpec((D_MODEL, D_MODEL), const),
                  pl.BlockSpec((D_MODEL, D_MODEL), const),
                  pl.BlockSpec((n_mem, D_MODEL), lambda i: (i // per_seq, 0)),
                  pl.BlockSpec((n_mem, D_MODEL), lambda i: (i // per_seq, 0))],
        out_specs=pl.BlockSpec((tm, D_MODEL), lambda i: (i, 0)),
        out_shape=jax.ShapeDtypeStruct((t, D_MODEL), F32),
        compiler_params=_params(1),
        name="xattn_prompt",
    )(x, gx.reshape(1, D_MODEL), wq, wo, mk, mv)


def _xattn_sample_body(x_ref, gx_ref, wq_ref, wo_ref, k_ref, v_ref, o_ref, q_scr, a_scr):
    i = pl.program_id(0)
    bt = k_ref.shape[0]

    @pl.when(i == 0)
    def _():
        q_scr[...] = _dot(_rmsnorm(x_ref[...], gx_ref[...]), wq_ref[...])

    assert X_ROWS == SUBLANES and X_HALVES == 2
    nrow = k_ref.shape[1]
    sub = lax.broadcasted_iota(jnp.int32, (X_ROWS, nrow), 0)
    lane = lax.broadcasted_iota(jnp.int32, (X_ROWS, nrow), 1)
    own = (lane % X_ROWS) == sub
    lower = sub < X_HEADS
    toks = range(bt)
    q_rows = []
    for tt in toks:
        q_t = q_scr[pl.ds(i * bt + tt, 1), :]
        q_rows.append(jnp.concatenate(
            [q_t[:, h * X_HD + half * LANES:h * X_HD + (half + 1) * LANES]
             for half in range(X_HALVES) for h in range(X_HEADS)], axis=0))
    es = [jnp.where(own, _dot_nt(q_rows[tt], k_ref[tt]), 0.0) for tt in toks]
    p2s = []
    for e in es:
        s = (e + pltpu.roll(pltpu.roll(e, X_HEADS, 0), nrow - X_HEADS, 1)) * (X_HD ** -0.5)
        s = jnp.where(own, s, -jnp.inf)
        p = jnp.exp(s - jnp.max(s, axis=-1, keepdims=True))
        p = jnp.where(lower, p / jnp.sum(p, axis=-1, keepdims=True), 0.0)
        p2s.append(p + pltpu.roll(pltpu.roll(p, X_HEADS, 0), X_HEADS, 1))
    o2s = [_dot(p2s[tt], v_ref[tt]) for tt in toks]
    for tt in toks:
        a_scr[pl.ds(i * bt + tt, 1), :] = jnp.concatenate(
            [o2s[tt][half * X_HEADS + h:half * X_HEADS + h + 1, :]
             for h in range(X_HEADS) for half in range(X_HALVES)], axis=1)

    @pl.when(i == pl.num_programs(0) - 1)
    def _():
        o_ref[...] = x_ref[...] + _dot(a_scr[...], wo_ref[...])


def _xattn_sample(x, gx, wq, wo, ck, cv, layer):
    t = x.shape[0]
    n_mem = ck.shape[1]
    bt = XATTN_BT
    blk0 = layer * (t // bt)
    const = lambda i: (0, 0)
    return pl.pallas_call(
        _xattn_sample_body,
        grid=(t // bt,),
        in_specs=[pl.BlockSpec((t, D_MODEL), const),
                  pl.BlockSpec((1, D_MODEL), const),
                  pl.BlockSpec((D_MODEL, D_MODEL), const),
                  pl.BlockSpec((D_MODEL, D_MODEL), const),
                  pl.BlockSpec((bt,) + ck.shape[1:], lambda i: (blk0 + i, 0, 0)),
                  pl.BlockSpec((bt,) + cv.shape[1:], lambda i: (blk0 + i, 0, 0))],
        out_specs=pl.BlockSpec((t, D_MODEL), const),
        out_shape=jax.ShapeDtypeStruct((t, D_MODEL), F32),
        scratch_shapes=[pltpu.VMEM((t, D_MODEL), F32), pltpu.VMEM((t, D_MODEL), F32)],
        compiler_params=_params(1),
        name="xattn_sample",
    )(x, gx.reshape(1, D_MODEL), wq, wo, ck, cv)


def _mlp_body(x_ref, g_ref, wu_ref, wd_ref, gf_ref, y_ref, hn_ref, acc_ref, *, final_norm):
    j = pl.program_id(1)

    @pl.when(j == 0)
    def _():
        hn_ref[...] = _rmsnorm(x_ref[...], g_ref[...]).astype(BF16)
        acc_ref[...] = jnp.zeros_like(acc_ref)

    hf = jnp.dot(hn_ref[...], wu_ref[...], preferred_element_type=F32)
    acc_ref[...] += _dot(jnp.square(jnp.maximum(hf, 0.0)), wd_ref[...])

    @pl.when(j == pl.num_programs(1) - 1)
    def _():
        x_new = x_ref[...] + acc_ref[...]
        y_ref[...] = _rmsnorm(x_new, gf_ref[...]) if final_norm else x_new


def _mlp(x, g, wu, wd, gf, *, tm, tf, final_norm):
    t = x.shape[0]
    tm = min(tm, t)
    assert t % tm == 0 and D_FF % tf == 0
    const = lambda i, j: (0, 0)
    return pl.pallas_call(
        functools.partial(_mlp_body, final_norm=final_norm),
        grid=(t // tm, D_FF // tf),
        in_specs=[pl.BlockSpec((tm, D_MODEL), lambda i, j: (i, 0)),
                  pl.BlockSpec((1, D_MODEL), const),
                  pl.BlockSpec((D_MODEL, tf), lambda i, j: (0, j)),
                  pl.BlockSpec((tf, D_MODEL), lambda i, j: (j, 0)),
                  pl.BlockSpec((1, D_MODEL), const)],
        out_specs=pl.BlockSpec((tm, D_MODEL), lambda i, j: (i, 0)),
        out_shape=jax.ShapeDtypeStruct((t, D_MODEL), F32),
        scratch_shapes=[pltpu.VMEM((tm, D_MODEL), BF16), pltpu.VMEM((tm, D_MODEL), F32)],
        compiler_params=_params(2),
        name="mlp",
    )(x, g.reshape(1, D_MODEL), wu, wd, gf.reshape(1, D_MODEL))


def _lane_row(vec, lane0):
    return jnp.zeros((SMALL_COLS,), F32).at[lane0:lane0 + vec.shape[0]].set(vec.astype(F32))


def _param_rows(*rows):
    out = jnp.zeros((SUBLANES, SMALL_COLS), F32)
    for i, r in enumerate(rows):
        out = out.at[i].set(r)
    return out


def kernel(x_prompt, x_sample, mem_prompt, state_gdn_conv, state_gdn, state_ssm_conv, state_ssm, cache_mem_k, cache_mem_v, g_mix, w_in, gdn_conv_w, gdn_A_log, gdn_dt_bias, gdn_norm_w, w_gdn_up, ssm_conv_w, ssm_conv_b, ssm_dt_bias, ssm_A_log, ssm_D, ssm_norm_w, w_ssm_up, w_out, g_mem, w_mk, w_mv, g_x, w_cq, w_co, g_ff, w_ff_up, w_ff_down, g_final):
    bp, seq, _ = x_prompt.shape
    bs, dec_seq, _ = x_sample.shape
    depth = w_in.shape[0]
    n_mem = mem_prompt.shape[1]
    assert seq % STEP == 0 and dec_seq == 1
    assert bs % SAMPLE_BT == 0 and bs % GDN_SAMPLE_BT == 0 and bs % XATTN_BT == 0
    tp = bp * seq

    xp = x_prompt.reshape(tp, D_MODEL)
    xs = x_sample.reshape(bs, D_MODEL)
    mem = mem_prompt.reshape(bp * n_mem, D_MODEL)
    gst_all = state_gdn_conv.astype(F32).reshape(depth * bs, (CONV_W - 1) * GDN_CH)
    sst_all = state_ssm_conv.astype(F32).reshape(depth * bs, (CONV_W - 1) * SSM_CH)
    sg_all = state_gdn.astype(F32).reshape(depth * bs, GDN_HEADS, GDN_DK, GDN_DV)
    ss_all = state_ssm.astype(F32).reshape(depth * bs, SSM_HEADS, SSM_P, SSM_N)
    def cache_rows(c):
        c = c.astype(F32).reshape(depth * bs, n_mem, X_HEADS, X_HALVES, LANES)
        return c.transpose(0, 1, 3, 2, 4).reshape(depth * bs, n_mem * X_ROWS, LANES)

    ck_all = cache_rows(cache_mem_k)
    cv_all = cache_rows(cache_mem_v)
    outs ={k: [] for k in ("p_gc", "p_g", "p_sc", "p_s", "p_mk", "p_mv", "s_gc", "s_g", "s_sc", "s_s")}

    starts = [0]
    for width in IN_SPLITS:
        starts.append(starts[-1] + width)
    col = lambda i: slice(starts[i], starts[i + 1])

    for l in range(depth):
        wl = w_in[l]
        w_main = jnp.concatenate([wl[:, starts[0]:starts[2]], wl[:, starts[4]:starts[6]],
                                  wl[:, starts[7]:starts[9]]], axis=1).astype(BF16)
        w_small = jnp.concatenate(
            [wl[:, col(2)], wl[:, col(3)], wl[:, col(6)],
             jnp.zeros((D_MODEL, SMALL_COLS - 2 * GDN_HEADS - SSM_HEADS), F32)], axis=1).astype(BF16)
        gdn_par = _param_rows(_lane_row(gdn_A_log[l], LANE_GDEC), _lane_row(gdn_dt_bias[l], LANE_GDEC))
        ssm_par = _param_rows(_lane_row(ssm_dt_bias[l], LANE_DT), _lane_row(ssm_A_log[l], LANE_DT))
        gdn_nw = gdn_norm_w[l].reshape(1, GDN_DV).astype(F32)
        ssm_nw = ssm_norm_w[l].reshape(1, SSM_DINNER).astype(F32)
        d_row = jnp.repeat(ssm_D[l].astype(F32), SSM_P).reshape(1, SSM_DINNER)
        gcw = gdn_conv_w[l].astype(F32)
        scw = ssm_conv_w[l].astype(F32)
        scb = ssm_conv_b[l].reshape(1, SSM_CH).astype(F32)
        wg, ws, wo = w_gdn_up[l].astype(BF16), w_ssm_up[l].astype(BF16), w_out[l].astype(BF16)
        wq, wco = w_cq[l].astype(BF16), w_co[l].astype(BF16)
        wu, wd = w_ff_up[l].astype(BF16), w_ff_down[l].astype(BF16)
        last = l == depth - 1

        pt, st = PROMPT_TILES, SAMPLE_TILES
        mk = _rms_mm(mem, g_mem[l], w_mk[l].astype(BF16), tm=pt["kv"][0], tn=pt["kv"][1])
        mv = _rms_mm(mem, g_mem[l], w_mv[l].astype(BF16), tm=pt["kv"][0], tn=pt["kv"][1])
        outs["p_mk"].append(mk.reshape(bp, n_mem, X_HEADS, X_HD))
        outs["p_mv"].append(mv.reshape(bp, n_mem, X_HEADS, X_HD))

        proj, small = _in_proj(xp, g_mix[l], w_main, w_small, tm=pt["in_proj"][0], tn=pt["in_proj"][1],
                               out_dtype=BF16)
        o_a, p_g = _gdn_prompt(proj, small, gcw, gdn_par, gdn_nw, bp, seq)
        y_b, p_s = _ssd_prompt(proj, small, scw, scb, ssm_par, d_row, ssm_nw, bp, seq)
        proj3 = proj.reshape(bp, seq, MAIN_COLS)
        outs["p_gc"].append(proj3[:, seq - (CONV_W - 1):, OFF_QKV:OFF_QKV + GDN_CH].astype(F32))
        outs["p_sc"].append(proj3[:, seq - (CONV_W - 1):, OFF_XS:OFF_XS + SSM_CH].astype(F32))
        outs["p_g"].append(p_g)
        outs["p_s"].append(p_s)
        xp = _merge(o_a, y_b, proj, xp, wg, ws, wo, tm=pt["merge"])
        xp = _xattn_prompt(xp, g_x[l], wq, wco, mk, mv, seq, tm=pt["xattn"])
        xp = _mlp(xp, g_ff[l], wu, wd, g_final, tm=pt["mlp"][0], tf=pt["mlp"][1], final_norm=last)

        proj, small = _in_proj(xs, g_mix[l], w_main, w_small, tm=st["in_proj"][0], tn=st["in_proj"][1],
                               out_dtype=F32)
        o_a, s_g = _gdn_sample(proj, small, gst_all, sg_all, l, gcw, gdn_par, gdn_nw)
        y_b, s_s = _ssd_sample(proj, small, sst_all, ss_all, l, scw, scb, ssm_par, d_row, ssm_nw)
        outs["s_gc"].append(jnp.concatenate(
            [state_gdn_conv[l, :, 1:].astype(F32), proj[:, None, OFF_QKV:OFF_QKV + GDN_CH]], axis=1))
        outs["s_sc"].append(jnp.concatenate(
            [state_ssm_conv[l, :, 1:].astype(F32), proj[:, None, OFF_XS:OFF_XS + SSM_CH]], axis=1))
        outs["s_g"].append(s_g)
        outs["s_s"].append(s_s)
        xs = _merge(o_a, y_b, proj, xs, wg, ws, wo, tm=st["merge"])
        xs = _xattn_sample(xs, g_x[l], wq, wco, ck_all, cv_all, l)
        xs = _mlp(xs, g_ff[l], wu, wd, g_final, tm=st["mlp"][0], tf=st["mlp"][1], final_norm=last)

    stack = lambda k: jnp.stack(outs[k])
    return (xp.reshape(bp, seq, D_MODEL), xs.reshape(bs, dec_seq, D_MODEL),
            stack("p_gc"), stack("p_g"), stack("p_sc"), stack("p_s"), stack("p_mk"), stack("p_mv"),
            stack("s_gc"), stack("s_g"), stack("s_sc"), stack("s_s"))
```

```python
import functools

import jax
import jax.numpy as jnp
from jax import lax
from jax.experimental import pallas as pl
from jax.experimental.pallas import tpu as pltpu

F32 = jnp.float32
BF16 = jnp.bfloat16

EPS = 1e-6
CHUNK = 64
CONV_W = 4
D_MODEL = 1024
GDN_HEADS = 8
GDN_DK = 128
GDN_DV = 128
GDN_QK = GDN_HEADS * GDN_DK
GDN_V = GDN_HEADS * GDN_DV
GDN_CH = 2 * GDN_QK + GDN_V
SSM_DINNER = 2 * D_MODEL
SSM_P = 64
SSM_HEADS = SSM_DINNER // SSM_P
SSM_GROUPS = 8
SSM_R = SSM_HEADS // SSM_GROUPS
SSM_N = 128
SSM_BC = SSM_GROUPS * SSM_N
SSM_CH = SSM_DINNER + 2 * SSM_BC
SSM_GW = SSM_R * SSM_P
X_HEADS = 4
X_HD = D_MODEL // X_HEADS
D_FF = 4 * D_MODEL
IN_SPLITS = (GDN_CH, GDN_V, GDN_HEADS, GDN_HEADS, SSM_DINNER, SSM_CH, SSM_HEADS, D_MODEL, D_MODEL)

MAIN_COLS = GDN_CH + GDN_V + SSM_DINNER + SSM_CH + 2 * D_MODEL
OFF_QKV, OFF_GATE, OFF_Z = 0, GDN_CH, GDN_CH + GDN_V
OFF_XS = OFF_Z + SSM_DINNER
OFF_B = OFF_XS + SSM_DINNER
OFF_C = OFF_B + SSM_BC
OFF_GA = OFF_C + SSM_BC
OFF_GB = OFF_GA + D_MODEL
LANES = 128
SUBLANES = 8
X_HALVES = X_HD // LANES
X_ROWS = X_HEADS * X_HALVES
SMALL_COLS = LANES
LANE_BETA, LANE_GDEC, LANE_DT = 0, GDN_HEADS, 2 * GDN_HEADS

STEP = 8 * CHUNK
GDN_SET = 2
TAIL = SUBLANES
SAMPLE_BT = 8
GDN_SAMPLE_BT = 16
XATTN_BT = 8
VMEM_LIMIT = 54 * 1024 * 1024
PROMPT_TILES = dict(kv=(1024, 1024), in_proj=(2048, 1536), merge=512, xattn=1024, mlp=(1024, 1024))
SAMPLE_TILES = dict(in_proj=(128, 2048), merge=128, mlp=(128, 2048))


def _params(n_axes):
    return pltpu.CompilerParams(dimension_semantics=("arbitrary",) * n_axes, vmem_limit_bytes=VMEM_LIMIT)


def _sigmoid(x):
    return 0.5 * jnp.tanh(0.5 * x) + 0.5


def _silu(x):
    h = 0.5 * x
    return h + h * jnp.tanh(h)


def _softplus(x):
    return jnp.maximum(x, 0.0) + jnp.log1p(jnp.exp(-jnp.abs(x)))


def _rmsnorm(x, g):
    xf = x.astype(F32)
    return xf * lax.rsqrt(jnp.mean(xf * xf, axis=-1, keepdims=True) + EPS) * g


def _dot(a, b):
    return jnp.dot(a.astype(BF16), b.astype(BF16), preferred_element_type=F32)


def _dot_nt(a, b):
    return lax.dot_general(a.astype(BF16), b.astype(BF16), (((1,), (1,)), ((), ())), preferred_element_type=F32)


def _dot_tn(a, b):
    return lax.dot_general(a.astype(BF16), b.astype(BF16), (((0,), (0,)), ((), ())), preferred_element_type=F32)


def _chunk_cumsum(x):
    n = x.shape[0]
    r = lax.broadcasted_iota(jnp.int32, (n, n), 0)
    c = lax.broadcasted_iota(jnp.int32, (n, n), 1)
    tri = jnp.where((r >= c) & ((r // CHUNK) == (c // CHUNK)), 1.0, 0.0).astype(BF16)
    h1 = x.astype(BF16)
    r1 = x - h1.astype(F32)
    h2 = r1.astype(BF16)
    h3 = (r1 - h2.astype(F32)).astype(BF16)
    d = functools.partial(jnp.dot, preferred_element_type=F32)
    return d(tri, h1) + (d(tri, h2) + d(tri, h3))


def _chunk_masks():
    r = lax.broadcasted_iota(jnp.int32, (CHUNK, CHUNK), 0)
    c = lax.broadcasted_iota(jnp.int32, (CHUNK, CHUNK), 1)
    return r >= c, r > c, jnp.where(r == c, 1.0, 0.0).astype(F32)


def _decay_matrix(col, row, tril):
    return jnp.where(tril, jnp.exp(jnp.where(tril, col - row, 0.0)), 0.0)


INV_BASE = 8


def _mm_shared(lhs_list, b):
    m = lhs_list[0].shape[0]
    out = _dot(jnp.concatenate(lhs_list, axis=0), b)
    return [out[i * m:(i + 1) * m] for i in range(len(lhs_list))]


def _inv_unit_lower_levels(a_list, eye, out):
    size = a_list[0].shape[0]
    r = lax.broadcasted_iota(jnp.int32, (size, size), 0)
    c = lax.broadcasted_iota(jnp.int32, (size, size), 1)
    same = (r // INV_BASE) == (c // INV_BASE)
    n = [jnp.where(same, -a, 0.0) for a in a_list]
    t = [eye + ni for ni in n]
    p = [_dot(ni, ni) for ni in n]
    yield
    terms = 2
    while terms < INV_BASE:
        terms *= 2
        if terms < INV_BASE:
            res = [_mm_shared([pi, ti], pi) for pi, ti in zip(p, t)]
            t = [ti + ri[1] for ti, ri in zip(t, res)]
            p = [ri[0] for ri in res]
        else:
            t = [ti + _dot(ti, pi) for pi, ti in zip(p, t)]
        yield
    b = INV_BASE
    while b < size:
        low = ((r // (2 * b)) == (c // (2 * b))) & ((r // b) % 2 == 1) & ((c // b) % 2 == 0)
        y = [_dot(jnp.where(low, a, 0.0), ti) for a, ti in zip(a_list, t)]
        yield
        t = [ti - _dot(ti, yi) for ti, yi in zip(t, y)]
        yield
        b *= 2
    out.extend(t)


def _interleave(gen, thunks, gen_steps):
    thunks = list(thunks)
    per_step = -(-len(thunks) // gen_steps)
    for _ in gen:
        for th in thunks[:per_step]:
            th()
        thunks = thunks[per_step:]
    for th in thunks:
        th()


def _rms_mm_body(x_ref, g_ref, w_ref, o_ref, hn_ref):
    @pl.when(pl.program_id(1) == 0)
    def _():
        hn_ref[...] = _rmsnorm(x_ref[...], g_ref[...]).astype(BF16)

    o_ref[...] = jnp.dot(hn_ref[...], w_ref[...], preferred_element_type=F32).astype(o_ref.dtype)


def _rms_mm(x, g, w, *, tm, tn, out_dtype=F32):
    t, k = x.shape
    n = w.shape[1]
    tm, tn = min(tm, t), min(tn, n)
    assert t % tm == 0 and n % tn == 0
    return pl.pallas_call(
        _rms_mm_body,
        grid=(t // tm, n // tn),
        in_specs=[pl.BlockSpec((tm, k), lambda i, j: (i, 0)),
                  pl.BlockSpec((1, k), lambda i, j: (0, 0)),
                  pl.BlockSpec((k, tn), lambda i, j: (0, j))],
        out_specs=pl.BlockSpec((tm, tn), lambda i, j: (i, j)),
        out_shape=jax.ShapeDtypeStruct((t, n), out_dtype),
        scratch_shapes=[pltpu.VMEM((tm, k), BF16)],
        compiler_params=_params(2),
        name="rms_matmul",
    )(x, g.reshape(1, k), w)


def _in_proj_body(x_ref, g_ref, w_ref, ws_ref, o_ref, os_ref, hn_ref):
    @pl.when(pl.program_id(1) == 0)
    def _():
        hn_ref[...] = _rmsnorm(x_ref[...], g_ref[...]).astype(BF16)
        os_ref[...] = jnp.dot(hn_ref[...], ws_ref[...], preferred_element_type=F32)

    o_ref[...] = jnp.dot(hn_ref[...], w_ref[...], preferred_element_type=F32).astype(o_ref.dtype)


def _in_proj(x, g, w_main, w_small, *, tm, tn, out_dtype):
    t, k = x.shape
    n, ns = w_main.shape[1], w_small.shape[1]
    tm = min(tm, t)
    assert t % tm == 0 and n % tn == 0
    return pl.pallas_call(
        _in_proj_body,
        grid=(t // tm, n // tn),
        in_specs=[pl.BlockSpec((tm, k), lambda i, j: (i, 0)),
                  pl.BlockSpec((1, k), lambda i, j: (0, 0)),
                  pl.BlockSpec((k, tn), lambda i, j: (0, j)),
                  pl.BlockSpec((k, ns), lambda i, j: (0, 0))],
        out_specs=[pl.BlockSpec((tm, tn), lambda i, j: (i, j)),
                   pl.BlockSpec((tm, ns), lambda i, j: (i, 0))],
        out_shape=[jax.ShapeDtypeStruct((t, n), out_dtype),
                   jax.ShapeDtypeStruct((t, ns), F32)],
        scratch_shapes=[pltpu.VMEM((tm, k), BF16)],
        compiler_params=_params(2),
        name="in_proj",
    )(x, g.reshape(1, k), w_main, w_small)


def _gdn_prompt_body(qkv_ref, gate_ref, sm_ref, cw_ref, par_ref, nw_ref, o_ref, sfin_ref, xbuf, s_ref):
    step = pl.program_id(1)

    @pl.when(step == 0)
    def _():
        xbuf[0:TAIL, :] = jnp.zeros((TAIL, GDN_CH), F32)
        s_ref[...] = jnp.zeros_like(s_ref)

    xbuf[TAIL:TAIL + STEP, :] = qkv_ref[...].astype(F32)
    sm = sm_ref[...]
    beta_all = _sigmoid(sm)
    g_all = -jnp.exp(par_ref[0:1, :]) * _softplus(sm + par_ref[1:2, :])
    gc_all = _chunk_cumsum(g_all)
    gc_t = gc_all.T
    tril, strict, eye = _chunk_masks()
    nw = nw_ref[...]

    def conv(blk, r0):
        ln = slice(blk * LANES, (blk + 1) * LANES)
        acc = cw_ref[CONV_W - 1:CONV_W, ln] * xbuf[pl.ds(TAIL + r0, CHUNK), ln]
        for j in range(CONV_W - 1):
            acc = acc + cw_ref[j:j + 1, ln] * xbuf[pl.ds(TAIL - (CONV_W - 1) + j + r0, CHUNK), ln]
        return _silu(acc)

    def pre_item(c, h):
        r0 = c * CHUNK
        q = conv(h, r0)
        k = conv(GDN_HEADS + h, r0)
        v = conv(2 * GDN_HEADS + h, r0)
        q = q * lax.rsqrt(jnp.sum(q * q, axis=-1, keepdims=True) + EPS) * (GDN_DK ** -0.5)
        k = k * lax.rsqrt(jnp.sum(k * k, axis=-1, keepdims=True) + EPS)
        lg = LANE_GDEC + h
        bcol = beta_all[r0:r0 + CHUNK, LANE_BETA + h:LANE_BETA + h + 1]
        gcol = gc_all[r0:r0 + CHUNK, lg:lg + 1]
        grow = gc_t[lg:lg + 1, r0:r0 + CHUNK]
        glast = gc_all[r0 + CHUNK - 1:r0 + CHUNK, lg:lg + 1]
        decay = _decay_matrix(gcol, grow, tril)
        kb = k * bcol
        egc = jnp.exp(gcol)
        return dict(
            a=jnp.where(strict, _dot_nt(kb, k) * decay, 0.0),
            rhs=jnp.concatenate([v * bcol, kb * egc], axis=1),
            aqk=_dot_nt(q, k) * decay,
            qg=q * egc,
            kd=k * jnp.exp(glast - gcol),
            dch=jnp.exp(glast))

    heads = range(GDN_HEADS)
    states = [s_ref[h] for h in heads]

    def recurrence_thunks(chunks, pre_s, sols):
        box = {}

        def round_ws(i):
            box["ws"] = [_dot(jnp.concatenate([sols[i][h][:, GDN_DV:], pre_s[i][h]["qg"]], axis=0), states[h])
                         for h in heads]

        def round_out(i):
            box["vn"] = [sols[i][h][:, :GDN_DV] - box["ws"][h][:CHUNK] for h in heads]
            box["o", i] = [box["ws"][h][CHUNK:] + _dot(pre_s[i][h]["aqk"], box["vn"][h]) for h in heads]

        def round_state(i):
            for h in heads:
                states[h] = states[h] * pre_s[i][h]["dch"] + _dot_tn(pre_s[i][h]["kd"], box["vn"][h])

        def store(i, h):
            r0 = chunks[i] * CHUNK
            gt = gate_ref[r0:r0 + CHUNK, h * GDN_DV:(h + 1) * GDN_DV].astype(F32)
            o_ref[r0:r0 + CHUNK, h * GDN_DV:(h + 1) * GDN_DV] = (
                _rmsnorm(box["o", i][h], nw) * _silu(gt)).astype(o_ref.dtype)

        ths = []
        for i in range(len(chunks)):
            ths += [functools.partial(f, i) for f in (round_ws, round_out, round_state)]
        return ths + [functools.partial(store, i, h) for i in range(len(chunks)) for h in heads]

    sets = [list(range(s, s + GDN_SET)) for s in range(0, STEP // CHUNK, GDN_SET)]
    doublings, merges = INV_BASE.bit_length() - 2, (CHUNK // INV_BASE).bit_length() - 1
    inv_steps = 1 + doublings + 2 * merges
    pre = {0: [[pre_item(c, h) for h in heads] for c in sets[0]]}
    pending = []
    for s, chunks in enumerate(sets):
        ahead = []
        if s + 1 < len(sets):
            pre[s + 1] = [[] for _ in sets[s + 1]]
            ahead = [functools.partial(lambda ss, i, c, h: pre[ss][i].append(pre_item(c, h)), s + 1, i, c, h)
                     for i, c in enumerate(sets[s + 1]) for h in heads]
        mixed = [th for pair in zip(pending, ahead) for th in pair]
        mixed += pending[len(ahead):] + ahead[len(pending):]
        tinv = []
        _interleave(_inv_unit_lower_levels([p["a"] for pc in pre[s] for p in pc], eye, tinv), mixed, inv_steps)
        sols = [[_dot(tinv[i * GDN_HEADS + h], pre[s][i][h]["rhs"]) for h in heads] for i in range(len(chunks))]
        pending = recurrence_thunks(chunks, pre[s], sols)
    for th in pending:
        th()
    for h in heads:
        s_ref[h] = states[h]

    xbuf[0:TAIL, :] = xbuf[STEP:STEP + TAIL, :]

    @pl.when(step == pl.num_programs(1) - 1)
    def _():
        sfin_ref[0] = s_ref[...]


def _gdn_prompt(proj, small, cw, par, nw, bsz, seq):
    nsteps = seq // STEP
    row = lambda b, s: b * nsteps + s
    return pl.pallas_call(
        _gdn_prompt_body,
        grid=(bsz, nsteps),
        in_specs=[pl.BlockSpec((STEP, GDN_CH), lambda b, s: (row(b, s), OFF_QKV // GDN_CH)),
                  pl.BlockSpec((STEP, GDN_V), lambda b, s: (row(b, s), OFF_GATE // GDN_V)),
                  pl.BlockSpec((STEP, SMALL_COLS), lambda b, s: (row(b, s), 0)),
                  pl.BlockSpec((CONV_W, GDN_CH), lambda b, s: (0, 0)),
                  pl.BlockSpec((SUBLANES, SMALL_COLS), lambda b, s: (0, 0)),
                  pl.BlockSpec((1, GDN_DV), lambda b, s: (0, 0))],
        out_specs=[pl.BlockSpec((STEP, GDN_V), lambda b, s: (row(b, s), 0)),
                   pl.BlockSpec((1, GDN_HEADS, GDN_DK, GDN_DV), lambda b, s: (b, 0, 0, 0))],
        out_shape=[jax.ShapeDtypeStruct((bsz * seq, GDN_V), BF16),
                   jax.ShapeDtypeStruct((bsz, GDN_HEADS, GDN_DK, GDN_DV), F32)],
        scratch_shapes=[pltpu.VMEM((TAIL + STEP, GDN_CH), F32),
                        pltpu.VMEM((GDN_HEADS, GDN_DK, GDN_DV), F32)],
        compiler_params=_params(2),
        name="gdn_prompt",
    )(proj, proj, small, cw, par, nw)


def _ssd_prompt_body(xs_ref, b_ref, c_ref, z_ref, sm_ref, cw_ref, cb_ref, par_ref, d_ref, nw_ref,
                     y_ref, sfin_ref, xb_x, xb_b, xb_c, st_ref):
    step = pl.program_id(1)

    @pl.when(step == 0)
    def _():
        xb_x[0:TAIL, :] = jnp.zeros((TAIL, SSM_DINNER), F32)
        xb_b[0:TAIL, :] = jnp.zeros((TAIL, SSM_BC), F32)
        xb_c[0:TAIL, :] = jnp.zeros((TAIL, SSM_BC), F32)
        st_ref[...] = jnp.zeros_like(st_ref)

    xb_x[TAIL:TAIL + STEP, :] = xs_ref[...].astype(F32)
    xb_b[TAIL:TAIL + STEP, :] = b_ref[...].astype(F32)
    xb_c[TAIL:TAIL + STEP, :] = c_ref[...].astype(F32)

    def conv(buf, r0, lo, width, woff):
        wl = woff + lo
        acc = cw_ref[CONV_W - 1:CONV_W, wl:wl + width] * buf[pl.ds(TAIL + r0, CHUNK), lo:lo + width]
        acc = acc + cb_ref[0:1, wl:wl + width]
        for j in range(CONV_W - 1):
            acc = acc + cw_ref[j:j + 1, wl:wl + width] * buf[pl.ds(TAIL - (CONV_W - 1) + j + r0, CHUNK), lo:lo + width]
        return _silu(acc)

    sm = sm_ref[...]
    dt_all = _softplus(sm + par_ref[0:1, :])
    cs_all = _chunk_cumsum(dt_all * (-jnp.exp(par_ref[1:2, :])))
    cs_t = cs_all.T
    tril, _, _ = _chunk_masks()

    assert SSM_P == CHUNK
    pw = 2 * SSM_P
    npair = SSM_R // 2
    items = [(c, g) for c in range(STEP // CHUNK) for g in range(SSM_GROUPS)]
    pairs = [(c, g, pr) for c, g in items for pr in range(npair)]
    lane_id = lax.broadcasted_iota(jnp.int32, (CHUNK, pw), 1)
    first = lane_id < SSM_P
    first_row = first[0:1]
    r2 = lax.broadcasted_iota(jnp.int32, (pw, pw), 0)
    c2 = lax.broadcasted_iota(jnp.int32, (pw, pw), 1)
    same_head = (r2 < SSM_P) == (c2 < SSM_P)
    tril_pair = lax.broadcasted_iota(jnp.int32, (CHUNK, pw), 0) >= (lane_id % SSM_P)

    def pair_cols(src, c, g, pr):
        l0 = LANE_DT + SSM_R * g + 2 * pr
        r0 = c * CHUNK
        return jnp.where(first, src[r0:r0 + CHUNK, l0:l0 + 1], src[r0:r0 + CHUNK, l0 + 1:l0 + 2])

    def pair_row(row0, row1):
        return jnp.where(first_row, row0, row1)

    dts, cscs, csrs, csls, lms = [], [], [], [], []

    def gating(c, g, pr):
        l0 = LANE_DT + SSM_R * g + 2 * pr
        r0 = c * CHUNK
        seg = (r0 // pw) * pw
        t0 = cs_t[l0:l0 + 1, seg:seg + pw]
        t1 = cs_t[l0 + 1:l0 + 2, seg:seg + pw]
        if r0 == seg:
            csr = pair_row(t0, pltpu.roll(t1, CHUNK, 1))
        else:
            csr = pair_row(pltpu.roll(t0, CHUNK, 1), t1)
        last = r0 + CHUNK - 1
        csc = pair_cols(cs_all, c, g, pr)
        dts.append(pair_cols(dt_all, c, g, pr))
        cscs.append(csc)
        csrs.append(csr)
        csls.append(pair_row(cs_all[last:last + 1, l0:l0 + 1], cs_all[last:last + 1, l0 + 1:l0 + 2]))
        lms.append(_decay_matrix(csc, csr, tril_pair))

    for p in pairs:
        gating(*p)
    bcs = [conv(xb_b, c * CHUNK, g * SSM_N, SSM_N, SSM_DINNER) for c, g in items]
    ccs = [conv(xb_c, c * CHUNK, g * SSM_N, SSM_N, SSM_DINNER + SSM_BC) for c, g in items]
    xss = [conv(xb_x, c * CHUNK, g * SSM_GW + pr * pw, pw, 0) for c, g, pr in pairs]
    cbs = [_dot_nt(cc, jnp.concatenate([bc, bc], axis=0)) for cc, bc in zip(ccs, bcs)]
    xdts = [xs * dt for xs, dt in zip(xss, dts)]
    ms = [cbs[i // npair] * lm for i, lm in enumerate(lms)]
    bds = [jnp.where(same_head, jnp.concatenate([x, x], axis=0), 0.0) for x in xdts]
    ylocs = []
    for (c, g, pr), m, bd, xs in zip(pairs, ms, bds, xss):
        lo = g * SSM_GW + pr * pw
        ylocs.append(_dot(m, bd) + d_ref[0:1, lo:lo + pw] * xs)
    ecss = [jnp.exp(csc) for csc in cscs]
    xds = [xdt * jnp.exp(csl - csc) for xdt, csl, csc in zip(xdts, csls, cscs)]
    dchs = [jnp.exp(csl) for csl in csls]

    def group_cat(vals, i):
        return jnp.concatenate(vals[i * npair:(i + 1) * npair], axis=1)

    outs = []
    for i, (c, g) in enumerate(items):
        st_g = st_ref[g]
        outs.append(group_cat(ylocs, i) + _dot(ccs[i], st_g) * group_cat(ecss, i))
        st_ref[g] = st_g * group_cat(dchs, i) + _dot_tn(bcs[i], group_cat(xds, i))
    for (c, g), y_g in zip(items, outs):
        r0 = c * CHUNK
        y_g = y_g * _silu(z_ref[r0:r0 + CHUNK, g * SSM_GW:(g + 1) * SSM_GW].astype(F32))
        y_ref[r0:r0 + CHUNK, g * SSM_GW:(g + 1) * SSM_GW] = _rmsnorm(
            y_g, nw_ref[0:1, g * SSM_GW:(g + 1) * SSM_GW]).astype(y_ref.dtype)

    xb_x[0:TAIL, :] = xb_x[STEP:STEP + TAIL, :]
    xb_b[0:TAIL, :] = xb_b[STEP:STEP + TAIL, :]
    xb_c[0:TAIL, :] = xb_c[STEP:STEP + TAIL, :]

    @pl.when(step == pl.num_programs(1) - 1)
    def _():
        for g in range(SSM_GROUPS):
            st_t = st_ref[g].T
            for rr in range(SSM_R):
                sfin_ref[0, SSM_R * g + rr] = st_t[rr * SSM_P:(rr + 1) * SSM_P, :]


def _ssd_prompt(proj, small, cw, cb, par, d_row, nw, bsz, seq):
    nsteps = seq // STEP
    row = lambda b, s: b * nsteps + s
    const = lambda b, s: (0, 0)
    return pl.pallas_call(
        _ssd_prompt_body,
        grid=(bsz, nsteps),
        in_specs=[pl.BlockSpec((STEP, SSM_DINNER), lambda b, s: (row(b, s), OFF_XS // SSM_DINNER)),
                  pl.BlockSpec((STEP, SSM_BC), lambda b, s: (row(b, s), OFF_B // SSM_BC)),
                  pl.BlockSpec((STEP, SSM_BC), lambda b, s: (row(b, s), OFF_C // SSM_BC)),
                  pl.BlockSpec((STEP, SSM_DINNER), lambda b, s: (row(b, s), OFF_Z // SSM_DINNER)),
                  pl.BlockSpec((STEP, SMALL_COLS), lambda b, s: (row(b, s), 0)),
                  pl.BlockSpec((CONV_W, SSM_CH), const),
                  pl.BlockSpec((1, SSM_CH), const),
                  pl.BlockSpec((SUBLANES, SMALL_COLS), const),
                  pl.BlockSpec((1, SSM_DINNER), const),
                  pl.BlockSpec((1, SSM_DINNER), const)],
        out_specs=[pl.BlockSpec((STEP, SSM_DINNER), lambda b, s: (row(b, s), 0)),
                   pl.BlockSpec((1, SSM_HEADS, SSM_P, SSM_N), lambda b, s: (b, 0, 0, 0))],
        out_shape=[jax.ShapeDtypeStruct((bsz * seq, SSM_DINNER), BF16),
                   jax.ShapeDtypeStruct((bsz, SSM_HEADS, SSM_P, SSM_N), F32)],
        scratch_shapes=[pltpu.VMEM((TAIL + STEP, SSM_DINNER), F32),
                        pltpu.VMEM((TAIL + STEP, SSM_BC), F32),
                        pltpu.VMEM((TAIL + STEP, SSM_BC), F32),
                        pltpu.VMEM((SSM_GROUPS, SSM_N, SSM_GW), F32)],
        compiler_params=_params(2),
        name="ssd_prompt",
    )(proj, proj, proj, proj, small, cw, cb, par, d_row, nw)


def _gdn_sample_body(qkv_ref, gate_ref, sm_ref, cst_ref, s_ref, cw_ref, par_ref, nw_ref, o_ref, sout_ref):
    bt = qkv_ref.shape[0]
    sm = sm_ref[...]
    beta_all = _sigmoid(sm)
    eg_all = jnp.exp(-jnp.exp(par_ref[0:1, :]) * _softplus(sm + par_ref[1:2, :]))
    rowid = lax.broadcasted_iota(jnp.int32, (bt, GDN_DK), 0)
    nw = nw_ref[...]

    def conv(blk):
        lo = blk * LANES
        acc = cw_ref[CONV_W - 1:CONV_W, lo:lo + LANES] * qkv_ref[:, lo:lo + LANES]
        for j in range(CONV_W - 1):
            acc = acc + cw_ref[j:j + 1, lo:lo + LANES] * cst_ref[:, j * GDN_CH + lo:j * GDN_CH + lo + LANES]
        return _silu(acc)

    for h in range(GDN_HEADS):
        q = conv(h)
        k = conv(GDN_HEADS + h)
        v = conv(2 * GDN_HEADS + h)
        q = q * lax.rsqrt(jnp.sum(q * q, axis=-1, keepdims=True) + EPS) * (GDN_DK ** -0.5)
        k = k * lax.rsqrt(jnp.sum(k * k, axis=-1, keepdims=True) + EPS)
        bcol = beta_all[:, LANE_BETA + h:LANE_BETA + h + 1]
        egcol = eg_all[:, LANE_GDEC + h:LANE_GDEC + h + 1]
        qs_rows, ks_rows = [], []
        for t in range(bt):
            lhs = jnp.where(rowid == 0, q[t:t + 1, :], jnp.where(rowid == 1, k[t:t + 1, :], 0.0))
            r = _dot(lhs, s_ref[t, h])
            qs_rows.append(r[0:1])
            ks_rows.append(r[1:2])
        q_s = jnp.concatenate(qs_rows, axis=0)
        k_s = jnp.concatenate(ks_rows, axis=0)
        v_new = bcol * v - (bcol * egcol) * k_s
        o = egcol * q_s + jnp.sum(q * k, axis=-1, keepdims=True) * v_new
        for t in range(bt):
            k_t = jnp.where(rowid == t, k, 0.0)
            sout_ref[t, h] = s_ref[t, h] * egcol[t:t + 1, :] + _dot_tn(k_t, v_new)
        gt = gate_ref[:, h * GDN_DV:(h + 1) * GDN_DV]
        o_ref[:, h * GDN_DV:(h + 1) * GDN_DV] = (_rmsnorm(o, nw) * _silu(gt)).astype(o_ref.dtype)


def _gdn_sample(proj, small, cstate, state, layer, cw, par, nw):
    t = proj.shape[0]
    bt = GDN_SAMPLE_BT
    blk0 = layer * (t // bt)
    const = lambda i: (0, 0)
    return pl.pallas_call(
        _gdn_sample_body,
        grid=(t // bt,),
        in_specs=[pl.BlockSpec((bt, GDN_CH), lambda i: (i, OFF_QKV // GDN_CH)),
                  pl.BlockSpec((bt, GDN_V), lambda i: (i, OFF_GATE // GDN_V)),
                  pl.BlockSpec((bt, SMALL_COLS), lambda i: (i, 0)),
                  pl.BlockSpec((bt, (CONV_W - 1) * GDN_CH), lambda i: (blk0 + i, 0)),
                  pl.BlockSpec((bt, GDN_HEADS, GDN_DK, GDN_DV), lambda i: (blk0 + i, 0, 0, 0)),
                  pl.BlockSpec((CONV_W, GDN_CH), const),
                  pl.BlockSpec((SUBLANES, SMALL_COLS), const),
                  pl.BlockSpec((1, GDN_DV), const)],
        out_specs=[pl.BlockSpec((bt, GDN_V), lambda i: (i, 0)),
                   pl.BlockSpec((bt, GDN_HEADS, GDN_DK, GDN_DV), lambda i: (i, 0, 0, 0))],
        out_shape=[jax.ShapeDtypeStruct((t, GDN_V), BF16),
                   jax.ShapeDtypeStruct((t,) + state.shape[1:], F32)],
        compiler_params=_params(1),
        name="gdn_sample",
    )(proj, proj, small, cstate, state, cw, par, nw)


def _ssd_sample_body(xs_ref, b_ref, c_ref, z_ref, sm_ref, cst_ref, s_ref, cw_ref, cb_ref, par_ref, d_ref, nw_ref,
                     y_ref, sout_ref):
    bt = xs_ref.shape[0]
    sm = sm_ref[...]
    dt_all = _softplus(sm + par_ref[0:1, :])
    e_all = jnp.exp(dt_all * (-jnp.exp(par_ref[1:2, :])))
    rowid_n = lax.broadcasted_iota(jnp.int32, (bt, SSM_N), 0)
    rowid_g = lax.broadcasted_iota(jnp.int32, (bt, SSM_GW), 0)

    def conv(x_ref, lo, width, woff):
        wl = woff + lo
        acc = cw_ref[CONV_W - 1:CONV_W, wl:wl + width] * x_ref[:, lo:lo + width] + cb_ref[0:1, wl:wl + width]
        for j in range(CONV_W - 1):
            acc = acc + cw_ref[j:j + 1, wl:wl + width] * cst_ref[:, j * SSM_CH + wl:j * SSM_CH + wl + width]
        return _silu(acc)

    for g in range(SSM_GROUPS):
        bc = conv(b_ref, g * SSM_N, SSM_N, SSM_DINNER)
        cc = conv(c_ref, g * SSM_N, SSM_N, SSM_DINNER + SSM_BC)
        xs_g = conv(xs_ref, g * SSM_GW, SSM_GW, 0)
        cb = jnp.sum(cc * bc, axis=-1, keepdims=True)
        yoff_rows = []
        for t in range(bt):
            c_t = jnp.where(rowid_n == 0, cc[t:t + 1, :], 0.0)
            s_tg = jnp.concatenate([s_ref[t, SSM_R * g + rr] for rr in range(SSM_R)], axis=0)
            yoff_rows.append(_dot_nt(c_t, s_tg)[0:1])
        yoff = jnp.concatenate(yoff_rows, axis=0)
        lanes = [LANE_DT + SSM_R * g + rr for rr in range(SSM_R)]
        dt_g = jnp.concatenate([jnp.broadcast_to(dt_all[:, l:l + 1], (bt, SSM_P)) for l in lanes], axis=1)
        e_g = jnp.concatenate([jnp.broadcast_to(e_all[:, l:l + 1], (bt, SSM_P)) for l in lanes], axis=1)
        xdt = xs_g * dt_g
        y = cb * xdt + yoff * e_g + d_ref[0:1, g * SSM_GW:(g + 1) * SSM_GW] * xs_g
        for t in range(bt):
            outer = _dot_tn(jnp.where(rowid_g == t, xdt, 0.0), bc)
            for rr in range(SSM_R):
                h = SSM_R * g + rr
                sout_ref[t, h] = (s_ref[t, h] * e_all[t:t + 1, lanes[rr]:lanes[rr] + 1]
                                  + outer[rr * SSM_P:(rr + 1) * SSM_P, :])
        y = y * _silu(z_ref[:, g * SSM_GW:(g + 1) * SSM_GW])
        y_ref[:, g * SSM_GW:(g + 1) * SSM_GW] = _rmsnorm(
            y, nw_ref[0:1, g * SSM_GW:(g + 1) * SSM_GW]).astype(y_ref.dtype)


def _ssd_sample(proj, small, cstate, state, layer, cw, cb, par, d_row, nw):
    t = proj.shape[0]
    bt = SAMPLE_BT
    blk0 = layer * (t // bt)
    const = lambda i: (0, 0)
    return pl.pallas_call(
        _ssd_sample_body,
        grid=(t // bt,),
        in_specs=[pl.BlockSpec((bt, SSM_DINNER), lambda i: (i, OFF_XS // SSM_DINNER)),
                  pl.BlockSpec((bt, SSM_BC), lambda i: (i, OFF_B // SSM_BC)),
                  pl.BlockSpec((bt, SSM_BC), lambda i: (i, OFF_C // SSM_BC)),
                  pl.BlockSpec((bt, SSM_DINNER), lambda i: (i, OFF_Z // SSM_DINNER)),
                  pl.BlockSpec((bt, SMALL_COLS), lambda i: (i, 0)),
                  pl.BlockSpec((bt, (CONV_W - 1) * SSM_CH), lambda i: (blk0 + i, 0)),
                  pl.BlockSpec((bt, SSM_HEADS, SSM_P, SSM_N), lambda i: (blk0 + i, 0, 0, 0)),
                  pl.BlockSpec((CONV_W, SSM_CH), const),
                  pl.BlockSpec((1, SSM_CH), const),
                  pl.BlockSpec((SUBLANES, SMALL_COLS), const),
                  pl.BlockSpec((1, SSM_DINNER), const),
                  pl.BlockSpec((1, SSM_DINNER), const)],
        out_specs=[pl.BlockSpec((bt, SSM_DINNER), lambda i: (i, 0)),
                   pl.BlockSpec((bt, SSM_HEADS, SSM_P, SSM_N), lambda i: (i, 0, 0, 0))],
        out_shape=[jax.ShapeDtypeStruct((t, SSM_DINNER), BF16),
                   jax.ShapeDtypeStruct((t,) + state.shape[1:], F32)],
        compiler_params=_params(1),
        name="ssd_sample",
    )(proj, proj, proj, proj, small, cstate, state, cw, cb, par, d_row, nw)


def _merge_body(oa_ref, yb_ref, ga_ref, gb_ref, x_ref, wg_ref, ws_ref, wo_ref, o_ref):
    a = jnp.dot(oa_ref[...], wg_ref[...], preferred_element_type=F32)
    b = jnp.dot(yb_ref[...], ws_ref[...], preferred_element_type=F32)
    merged = _sigmoid(ga_ref[...].astype(F32)) * a + _sigmoid(gb_ref[...].astype(F32)) * b
    o_ref[...] = x_ref[...] + _dot(merged, wo_ref[...])


def _merge(o_a, y_b, proj, x, wg, ws, wo, *, tm):
    t = x.shape[0]
    tm = min(tm, t)
    assert t % tm == 0
    const = lambda i: (0, 0)
    return pl.pallas_call(
        _merge_body,
        grid=(t // tm,),
        in_specs=[pl.BlockSpec((tm, GDN_V), lambda i: (i, 0)),
                  pl.BlockSpec((tm, SSM_DINNER), lambda i: (i, 0)),
                  pl.BlockSpec((tm, D_MODEL), lambda i: (i, OFF_GA // D_MODEL)),
                  pl.BlockSpec((tm, D_MODEL), lambda i: (i, OFF_GB // D_MODEL)),
                  pl.BlockSpec((tm, D_MODEL), lambda i: (i, 0)),
                  pl.BlockSpec((GDN_V, D_MODEL), const),
                  pl.BlockSpec((SSM_DINNER, D_MODEL), const),
                  pl.BlockSpec((D_MODEL, D_MODEL), const)],
        out_specs=pl.BlockSpec((tm, D_MODEL), lambda i: (i, 0)),
        out_shape=jax.ShapeDtypeStruct((t, D_MODEL), F32),
        compiler_params=_params(1),
        name="merge_out",
    )(o_a, y_b, proj, proj, x, wg, ws, wo)


def _softmax_rows(s):
    e = jnp.exp(s - jnp.max(s, axis=-1, keepdims=True))
    return e / jnp.sum(e, axis=-1, keepdims=True)


def _xattn_prompt_body(x_ref, gx_ref, wq_ref, wo_ref, k_ref, v_ref, o_ref):
    x = x_ref[...]
    q = _dot(_rmsnorm(x, gx_ref[...]), wq_ref[...])
    outs = []
    for h in range(X_HEADS):
        sl = slice(h * X_HD, (h + 1) * X_HD)
        p = _softmax_rows(_dot_nt(q[:, sl], k_ref[:, sl]) * (X_HD ** -0.5))
        outs.append(_dot(p, v_ref[:, sl]))
    o_ref[...] = x + _dot(jnp.concatenate(outs, axis=1), wo_ref[...])


def _xattn_prompt(x, gx, wq, wo, mk, mv, seq, *, tm):
    t = x.shape[0]
    n_mem = mk.shape[0] // (t // seq)
    tm = min(tm, seq)
    assert seq % tm == 0
    per_seq = seq // tm
    const = lambda i: (0, 0)
    return pl.pallas_call(
        _xattn_prompt_body,
        grid=(t // tm,),
        in_specs=[pl.BlockSpec((tm, D_MODEL), lambda i: (i, 0)),
                  pl.BlockSpec((1, D_MODEL), const),
                  pl.BlockSpec((D_MODEL, D_MODEL), const),
                  pl.BlockSpec((D_MODEL, D_MODEL), const),
                  pl.BlockSpec((n_mem, D_MODEL), lambda i: (i // per_seq, 0)),
                  pl.BlockSpec((n_mem, D_MODEL), lambda i: (i // per_seq, 0))],
        out_specs=pl.BlockSpec((tm, D_MODEL), lambda i: (i, 0)),
        out_shape=jax.ShapeDtypeStruct((t, D_MODEL), F32),
        compiler_params=_params(1),
        name="xattn_prompt",
    )(x, gx.reshape(1, D_MODEL), wq, wo, mk, mv)


def _xattn_sample_body(x_ref, gx_ref, wq_ref, wo_ref, k_ref, v_ref, o_ref, q_scr, a_scr):
    i = pl.program_id(0)
    bt = k_ref.shape[0]

    @pl.when(i == 0)
    def _():
        q_scr[...] = _dot(_rmsnorm(x_ref[...], gx_ref[...]), wq_ref[...])

    assert X_ROWS == SUBLANES and X_HALVES == 2
    nrow = k_ref.shape[1]
    sub = lax.broadcasted_iota(jnp.int32, (X_ROWS, nrow), 0)
    lane = lax.broadcasted_iota(jnp.int32, (X_ROWS, nrow), 1)
    own = (lane % X_ROWS) == sub
    lower = sub < X_HEADS
    toks = range(bt)
    q_rows = []
    for tt in toks:
        q_t = q_scr[pl.ds(i * bt + tt, 1), :]
        q_rows.append(jnp.concatenate(
            [q_t[:, h * X_HD + half * LANES:h * X_HD + (half + 1) * LANES]
             for half in range(X_HALVES) for h in range(X_HEADS)], axis=0))
    es = [jnp.where(own, _dot_nt(q_rows[tt], k_ref[tt]), 0.0) for tt in toks]
    p2s = []
    for e in es:
        s = (e + pltpu.roll(pltpu.roll(e, X_HEADS, 0), nrow - X_HEADS, 1)) * (X_HD ** -0.5)
        s = jnp.where(own, s, -jnp.inf)
        p = jnp.exp(s - jnp.max(s, axis=-1, keepdims=True))
        p = jnp.where(lower, p / jnp.sum(p, axis=-1, keepdims=True), 0.0)
        p2s.append(p + pltpu.roll(pltpu.roll(p, X_HEADS, 0), X_HEADS, 1))
    o2s = [_dot(p2s[tt], v_ref[tt]) for tt in toks]
    for tt in toks:
        a_scr[pl.ds(i * bt + tt, 1), :] = jnp.concatenate(
            [o2s[tt][half * X_HEADS + h:half * X_HEADS + h + 1, :]
             for h in range(X_HEADS) for half in range(X_HALVES)], axis=1)

    @pl.when(i == pl.num_programs(0) - 1)
    def _():
        o_ref[...] = x_ref[...] + _dot(a_scr[...], wo_ref[...])


def _xattn_sample(x, gx, wq, wo, ck, cv, layer):
    t = x.shape[0]
    n_mem = ck.shape[1]
    bt = XATTN_BT
    blk0 = layer * (t // bt)
    const = lambda i: (0, 0)
    return pl.pallas_call(
        _xattn_sample_body,
        grid=(t // bt,),
        in_specs=[pl.BlockSpec((t, D_MODEL), const),
                  pl.BlockSpec((1, D_MODEL), const),
                  pl.BlockSpec((D_MODEL, D_MODEL), const),
                  pl.BlockSpec((D_MODEL, D_MODEL), const),
                  pl.BlockSpec((bt,) + ck.shape[1:], lambda i: (blk0 + i, 0, 0)),
                  pl.BlockSpec((bt,) + cv.shape[1:], lambda i: (blk0 + i, 0, 0))],
        out_specs=pl.BlockSpec((t, D_MODEL), const),
        out_shape=jax.ShapeDtypeStruct((t, D_MODEL), F32),
        scratch_shapes=[pltpu.VMEM((t, D_MODEL), F32), pltpu.VMEM((t, D_MODEL), F32)],
        compiler_params=_params(1),
        name="xattn_sample",
    )(x, gx.reshape(1, D_MODEL), wq, wo, ck, cv)


def _mlp_body(x_ref, g_ref, wu_ref, wd_ref, gf_ref, y_ref, hn_ref, acc_ref, *, final_norm):
    j = pl.program_id(1)

    @pl.when(j == 0)
    def _():
        hn_ref[...] = _rmsnorm(x_ref[...], g_ref[...]).astype(BF16)
        acc_ref[...] = jnp.zeros_like(acc_ref)

    hf = jnp.dot(hn_ref[...], wu_ref[...], preferred_element_type=F32)
    acc_ref[...] += _dot(jnp.square(jnp.maximum(hf, 0.0)), wd_ref[...])

    @pl.when(j == pl.num_programs(1) - 1)
    def _():
        x_new = x_ref[...] + acc_ref[...]
        y_ref[...] = _rmsnorm(x_new, gf_ref[...]) if final_norm else x_new


def _mlp(x, g, wu, wd, gf, *, tm, tf, final_norm):
    t = x.shape[0]
    tm = min(tm, t)
    assert t % tm == 0 and D_FF % tf == 0
    const = lambda i, j: (0, 0)
    return pl.pallas_call(
        functools.partial(_mlp_body, final_norm=final_norm),
        grid=(t // tm, D_FF // tf),
        in_specs=[pl.BlockSpec((tm, D_MODEL), lambda i, j: (i, 0)),
                  pl.BlockSpec((1, D_MODEL), const),
                  pl.BlockSpec((D_MODEL, tf), lambda i, j: (0, j)),
                  pl.BlockSpec((tf, D_MODEL), lambda i, j: (j, 0)),
                  pl.BlockSpec((1, D_MODEL), const)],
        out_specs=pl.BlockSpec((tm, D_MODEL), lambda i, j: (i, 0)),
        out_shape=jax.ShapeDtypeStruct((t, D_MODEL), F32),
        scratch_shapes=[pltpu.VMEM((tm, D_MODEL), BF16), pltpu.VMEM((tm, D_MODEL), F32)],
        compiler_params=_params(2),
        name="mlp",
    )(x, g.reshape(1, D_MODEL), wu, wd, gf.reshape(1, D_MODEL))


def _lane_row(vec, lane0):
    return jnp.zeros((SMALL_COLS,), F32).at[lane0:lane0 + vec.shape[0]].set(vec.astype(F32))


def _param_rows(*rows):
    out = jnp.zeros((SUBLANES, SMALL_COLS), F32)
    for i, r in enumerate(rows):
        out = out.at[i].set(r)
    return out


def kernel(x_prompt, x_sample, mem_prompt, state_gdn_conv, state_gdn, state_ssm_conv, state_ssm, cache_mem_k, cache_mem_v, g_mix, w_in, gdn_conv_w, gdn_A_log, gdn_dt_bias, gdn_norm_w, w_gdn_up, ssm_conv_w, ssm_conv_b, ssm_dt_bias, ssm_A_log, ssm_D, ssm_norm_w, w_ssm_up, w_out, g_mem, w_mk, w_mv, g_x, w_cq, w_co, g_ff, w_ff_up, w_ff_down, g_final):
    bp, seq, _ = x_prompt.shape
    bs, dec_seq, _ = x_sample.shape
    depth = w_in.shape[0]
    n_mem = mem_prompt.shape[1]
    assert seq % STEP == 0 and dec_seq == 1
    assert bs % SAMPLE_BT == 0 and bs % GDN_SAMPLE_BT == 0 and bs % XATTN_BT == 0
    tp = bp * seq

    xp = x_prompt.reshape(tp, D_MODEL)
    xs = x_sample.reshape(bs, D_MODEL)
    mem = mem_prompt.reshape(bp * n_mem, D_MODEL)
    gst_all = state_gdn_conv.astype(F32).reshape(depth * bs, (CONV_W - 1) * GDN_CH)
    sst_all = state_ssm_conv.astype(F32).reshape(depth * bs, (CONV_W - 1) * SSM_CH)
    sg_all = state_gdn.astype(F32).reshape(depth * bs, GDN_HEADS, GDN_DK, GDN_DV)
    ss_all = state_ssm.astype(F32).reshape(depth * bs, SSM_HEADS, SSM_P, SSM_N)
    def cache_rows(c):
        c = c.astype(F32).reshape(depth * bs, n_mem, X_HEADS, X_HALVES, LANES)
        return c.transpose(0, 1, 3, 2, 4).reshape(depth * bs, n_mem * X_ROWS, LANES)

    ck_all = cache_rows(cache_mem_k)
    cv_all = cache_rows(cache_mem_v)
    outs ={k: [] for k in ("p_gc", "p_g", "p_sc", "p_s", "p_mk", "p_mv", "s_gc", "s_g", "s_sc", "s_s")}

    starts = [0]
    for width in IN_SPLITS:
        starts.append(starts[-1] + width)
    col = lambda i: slice(starts[i], starts[i + 1])

    for l in range(depth):
        wl = w_in[l]
        w_main = jnp.concatenate([wl[:, starts[0]:starts[2]], wl[:, starts[4]:starts[6]],
                                  wl[:, starts[7]:starts[9]]], axis=1).astype(BF16)
        w_small = jnp.concatenate(
            [wl[:, col(2)], wl[:, col(3)], wl[:, col(6)],
             jnp.zeros((D_MODEL, SMALL_COLS - 2 * GDN_HEADS - SSM_HEADS), F32)], axis=1).astype(BF16)
        gdn_par = _param_rows(_lane_row(gdn_A_log[l], LANE_GDEC), _lane_row(gdn_dt_bias[l], LANE_GDEC))
        ssm_par = _param_rows(_lane_row(ssm_dt_bias[l], LANE_DT), _lane_row(ssm_A_log[l], LANE_DT))
        gdn_nw = gdn_norm_w[l].reshape(1, GDN_DV).astype(F32)
        ssm_nw = ssm_norm_w[l].reshape(1, SSM_DINNER).astype(F32)
        d_row = jnp.repeat(ssm_D[l].astype(F32), SSM_P).reshape(1, SSM_DINNER)
        gcw = gdn_conv_w[l].astype(F32)
        scw = ssm_conv_w[l].astype(F32)
        scb = ssm_conv_b[l].reshape(1, SSM_CH).astype(F32)
        wg, ws, wo = w_gdn_up[l].astype(BF16), w_ssm_up[l].astype(BF16), w_out[l].astype(BF16)
        wq, wco = w_cq[l].astype(BF16), w_co[l].astype(BF16)
        wu, wd = w_ff_up[l].astype(BF16), w_ff_down[l].astype(BF16)
        last = l == depth - 1

        pt, st = PROMPT_TILES, SAMPLE_TILES
        mk = _rms_mm(mem, g_mem[l], w_mk[l].astype(BF16), tm=pt["kv"][0], tn=pt["kv"][1])
        mv = _rms_mm(mem, g_mem[l], w_mv[l].astype(BF16), tm=pt["kv"][0], tn=pt["kv"][1])
        outs["p_mk"].append(mk.reshape(bp, n_mem, X_HEADS, X_HD))
        outs["p_mv"].append(mv.reshape(bp, n_mem, X_HEADS, X_HD))

        proj, small = _in_proj(xp, g_mix[l], w_main, w_small, tm=pt["in_proj"][0], tn=pt["in_proj"][1],
                               out_dtype=BF16)
        o_a, p_g = _gdn_prompt(proj, small, gcw, gdn_par, gdn_nw, bp, seq)
        y_b, p_s = _ssd_prompt(proj, small, scw, scb, ssm_par, d_row, ssm_nw, bp, seq)
        proj3 = proj.reshape(bp, seq, MAIN_COLS)
        outs["p_gc"].append(proj3[:, seq - (CONV_W - 1):, OFF_QKV:OFF_QKV + GDN_CH].astype(F32))
        outs["p_sc"].append(proj3[:, seq - (CONV_W - 1):, OFF_XS:OFF_XS + SSM_CH].astype(F32))
        outs["p_g"].append(p_g)
        outs["p_s"].append(p_s)
        xp = _merge(o_a, y_b, proj, xp, wg, ws, wo, tm=pt["merge"])
        xp = _xattn_prompt(xp, g_x[l], wq, wco, mk, mv, seq, tm=pt["xattn"])
        xp = _mlp(xp, g_ff[l], wu, wd, g_final, tm=pt["mlp"][0], tf=pt["mlp"][1], final_norm=last)

        proj, small = _in_proj(xs, g_mix[l], w_main, w_small, tm=st["in_proj"][0], tn=st["in_proj"][1],
                               out_dtype=F32)
        o_a, s_g = _gdn_sample(proj, small, gst_all, sg_all, l, gcw, gdn_par, gdn_nw)
        y_b, s_s = _ssd_sample(proj, small, sst_all, ss_all, l, scw, scb, ssm_par, d_row, ssm_nw)
        outs["s_gc"].append(jnp.concatenate(
            [state_gdn_conv[l, :, 1:].astype(F32), proj[:, None, OFF_QKV:OFF_QKV + GDN_CH]], axis=1))
        outs["s_sc"].append(jnp.concatenate(
            [state_ssm_conv[l, :, 1:].astype(F32), proj[:, None, OFF_XS:OFF_XS + SSM_CH]], axis=1))
        outs["s_g"].append(s_g)
        outs["s_s"].append(s_s)
        xs = _merge(o_a, y_b, proj, xs, wg, ws, wo, tm=st["merge"])
        xs = _xattn_sample(xs, g_x[l], wq, wco, ck_all, cv_all, l)
        xs = _mlp(xs, g_ff[l], wu, wd, g_final, tm=st["mlp"][0], tf=st["mlp"][1], final_norm=last)

    stack = lambda k: jnp.stack(outs[k])
    return (xp.reshape(bp, seq, D_MODEL), xs.reshape(bs, dec_seq, D_MODEL),
            stack("p_gc"), stack("p_g"), stack("p_sc"), stack("p_s"), stack("p_mk"), stack("p_mv"),
            stack("s_gc"), stack("s_g"), stack("s_sc"), stack("s_s"))
```

```python
import functools

import jax
import jax.numpy as jnp
from jax import lax
from jax.experimental import pallas as pl
from jax.experimental.pallas import tpu as pltpu

F32 = jnp.float32
BF16 = jnp.bfloat16

EPS = 1e-6
CHUNK = 64
CONV_W = 4
D_MODEL = 1024
GDN_HEADS = 8
GDN_DK = 128
GDN_DV = 128
GDN_QK = GDN_HEADS * GDN_DK
GDN_V = GDN_HEADS * GDN_DV
GDN_CH = 2 * GDN_QK + GDN_V
SSM_DINNER = 2 * D_MODEL
SSM_P = 64
SSM_HEADS = SSM_DINNER // SSM_P
SSM_GROUPS = 8
SSM_R = SSM_HEADS // SSM_GROUPS
SSM_N = 128
SSM_BC = SSM_GROUPS * SSM_N
SSM_CH = SSM_DINNER + 2 * SSM_BC
SSM_GW = SSM_R * SSM_P
X_HEADS = 4
X_HD = D_MODEL // X_HEADS
D_FF = 4 * D_MODEL
IN_SPLITS = (GDN_CH, GDN_V, GDN_HEADS, GDN_HEADS, SSM_DINNER, SSM_CH, SSM_HEADS, D_MODEL, D_MODEL)

MAIN_COLS = GDN_CH + GDN_V + SSM_DINNER + SSM_CH + 2 * D_MODEL
OFF_QKV, OFF_GATE, OFF_Z = 0, GDN_CH, GDN_CH + GDN_V
OFF_XS = OFF_Z + SSM_DINNER
OFF_B = OFF_XS + SSM_DINNER
OFF_C = OFF_B + SSM_BC
OFF_GA = OFF_C + SSM_BC
OFF_GB = OFF_GA + D_MODEL
LANES = 128
SUBLANES = 8
X_HALVES = X_HD // LANES
X_ROWS = X_HEADS * X_HALVES
SMALL_COLS = LANES
LANE_BETA, LANE_GDEC, LANE_DT = 0, GDN_HEADS, 2 * GDN_HEADS

STEP = 8 * CHUNK
GDN_SET = 2
TAIL = SUBLANES
SAMPLE_BT = 8
GDN_SAMPLE_BT = 16
XATTN_BT = 8
VMEM_LIMIT = 54 * 1024 * 1024
PROMPT_TILES = dict(kv=(1024, 1024), in_proj=(2048, 1536), merge=512, xattn=1024, mlp=(1024, 1024))
SAMPLE_TILES = dict(in_proj=(128, 2048), merge=128, mlp=(128, 2048))


def _params(n_axes):
    return pltpu.CompilerParams(dimension_semantics=("arbitrary",) * n_axes, vmem_limit_bytes=VMEM_LIMIT)


def _sigmoid(x):
    return 0.5 * jnp.tanh(0.5 * x) + 0.5


def _silu(x):
    h = 0.5 * x
    return h + h * jnp.tanh(h)


def _softplus(x):
    return jnp.maximum(x, 0.0) + jnp.log1p(jnp.exp(-jnp.abs(x)))


def _rmsnorm(x, g):
    xf = x.astype(F32)
    return xf * lax.rsqrt(jnp.mean(xf * xf, axis=-1, keepdims=True) + EPS) * g


def _dot(a, b):
    return jnp.dot(a.astype(BF16), b.astype(BF16), preferred_element_type=F32)


def _dot_nt(a, b):
    return lax.dot_general(a.astype(BF16), b.astype(BF16), (((1,), (1,)), ((), ())), preferred_element_type=F32)


def _dot_tn(a, b):
    return lax.dot_general(a.astype(BF16), b.astype(BF16), (((0,), (0,)), ((), ())), preferred_element_type=F32)


def _chunk_cumsum(x):
    n = x.shape[0]
    r = lax.broadcasted_iota(jnp.int32, (n, n), 0)
    c = lax.broadcasted_iota(jnp.int32, (n, n), 1)
    tri = jnp.where((r >= c) & ((r // CHUNK) == (c // CHUNK)), 1.0, 0.0).astype(BF16)
    h1 = x.astype(BF16)
    r1 = x - h1.astype(F32)
    h2 = r1.astype(BF16)
    h3 = (r1 - h2.astype(F32)).astype(BF16)
    d = functools.partial(jnp.dot, preferred_element_type=F32)
    return d(tri, h1) + (d(tri, h2) + d(tri, h3))


def _chunk_masks():
    r = lax.broadcasted_iota(jnp.int32, (CHUNK, CHUNK), 0)
    c = lax.broadcasted_iota(jnp.int32, (CHUNK, CHUNK), 1)
    return r >= c, r > c, jnp.where(r == c, 1.0, 0.0).astype(F32)


def _decay_matrix(col, row, tril):
    return jnp.where(tril, jnp.exp(jnp.where(tril, col - row, 0.0)), 0.0)


INV_BASE = 8


def _mm_shared(lhs_list, b):
    m = lhs_list[0].shape[0]
    out = _dot(jnp.concatenate(lhs_list, axis=0), b)
    return [out[i * m:(i + 1) * m] for i in range(len(lhs_list))]


def _inv_unit_lower_levels(a_list, eye, out):
    size = a_list[0].shape[0]
    r = lax.broadcasted_iota(jnp.int32, (size, size), 0)
    c = lax.broadcasted_iota(jnp.int32, (size, size), 1)
    same = (r // INV_BASE) == (c // INV_BASE)
    n = [jnp.where(same, -a, 0.0) for a in a_list]
    t = [eye + ni for ni in n]
    p = [_dot(ni, ni) for ni in n]
    yield
    terms = 2
    while terms < INV_BASE:
        terms *= 2
        if terms < INV_BASE:
            res = [_mm_shared([pi, ti], pi) for pi, ti in zip(p, t)]
            t = [ti + ri[1] for ti, ri in zip(t, res)]
            p = [ri[0] for ri in res]
        else:
            t = [ti + _dot(ti, pi) for pi, ti in zip(p, t)]
        yield
    b = INV_BASE
    while b < size:
        low = ((r // (2 * b)) == (c // (2 * b))) & ((r // b) % 2 == 1) & ((c // b) % 2 == 0)
        y = [_dot(jnp.where(low, a, 0.0), ti) for a, ti in zip(a_list, t)]
        yield
        t = [ti - _dot(ti, yi) for ti, yi in zip(t, y)]
        yield
        b *= 2
    out.extend(t)


def _interleave(gen, thunks, gen_steps):
    thunks = list(thunks)
    per_step = -(-len(thunks) // gen_steps)
    for _ in gen:
        for th in thunks[:per_step]:
            th()
        thunks = thunks[per_step:]
    for th in thunks:
        th()


def _rms_mm_body(x_ref, g_ref, w_ref, o_ref, hn_ref):
    @pl.when(pl.program_id(1) == 0)
    def _():
        hn_ref[...] = _rmsnorm(x_ref[...], g_ref[...]).astype(BF16)

    o_ref[...] = jnp.dot(hn_ref[...], w_ref[...], preferred_element_type=F32).astype(o_ref.dtype)


def _rms_mm(x, g, w, *, tm, tn, out_dtype=F32):
    t, k = x.shape
    n = w.shape[1]
    tm, tn = min(tm, t), min(tn, n)
    assert t % tm == 0 and n % tn == 0
    return pl.pallas_call(
        _rms_mm_body,
        grid=(t // tm, n // tn),
        in_specs=[pl.BlockSpec((tm, k), lambda i, j: (i, 0)),
                  pl.BlockSpec((1, k), lambda i, j: (0, 0)),
                  pl.BlockSpec((k, tn), lambda i, j: (0, j))],
        out_specs=pl.BlockSpec((tm, tn), lambda i, j: (i, j)),
        out_shape=jax.ShapeDtypeStruct((t, n), out_dtype),
        scratch_shapes=[pltpu.VMEM((tm, k), BF16)],
        compiler_params=_params(2),
        name="rms_matmul",
    )(x, g.reshape(1, k), w)


def _in_proj_body(x_ref, g_ref, w_ref, ws_ref, o_ref, os_ref, hn_ref):
    @pl.when(pl.program_id(1) == 0)
    def _():
        hn_ref[...] = _rmsnorm(x_ref[...], g_ref[...]).astype(BF16)
        os_ref[...] = jnp.dot(hn_ref[...], ws_ref[...], preferred_element_type=F32)

    o_ref[...] = jnp.dot(hn_ref[...], w_ref[...], preferred_element_type=F32).astype(o_ref.dtype)


def _in_proj(x, g, w_main, w_small, *, tm, tn, out_dtype):
    t, k = x.shape
    n, ns = w_main.shape[1], w_small.shape[1]
    tm = min(tm, t)
    assert t % tm == 0 and n % tn == 0
    return pl.pallas_call(
        _in_proj_body,
        grid=(t // tm, n // tn),
        in_specs=[pl.BlockSpec((tm, k), lambda i, j: (i, 0)),
                  pl.BlockSpec((1, k), lambda i, j: (0, 0)),
                  pl.BlockSpec((k, tn), lambda i, j: (0, j)),
                  pl.BlockSpec((k, ns), lambda i, j: (0, 0))],
        out_specs=[pl.BlockSpec((tm, tn), lambda i, j: (i, j)),
                   pl.BlockSpec((tm, ns), lambda i, j: (i, 0))],
        out_shape=[jax.ShapeDtypeStruct((t, n), out_dtype),
                   jax.ShapeDtypeStruct((t, ns), F32)],
        scratch_shapes=[pltpu.VMEM((tm, k), BF16)],
        compiler_params=_params(2),
        name="in_proj",
    )(x, g.reshape(1, k), w_main, w_small)


def _gdn_prompt_body(qkv_ref, gate_ref, sm_ref, cw_ref, par_ref, nw_ref, o_ref, sfin_ref, xbuf, s_ref):
    step = pl.program_id(1)

    @pl.when(step == 0)
    def _():
        xbuf[0:TAIL, :] = jnp.zeros((TAIL, GDN_CH), F32)
        s_ref[...] = jnp.zeros_like(s_ref)

    xbuf[TAIL:TAIL + STEP, :] = qkv_ref[...].astype(F32)
    sm = sm_ref[...]
    beta_all = _sigmoid(sm)
    g_all = -jnp.exp(par_ref[0:1, :]) * _softplus(sm + par_ref[1:2, :])
    gc_all = _chunk_cumsum(g_all)
    gc_t = gc_all.T
    tril, strict, eye = _chunk_masks()
    nw = nw_ref[...]

    def conv(blk, r0):
        ln = slice(blk * LANES, (blk + 1) * LANES)
        acc = cw_ref[CONV_W - 1:CONV_W, ln] * xbuf[pl.ds(TAIL + r0, CHUNK), ln]
        for j in range(CONV_W - 1):
            acc = acc + cw_ref[j:j + 1, ln] * xbuf[pl.ds(TAIL - (CONV_W - 1) + j + r0, CHUNK), ln]
        return _silu(acc)

    def pre_item(c, h):
        r0 = c * CHUNK
        q = conv(h, r0)
        k = conv(GDN_HEADS + h, r0)
        v = conv(2 * GDN_HEADS + h, r0)
        q = q * lax.rsqrt(jnp.sum(q * q, axis=-1, keepdims=True) + EPS) * (GDN_DK ** -0.5)
        k = k * lax.rsqrt(jnp.sum(k * k, axis=-1, keepdims=True) + EPS)
        lg = LANE_GDEC + h
        bcol = beta_all[r0:r0 + CHUNK, LANE_BETA + h:LANE_BETA + h + 1]
        gcol = gc_all[r0:r0 + CHUNK, lg:lg + 1]
        grow = gc_t[lg:lg + 1, r0:r0 + CHUNK]
        glast = gc_all[r0 + CHUNK - 1:r0 + CHUNK, lg:lg + 1]
        decay = _decay_matrix(gcol, grow, tril)
        kb = k * bcol
        egc = jnp.exp(gcol)
        return dict(
            a=jnp.where(strict, _dot_nt(kb, k) * decay, 0.0),
            rhs=jnp.concatenate([v * bcol, kb * egc], axis=1),
            aqk=_dot_nt(q, k) * decay,
            qg=q * egc,
            kd=k * jnp.exp(glast - gcol),
            dch=jnp.exp(glast))

    heads = range(GDN_HEADS)
    states = [s_ref[h] for h in heads]

    def recurrence_thunks(chunks, pre_s, sols):
        box = {}

        def round_ws(i):
            box["ws"] = [_dot(jnp.concatenate([sols[i][h][:, GDN_DV:], pre_s[i][h]["qg"]], axis=0), states[h])
                         for h in heads]

        def round_out(i):
            box["vn"] = [sols[i][h][:, :GDN_DV] - box["ws"][h][:CHUNK] for h in heads]
            box["o", i] = [box["ws"][h][CHUNK:] + _dot(pre_s[i][h]["aqk"], box["vn"][h]) for h in heads]

        def round_state(i):
            for h in heads:
                states[h] = states[h] * pre_s[i][h]["dch"] + _dot_tn(pre_s[i][h]["kd"], box["vn"][h])

        def store(i, h):
            r0 = chunks[i] * CHUNK
            gt = gate_ref[r0:r0 + CHUNK, h * GDN_DV:(h + 1) * GDN_DV].astype(F32)
            o_ref[r0:r0 + CHUNK, h * GDN_DV:(h + 1) * GDN_DV] = (
                _rmsnorm(box["o", i][h], nw) * _silu(gt)).astype(o_ref.dtype)

        ths = []
        for i in range(len(chunks)):
            ths += [functools.partial(f, i) for f in (round_ws, round_out, round_state)]
        return ths + [functools.partial(store, i, h) for i in range(len(chunks)) for h in heads]

    sets = [list(range(s, s + GDN_SET)) for s in range(0, STEP // CHUNK, GDN_SET)]
    doublings, merges = INV_BASE.bit_length() - 2, (CHUNK // INV_BASE).bit_length() - 1
    inv_steps = 1 + doublings + 2 * merges
    pre = {0: [[pre_item(c, h) for h in heads] for c in sets[0]]}
    pending = []
    for s, chunks in enumerate(sets):
        ahead = []
        if s + 1 < len(sets):
            pre[s + 1] = [[] for _ in sets[s + 1]]
            ahead = [functools.partial(lambda ss, i, c, h: pre[ss][i].append(pre_item(c, h)), s + 1, i, c, h)
                     for i, c in enumerate(sets[s + 1]) for h in heads]
        mixed = [th for pair in zip(pending, ahead) for th in pair]
        mixed += pending[len(ahead):] + ahead[len(pending):]
        tinv = []
        _interleave(_inv_unit_lower_levels([p["a"] for pc in pre[s] for p in pc], eye, tinv), mixed, inv_steps)
        sols = [[_dot(tinv[i * GDN_HEADS + h], pre[s][i][h]["rhs"]) for h in heads] for i in range(len(chunks))]
        pending = recurrence_thunks(chunks, pre[s], sols)
    for th in pending:
        th()
    for h in heads:
        s_ref[h] = states[h]

    xbuf[0:TAIL, :] = xbuf[STEP:STEP + TAIL, :]

    @pl.when(step == pl.num_programs(1) - 1)
    def _():
        sfin_ref[0] = s_ref[...]


def _gdn_prompt(proj, small, cw, par, nw, bsz, seq):
    nsteps = seq // STEP
    row = lambda b, s: b * nsteps + s
    return pl.pallas_call(
        _gdn_prompt_body,
        grid=(bsz, nsteps),
        in_specs=[pl.BlockSpec((STEP, GDN_CH), lambda b, s: (row(b, s), OFF_QKV // GDN_CH)),
                  pl.BlockSpec((STEP, GDN_V), lambda b, s: (row(b, s), OFF_GATE // GDN_V)),
                  pl.BlockSpec((STEP, SMALL_COLS), lambda b, s: (row(b, s), 0)),
                  pl.BlockSpec((CONV_W, GDN_CH), lambda b, s: (0, 0)),
                  pl.BlockSpec((SUBLANES, SMALL_COLS), lambda b, s: (0, 0)),
                  pl.BlockSpec((1, GDN_DV), lambda b, s: (0, 0))],
        out_specs=[pl.BlockSpec((STEP, GDN_V), lambda b, s: (row(b, s), 0)),
                   pl.BlockSpec((1, GDN_HEADS, GDN_DK, GDN_DV), lambda b, s: (b, 0, 0, 0))],
        out_shape=[jax.ShapeDtypeStruct((bsz * seq, GDN_V), BF16),
                   jax.ShapeDtypeStruct((bsz, GDN_HEADS, GDN_DK, GDN_DV), F32)],
        scratch_shapes=[pltpu.VMEM((TAIL + STEP, GDN_CH), F32),
                        pltpu.VMEM((GDN_HEADS, GDN_DK, GDN_DV), F32)],
        compiler_params=_params(2),
        name="gdn_prompt",
    )(proj, proj, small, cw, par, nw)


def _ssd_prompt_body(xs_ref, b_ref, c_ref, z_ref, sm_ref, cw_ref, cb_ref, par_ref, d_ref, nw_ref,
                     y_ref, sfin_ref, xb_x, xb_b, xb_c, st_ref):
    step = pl.program_id(1)

    @pl.when(step == 0)
    def _():
        xb_x[0:TAIL, :] = jnp.zeros((TAIL, SSM_DINNER), F32)
        xb_b[0:TAIL, :] = jnp.zeros((TAIL, SSM_BC), F32)
        xb_c[0:TAIL, :] = jnp.zeros((TAIL, SSM_BC), F32)
        st_ref[...] = jnp.zeros_like(st_ref)

    xb_x[TAIL:TAIL + STEP, :] = xs_ref[...].astype(F32)
    xb_b[TAIL:TAIL + STEP, :] = b_ref[...].astype(F32)
    xb_c[TAIL:TAIL + STEP, :] = c_ref[...].astype(F32)

    def conv(buf, r0, lo, width, woff):
        wl = woff + lo
        acc = cw_ref[CONV_W - 1:CONV_W, wl:wl + width] * buf[pl.ds(TAIL + r0, CHUNK), lo:lo + width]
        acc = acc + cb_ref[0:1, wl:wl + width]
        for j in range(CONV_W - 1):
            acc = acc + cw_ref[j:j + 1, wl:wl + width] * buf[pl.ds(TAIL - (CONV_W - 1) + j + r0, CHUNK), lo:lo + width]
        return _silu(acc)

    sm = sm_ref[...]
    dt_all = _softplus(sm + par_ref[0:1, :])
    cs_all = _chunk_cumsum(dt_all * (-jnp.exp(par_ref[1:2, :])))
    cs_t = cs_all.T
    tril, _, _ = _chunk_masks()

    assert SSM_P == CHUNK
    pw = 2 * SSM_P
    npair = SSM_R // 2
    items = [(c, g) for c in range(STEP // CHUNK) for g in range(SSM_GROUPS)]
    pairs = [(c, g, pr) for c, g in items for pr in range(npair)]
    lane_id = lax.broadcasted_iota(jnp.int32, (CHUNK, pw), 1)
    first = lane_id < SSM_P
    first_row = first[0:1]
    r2 = lax.broadcasted_iota(jnp.int32, (pw, pw), 0)
    c2 = lax.broadcasted_iota(jnp.int32, (pw, pw), 1)
    same_head = (r2 < SSM_P) == (c2 < SSM_P)
    tril_pair = lax.broadcasted_iota(jnp.int32, (CHUNK, pw), 0) >= (lane_id % SSM_P)

    def pair_cols(src, c, g, pr):
        l0 = LANE_DT + SSM_R * g + 2 * pr
        r0 = c * CHUNK
        return jnp.where(first, src[r0:r0 + CHUNK, l0:l0 + 1], src[r0:r0 + CHUNK, l0 + 1:l0 + 2])

    def pair_row(row0, row1):
        return jnp.where(first_row, row0, row1)

    dts, cscs, csrs, csls, lms = [], [], [], [], []

    def gating(c, g, pr):
        l0 = LANE_DT + SSM_R * g + 2 * pr
        r0 = c * CHUNK
        seg = (r0 // pw) * pw
        t0 = cs_t[l0:l0 + 1, seg:seg + pw]
        t1 = cs_t[l0 + 1:l0 + 2, seg:seg + pw]
        if r0 == seg:
            csr = pair_row(t0, pltpu.roll(t1, CHUNK, 1))
        else:
            csr = pair_row(pltpu.roll(t0, CHUNK, 1), t1)
        last = r0 + CHUNK - 1
        csc = pair_cols(cs_all, c, g, pr)
        dts.append(pair_cols(dt_all, c, g, pr))
        cscs.append(csc)
        csrs.append(csr)
        csls.append(pair_row(cs_all[last:last + 1, l0:l0 + 1], cs_all[last:last + 1, l0 + 1:l0 + 2]))
        lms.append(_decay_matrix(csc, csr, tril_pair))

    for p in pairs:
        gating(*p)
    bcs = [conv(xb_b, c * CHUNK, g * SSM_N, SSM_N, SSM_DINNER) for c, g in items]
    ccs = [conv(xb_c, c * CHUNK, g * SSM_N, SSM_N, SSM_DINNER + SSM_BC) for c, g in items]
    xss = [conv(xb_x, c * CHUNK, g * SSM_GW + pr * pw, pw, 0) for c, g, pr in pairs]
    cbs = [_dot_nt(cc, jnp.concatenate([bc, bc], axis=0)) for cc, bc in zip(ccs, bcs)]
    xdts = [xs * dt for xs, dt in zip(xss, dts)]
    ms = [cbs[i // npair] * lm for i, lm in enumerate(lms)]
    bds = [jnp.where(same_head, jnp.concatenate([x, x], axis=0), 0.0) for x in xdts]
    ylocs = []
    for (c, g, pr), m, bd, xs in zip(pairs, ms, bds, xss):
        lo = g * SSM_GW + pr * pw
        ylocs.append(_dot(m, bd) + d_ref[0:1, lo:lo + pw] * xs)
    ecss = [jnp.exp(csc) for csc in cscs]
    xds = [xdt * jnp.exp(csl - csc) for xdt, csl, csc in zip(xdts, csls, cscs)]
    dchs = [jnp.exp(csl) for csl in csls]

    def group_cat(vals, i):
        return jnp.concatenate(vals[i * npair:(i + 1) * npair], axis=1)

    outs = []
    for i, (c, g) in enumerate(items):
        st_g = st_ref[g]
        outs.append(group_cat(ylocs, i) + _dot(ccs[i], st_g) * group_cat(ecss, i))
        st_ref[g] = st_g * group_cat(dchs, i) + _dot_tn(bcs[i], group_cat(xds, i))
    for (c, g), y_g in zip(items, outs):
        r0 = c * CHUNK
        y_g = y_g * _silu(z_ref[r0:r0 + CHUNK, g * SSM_GW:(g + 1) * SSM_GW].astype(F32))
        y_ref[r0:r0 + CHUNK, g * SSM_GW:(g + 1) * SSM_GW] = _rmsnorm(
            y_g, nw_ref[0:1, g * SSM_GW:(g + 1) * SSM_GW]).astype(y_ref.dtype)

    xb_x[0:TAIL, :] = xb_x[STEP:STEP + TAIL, :]
    xb_b[0:TAIL, :] = xb_b[STEP:STEP + TAIL, :]
    xb_c[0:TAIL, :] = xb_c[STEP:STEP + TAIL, :]

    @pl.when(step == pl.num_programs(1) - 1)
    def _():
        for g in range(SSM_GROUPS):
            st_t = st_ref[g].T
            for rr in range(SSM_R):
                sfin_ref[0, SSM_R * g + rr] = st_t[rr * SSM_P:(rr + 1) * SSM_P, :]


def _ssd_prompt(proj, small, cw, cb, par, d_row, nw, bsz, seq):
    nsteps = seq // STEP
    row = lambda b, s: b * nsteps + s
    const = lambda b, s: (0, 0)
    return pl.pallas_call(
        _ssd_prompt_body,
        grid=(bsz, nsteps),
        in_specs=[pl.BlockSpec((STEP, SSM_DINNER), lambda b, s: (row(b, s), OFF_XS // SSM_DINNER)),
                  pl.BlockSpec((STEP, SSM_BC), lambda b, s: (row(b, s), OFF_B // SSM_BC)),
                  pl.BlockSpec((STEP, SSM_BC), lambda b, s: (row(b, s), OFF_C // SSM_BC)),
                  pl.BlockSpec((STEP, SSM_DINNER), lambda b, s: (row(b, s), OFF_Z // SSM_DINNER)),
                  pl.BlockSpec((STEP, SMALL_COLS), lambda b, s: (row(b, s), 0)),
                  pl.BlockSpec((CONV_W, SSM_CH), const),
                  pl.BlockSpec((1, SSM_CH), const),
                  pl.BlockSpec((SUBLANES, SMALL_COLS), const),
                  pl.BlockSpec((1, SSM_DINNER), const),
                  pl.BlockSpec((1, SSM_DINNER), const)],
        out_specs=[pl.BlockSpec((STEP, SSM_DINNER), lambda b, s: (row(b, s), 0)),
                   pl.BlockSpec((1, SSM_HEADS, SSM_P, SSM_N), lambda b, s: (b, 0, 0, 0))],
        out_shape=[jax.ShapeDtypeStruct((bsz * seq, SSM_DINNER), BF16),
                   jax.ShapeDtypeStruct((bsz, SSM_HEADS, SSM_P, SSM_N), F32)],
        scratch_shapes=[pltpu.VMEM((TAIL + STEP, SSM_DINNER), F32),
                        pltpu.VMEM((TAIL + STEP, SSM_BC), F32),
                        pltpu.VMEM((TAIL + STEP, SSM_BC), F32),
                        pltpu.VMEM((SSM_GROUPS, SSM_N, SSM_GW), F32)],
        compiler_params=_params(2),
        name="ssd_prompt",
    )(proj, proj, proj, proj, small, cw, cb, par, d_row, nw)


def _gdn_sample_body(qkv_ref, gate_ref, sm_ref, cst_ref, s_ref, cw_ref, par_ref, nw_ref, o_ref, sout_ref):
    bt = qkv_ref.shape[0]
    sm = sm_ref[...]
    beta_all = _sigmoid(sm)
    eg_all = jnp.exp(-jnp.exp(par_ref[0:1, :]) * _softplus(sm + par_ref[1:2, :]))
    rowid = lax.broadcasted_iota(jnp.int32, (bt, GDN_DK), 0)
    nw = nw_ref[...]

    def conv(blk):
        lo = blk * LANES
        acc = cw_ref[CONV_W - 1:CONV_W, lo:lo + LANES] * qkv_ref[:, lo:lo + LANES]
        for j in range(CONV_W - 1):
            acc = acc + cw_ref[j:j + 1, lo:lo + LANES] * cst_ref[:, j * GDN_CH + lo:j * GDN_CH + lo + LANES]
        return _silu(acc)

    for h in range(GDN_HEADS):
        q = conv(h)
        k = conv(GDN_HEADS + h)
        v = conv(2 * GDN_HEADS + h)
        q = q * lax.rsqrt(jnp.sum(q * q, axis=-1, keepdims=True) + EPS) * (GDN_DK ** -0.5)
        k = k * lax.rsqrt(jnp.sum(k * k, axis=-1, keepdims=True) + EPS)
        bcol = beta_all[:, LANE_BETA + h:LANE_BETA + h + 1]
        egcol = eg_all[:, LANE_GDEC + h:LANE_GDEC + h + 1]
        qs_rows, ks_rows = [], []
        for t in range(bt):
            lhs = jnp.where(rowid == 0, q[t:t + 1, :], jnp.where(rowid == 1, k[t:t + 1, :], 0.0))
            r = _dot(lhs, s_ref[t, h])
            qs_rows.append(r[0:1])
            ks_rows.append(r[1:2])
        q_s = jnp.concatenate(qs_rows, axis=0)
        k_s = jnp.concatenate(ks_rows, axis=0)
        v_new = bcol * v - (bcol * egcol) * k_s
        o = egcol * q_s + jnp.sum(q * k, axis=-1, keepdims=True) * v_new
        for t in range(bt):
            k_t = jnp.where(rowid == t, k, 0.0)
            sout_ref[t, h] = s_ref[t, h] * egcol[t:t + 1, :] + _dot_tn(k_t, v_new)
        gt = gate_ref[:, h * GDN_DV:(h + 1) * GDN_DV]
        o_ref[:, h * GDN_DV:(h + 1) * GDN_DV] = (_rmsnorm(o, nw) * _silu(gt)).astype(o_ref.dtype)


def _gdn_sample(proj, small, cstate, state, layer, cw, par, nw):
    t = proj.shape[0]
    bt = GDN_SAMPLE_BT
    blk0 = layer * (t // bt)
    const = lambda i: (0, 0)
    return pl.pallas_call(
        _gdn_sample_body,
        grid=(t // bt,),
        in_specs=[pl.BlockSpec((bt, GDN_CH), lambda i: (i, OFF_QKV // GDN_CH)),
                  pl.BlockSpec((bt, GDN_V), lambda i: (i, OFF_GATE // GDN_V)),
                  pl.BlockSpec((bt, SMALL_COLS), lambda i: (i, 0)),
                  pl.BlockSpec((bt, (CONV_W - 1) * GDN_CH), lambda i: (blk0 + i, 0)),
                  pl.BlockSpec((bt, GDN_HEADS, GDN_DK, GDN_DV), lambda i: (blk0 + i, 0, 0, 0)),
                  pl.BlockSpec((CONV_W, GDN_CH), const),
                  pl.BlockSpec((SUBLANES, SMALL_COLS), const),
                  pl.BlockSpec((1, GDN_DV), const)],
        out_specs=[pl.BlockSpec((bt, GDN_V), lambda i: (i, 0)),
                   pl.BlockSpec((bt, GDN_HEADS, GDN_DK, GDN_DV), lambda i: (i, 0, 0, 0))],
        out_shape=[jax.ShapeDtypeStruct((t, GDN_V), BF16),
                   jax.ShapeDtypeStruct((t,) + state.shape[1:], F32)],
        compiler_params=_params(1),
        name="gdn_sample",
    )(proj, proj, small, cstate, state, cw, par, nw)


def _ssd_sample_body(xs_ref, b_ref, c_ref, z_ref, sm_ref, cst_ref, s_ref, cw_ref, cb_ref, par_ref, d_ref, nw_ref,
                     y_ref, sout_ref):
    bt = xs_ref.shape[0]
    sm = sm_ref[...]
    dt_all = _softplus(sm + par_ref[0:1, :])
    e_all = jnp.exp(dt_all * (-jnp.exp(par_ref[1:2, :])))
    rowid_n = lax.broadcasted_iota(jnp.int32, (bt, SSM_N), 0)
    rowid_g = lax.broadcasted_iota(jnp.int32, (bt, SSM_GW), 0)

    def conv(x_ref, lo, width, woff):
        wl = woff + lo
        acc = cw_ref[CONV_W - 1:CONV_W, wl:wl + width] * x_ref[:, lo:lo + width] + cb_ref[0:1, wl:wl + width]
        for j in range(CONV_W - 1):
            acc = acc + cw_ref[j:j + 1, wl:wl + width] * cst_ref[:, j * SSM_CH + wl:j * SSM_CH + wl + width]
        return _silu(acc)

    for g in range(SSM_GROUPS):
        bc = conv(b_ref, g * SSM_N, SSM_N, SSM_DINNER)
        cc = conv(c_ref, g * SSM_N, SSM_N, SSM_DINNER + SSM_BC)
        xs_g = conv(xs_ref, g * SSM_GW, SSM_GW, 0)
        cb = jnp.sum(cc * bc, axis=-1, keepdims=True)
        yoff_rows = []
        for t in range(bt):
            c_t = jnp.where(rowid_n == 0, cc[t:t + 1, :], 0.0)
            s_tg = jnp.concatenate([s_ref[t, SSM_R * g + rr] for rr in range(SSM_R)], axis=0)
            yoff_rows.append(_dot_nt(c_t, s_tg)[0:1])
        yoff = jnp.concatenate(yoff_rows, axis=0)
        lanes = [LANE_DT + SSM_R * g + rr for rr in range(SSM_R)]
        dt_g = jnp.concatenate([jnp.broadcast_to(dt_all[:, l:l + 1], (bt, SSM_P)) for l in lanes], axis=1)
        e_g = jnp.concatenate([jnp.broadcast_to(e_all[:, l:l + 1], (bt, SSM_P)) for l in lanes], axis=1)
        xdt = xs_g * dt_g
        y = cb * xdt + yoff * e_g + d_ref[0:1, g * SSM_GW:(g + 1) * SSM_GW] * xs_g
        for t in range(bt):
            outer = _dot_tn(jnp.where(rowid_g == t, xdt, 0.0), bc)
            for rr in range(SSM_R):
                h = SSM_R * g + rr
                sout_ref[t, h] = (s_ref[t, h] * e_all[t:t + 1, lanes[rr]:lanes[rr] + 1]
                                  + outer[rr * SSM_P:(rr + 1) * SSM_P, :])
        y = y * _silu(z_ref[:, g * SSM_GW:(g + 1) * SSM_GW])
        y_ref[:, g * SSM_GW:(g + 1) * SSM_GW] = _rmsnorm(
            y, nw_ref[0:1, g * SSM_GW:(g + 1) * SSM_GW]).astype(y_ref.dtype)


def _ssd_sample(proj, small, cstate, state, layer, cw, cb, par, d_row, nw):
    t = proj.shape[0]
    bt = SAMPLE_BT
    blk0 = layer * (t // bt)
    const = lambda i: (0, 0)
    return pl.pallas_call(
        _ssd_sample_body,
        grid=(t // bt,),
        in_specs=[pl.BlockSpec((bt, SSM_DINNER), lambda i: (i, OFF_XS // SSM_DINNER)),
                  pl.BlockSpec((bt, SSM_BC), lambda i: (i, OFF_B // SSM_BC)),
                  pl.BlockSpec((bt, SSM_BC), lambda i: (i, OFF_C // SSM_BC)),
                  pl.BlockSpec((bt, SSM_DINNER), lambda i: (i, OFF_Z // SSM_DINNER)),
                  pl.BlockSpec((bt, SMALL_COLS), lambda i: (i, 0)),
                  pl.BlockSpec((bt, (CONV_W - 1) * SSM_CH), lambda i: (blk0 + i, 0)),
                  pl.BlockSpec((bt, SSM_HEADS, SSM_P, SSM_N), lambda i: (blk0 + i, 0, 0, 0)),
                  pl.BlockSpec((CONV_W, SSM_CH), const),
                  pl.BlockSpec((1, SSM_CH), const),
                  pl.BlockSpec((SUBLANES, SMALL_COLS), const),
                  pl.BlockSpec((1, SSM_DINNER), const),
                  pl.BlockSpec((1, SSM_DINNER), const)],
        out_specs=[pl.BlockSpec((bt, SSM_DINNER), lambda i: (i, 0)),
                   pl.BlockSpec((bt, SSM_HEADS, SSM_P, SSM_N), lambda i: (i, 0, 0, 0))],
        out_shape=[jax.ShapeDtypeStruct((t, SSM_DINNER), BF16),
                   jax.ShapeDtypeStruct((t,) + state.shape[1:], F32)],
        compiler_params=_params(1),
        name="ssd_sample",
    )(proj, proj, proj, proj, small, cstate, state, cw, cb, par, d_row, nw)


def _merge_body(oa_ref, yb_ref, ga_ref, gb_ref, x_ref, wg_ref, ws_ref, wo_ref, o_ref):
    a = jnp.dot(oa_ref[...], wg_ref[...], preferred_element_type=F32)
    b = jnp.dot(yb_ref[...], ws_ref[...], preferred_element_type=F32)
    merged = _sigmoid(ga_ref[...].astype(F32)) * a + _sigmoid(gb_ref[...].astype(F32)) * b
    o_ref[...] = x_ref[...] + _dot(merged, wo_ref[...])


def _merge(o_a, y_b, proj, x, wg, ws, wo, *, tm):
    t = x.shape[0]
    tm = min(tm, t)
    assert t % tm == 0
    const = lambda i: (0, 0)
    return pl.pallas_call(
        _merge_body,
        grid=(t // tm,),
        in_specs=[pl.BlockSpec((tm, GDN_V), lambda i: (i, 0)),
                  pl.BlockSpec((tm, SSM_DINNER), lambda i: (i, 0)),
                  pl.BlockSpec((tm, D_MODEL), lambda i: (i, OFF_GA // D_MODEL)),
                  pl.BlockSpec((tm, D_MODEL), lambda i: (i, OFF_GB // D_MODEL)),
                  pl.BlockSpec((tm, D_MODEL), lambda i: (i, 0)),
                  pl.BlockSpec((GDN_V, D_MODEL), const),
                  pl.BlockSpec((SSM_DINNER, D_MODEL), const),
                  pl.BlockSpec((D_MODEL, D_MODEL), const)],
        out_specs=pl.BlockSpec((tm, D_MODEL), lambda i: (i, 0)),
        out_shape=jax.ShapeDtypeStruct((t, D_MODEL), F32),
        compiler_params=_params(1),
        name="merge_out",
    )(o_a, y_b, proj, proj, x, wg, ws, wo)


def _softmax_rows(s):
    e = jnp.exp(s - jnp.max(s, axis=-1, keepdims=True))
    return e / jnp.sum(e, axis=-1, keepdims=True)


def _xattn_prompt_body(x_ref, gx_ref, wq_ref, wo_ref, k_ref, v_ref, o_ref):
    x = x_ref[...]
    q = _dot(_rmsnorm(x, gx_ref[...]), wq_ref[...])
    outs = []
    for h in range(X_HEADS):
        sl = slice(h * X_HD, (h + 1) * X_HD)
        p = _softmax_rows(_dot_nt(q[:, sl], k_ref[:, sl]) * (X_HD ** -0.5))
        outs.append(_dot(p, v_ref[:, sl]))
    o_ref[...] = x + _dot(jnp.concatenate(outs, axis=1), wo_ref[...])


def _xattn_prompt(x, gx, wq, wo, kv, seq, *, tm):
    t = x.shape[0]
    n_mem = kv.shape[0] // (t // seq)
    tm = min(tm, seq)
    assert seq % tm == 0
    per_seq = seq // tm
    const = lambda i: (0, 0)
    return pl.pallas_call(
        _xattn_prompt_body,
        grid=(t // tm,),
        in_specs=[pl.BlockSpec((tm, D_MODEL), lambda i: (i, 0)),
                  pl.BlockSpec((1, D_MODEL), const),
                  pl.BlockSpec((D_MODEL, D_MODEL), const),
                  pl.BlockSpec((D_MODEL, D_MODEL), const),
                  pl.BlockSpec((n_mem, D_MODEL), lambda i: (i // per_seq, 0)),
                  pl.BlockSpec((n_mem, D_MODEL), lambda i: (i // per_seq, 1))],
        out_specs=pl.BlockSpec((tm, D_MODEL), lambda i: (i, 0)),
        out_shape=jax.ShapeDtypeStruct((t, D_MODEL), F32),
        compiler_params=_params(1),
        name="xattn_prompt",
    )(x, gx.reshape(1, D_MODEL), wq, wo, kv, kv)


def _xattn_sample_body(x_ref, gx_ref, wq_ref, wo_ref, k_ref, v_ref, o_ref, q_scr, a_scr):
    i = pl.program_id(0)
    bt = k_ref.shape[0]

    @pl.when(i == 0)
    def _():
        q_scr[...] = _dot(_rmsnorm(x_ref[...], gx_ref[...]), wq_ref[...])

    assert X_ROWS == SUBLANES and X_HALVES == 2
    nrow = k_ref.shape[1]
    sub = lax.broadcasted_iota(jnp.int32, (X_ROWS, nrow), 0)
    lane = lax.broadcasted_iota(jnp.int32, (X_ROWS, nrow), 1)
    own = (lane % X_ROWS) == sub
    lower = sub < X_HEADS
    toks = range(bt)
    q_rows = []
    for tt in toks:
        q_t = q_scr[pl.ds(i * bt + tt, 1), :]
        q_rows.append(jnp.concatenate(
            [q_t[:, h * X_HD + half * LANES:h * X_HD + (half + 1) * LANES]
             for half in range(X_HALVES) for h in range(X_HEADS)], axis=0))
    es = [jnp.where(own, _dot_nt(q_rows[tt], k_ref[tt]), 0.0) for tt in toks]
    p2s = []
    for e in es:
        s = (e + pltpu.roll(pltpu.roll(e, X_HEADS, 0), nrow - X_HEADS, 1)) * (X_HD ** -0.5)
        s = jnp.where(own, s, -jnp.inf)
        p = jnp.exp(s - jnp.max(s, axis=-1, keepdims=True))
        p = jnp.where(lower, p / jnp.sum(p, axis=-1, keepdims=True), 0.0)
        p2s.append(p + pltpu.roll(pltpu.roll(p, X_HEADS, 0), X_HEADS, 1))
    o2s = [_dot(p2s[tt], v_ref[tt]) for tt in toks]
    for tt in toks:
        a_scr[pl.ds(i * bt + tt, 1), :] = jnp.concatenate(
            [o2s[tt][half * X_HEADS + h:half * X_HEADS + h + 1, :]
             for h in range(X_HEADS) for half in range(X_HALVES)], axis=1)

    @pl.when(i == pl.num_programs(0) - 1)
    def _():
        o_ref[...] = x_ref[...] + _dot(a_scr[...], wo_ref[...])


def _xattn_sample(x, gx, wq, wo, ck, cv, layer):
    t = x.shape[0]
    n_mem = ck.shape[1]
    bt = XATTN_BT
    blk0 = layer * (t // bt)
    const = lambda i: (0, 0)
    return pl.pallas_call(
        _xattn_sample_body,
        grid=(t // bt,),
        in_specs=[pl.BlockSpec((t, D_MODEL), const),
                  pl.BlockSpec((1, D_MODEL), const),
                  pl.BlockSpec((D_MODEL, D_MODEL), const),
                  pl.BlockSpec((D_MODEL, D_MODEL), const),
                  pl.BlockSpec((bt,) + ck.shape[1:], lambda i: (blk0 + i, 0, 0)),
                  pl.BlockSpec((bt,) + cv.shape[1:], lambda i: (blk0 + i, 0, 0))],
        out_specs=pl.BlockSpec((t, D_MODEL), const),
        out_shape=jax.ShapeDtypeStruct((t, D_MODEL), F32),
        scratch_shapes=[pltpu.VMEM((t, D_MODEL), F32), pltpu.VMEM((t, D_MODEL), F32)],
        compiler_params=_params(1),
        name="xattn_sample",
    )(x, gx.reshape(1, D_MODEL), wq, wo, ck, cv)


def _mlp_body(x_ref, g_ref, wu_ref, wd_ref, gf_ref, y_ref, hn_ref, acc_ref, *, final_norm):
    j = pl.program_id(1)

    @pl.when(j == 0)
    def _():
        hn_ref[...] = _rmsnorm(x_ref[...], g_ref[...]).astype(BF16)
        acc_ref[...] = jnp.zeros_like(acc_ref)

    hf = jnp.dot(hn_ref[...], wu_ref[...], preferred_element_type=F32)
    acc_ref[...] += _dot(jnp.square(jnp.maximum(hf, 0.0)), wd_ref[...])

    @pl.when(j == pl.num_programs(1) - 1)
    def _():
        x_new = x_ref[...] + acc_ref[...]
        y_ref[...] = _rmsnorm(x_new, gf_ref[...]) if final_norm else x_new


def _mlp(x, g, wu, wd, gf, *, tm, tf, final_norm):
    t = x.shape[0]
    tm = min(tm, t)
    assert t % tm == 0 and D_FF % tf == 0
    const = lambda i, j: (0, 0)
    return pl.pallas_call(
        functools.partial(_mlp_body, final_norm=final_norm),
        grid=(t // tm, D_FF // tf),
        in_specs=[pl.BlockSpec((tm, D_MODEL), lambda i, j: (i, 0)),
                  pl.BlockSpec((1, D_MODEL), const),
                  pl.BlockSpec((D_MODEL, tf), lambda i, j: (0, j)),
                  pl.BlockSpec((tf, D_MODEL), lambda i, j: (j, 0)),
                  pl.BlockSpec((1, D_MODEL), const)],
        out_specs=pl.BlockSpec((tm, D_MODEL), lambda i, j: (i, 0)),
        out_shape=jax.ShapeDtypeStruct((t, D_MODEL), F32),
        scratch_shapes=[pltpu.VMEM((tm, D_MODEL), BF16), pltpu.VMEM((tm, D_MODEL), F32)],
        compiler_params=_params(2),
        name="mlp",
    )(x, g.reshape(1, D_MODEL), wu, wd, gf.reshape(1, D_MODEL))


def _lane_row(vec, lane0):
    return jnp.zeros((SMALL_COLS,), F32).at[lane0:lane0 + vec.shape[0]].set(vec.astype(F32))


def _param_rows(*rows):
    out = jnp.zeros((SUBLANES, SMALL_COLS), F32)
    for i, r in enumerate(rows):
        out = out.at[i].set(r)
    return out


def kernel(x_prompt, x_sample, mem_prompt, state_gdn_conv, state_gdn, state_ssm_conv, state_ssm, cache_mem_k, cache_mem_v, g_mix, w_in, gdn_conv_w, gdn_A_log, gdn_dt_bias, gdn_norm_w, w_gdn_up, ssm_conv_w, ssm_conv_b, ssm_dt_bias, ssm_A_log, ssm_D, ssm_norm_w, w_ssm_up, w_out, g_mem, w_mk, w_mv, g_x, w_cq, w_co, g_ff, w_ff_up, w_ff_down, g_final):
    bp, seq, _ = x_prompt.shape
    bs, dec_seq, _ = x_sample.shape
    depth = w_in.shape[0]
    n_mem = mem_prompt.shape[1]
    assert seq % STEP == 0 and dec_seq == 1
    assert bs % SAMPLE_BT == 0 and bs % GDN_SAMPLE_BT == 0 and bs % XATTN_BT == 0
    tp = bp * seq

    xp = x_prompt.reshape(tp, D_MODEL)
    xs = x_sample.reshape(bs, D_MODEL)
    mem = mem_prompt.reshape(bp * n_mem, D_MODEL)
    gst_all = state_gdn_conv.astype(F32).reshape(depth * bs, (CONV_W - 1) * GDN_CH)
    sst_all = state_ssm_conv.astype(F32).reshape(depth * bs, (CONV_W - 1) * SSM_CH)
    sg_all = state_gdn.astype(F32).reshape(depth * bs, GDN_HEADS, GDN_DK, GDN_DV)
    ss_all = state_ssm.astype(F32).reshape(depth * bs, SSM_HEADS, SSM_P, SSM_N)
    def cache_rows(c):
        c = c.astype(F32).reshape(depth * bs, n_mem, X_HEADS, X_HALVES, LANES)
        return c.transpose(0, 1, 3, 2, 4).reshape(depth * bs, n_mem * X_ROWS, LANES)

    ck_all = cache_rows(cache_mem_k)
    cv_all = cache_rows(cache_mem_v)
    outs ={k: [] for k in ("p_gc", "p_g", "p_sc", "p_s", "p_mk", "p_mv", "s_gc", "s_g", "s_sc", "s_s")}

    starts = [0]
    for width in IN_SPLITS:
        starts.append(starts[-1] + width)
    col = lambda i: slice(starts[i], starts[i + 1])

    for l in range(depth):
        wl = w_in[l]
        w_main = jnp.concatenate([wl[:, starts[0]:starts[2]], wl[:, starts[4]:starts[6]],
                                  wl[:, starts[7]:starts[9]]], axis=1).astype(BF16)
        w_small = jnp.concatenate(
            [wl[:, col(2)], wl[:, col(3)], wl[:, col(6)],
             jnp.zeros((D_MODEL, SMALL_COLS - 2 * GDN_HEADS - SSM_HEADS), F32)], axis=1).astype(BF16)
        gdn_par = _param_rows(_lane_row(gdn_A_log[l], LANE_GDEC), _lane_row(gdn_dt_bias[l], LANE_GDEC))
        ssm_par = _param_rows(_lane_row(ssm_dt_bias[l], LANE_DT), _lane_row(ssm_A_log[l], LANE_DT))
        gdn_nw = gdn_norm_w[l].reshape(1, GDN_DV).astype(F32)
        ssm_nw = ssm_norm_w[l].reshape(1, SSM_DINNER).astype(F32)
        d_row = jnp.repeat(ssm_D[l].astype(F32), SSM_P).reshape(1, SSM_DINNER)
        gcw = gdn_conv_w[l].astype(F32)
        scw = ssm_conv_w[l].astype(F32)
        scb = ssm_conv_b[l].reshape(1, SSM_CH).astype(F32)
        wg, ws, wo = w_gdn_up[l].astype(BF16), w_ssm_up[l].astype(BF16), w_out[l].astype(BF16)
        wq, wco = w_cq[l].astype(BF16), w_co[l].astype(BF16)
        wu, wd = w_ff_up[l].astype(BF16), w_ff_down[l].astype(BF16)
        last = l == depth - 1

        pt, st = PROMPT_TILES, SAMPLE_TILES
        w_kv = jnp.concatenate([w_mk[l], w_mv[l]], axis=1).astype(BF16)
        kv = _rms_mm(mem, g_mem[l], w_kv, tm=pt["kv"][0], tn=pt["kv"][1])
        outs["p_mk"].append(kv[:, :D_MODEL].reshape(bp, n_mem, X_HEADS, X_HD))
        outs["p_mv"].append(kv[:, D_MODEL:].reshape(bp, n_mem, X_HEADS, X_HD))

        proj, small = _in_proj(xp, g_mix[l], w_main, w_small, tm=pt["in_proj"][0], tn=pt["in_proj"][1],
                               out_dtype=BF16)
        o_a, p_g = _gdn_prompt(proj, small, gcw, gdn_par, gdn_nw, bp, seq)
        y_b, p_s = _ssd_prompt(proj, small, scw, scb, ssm_par, d_row, ssm_nw, bp, seq)
        proj3 = proj.reshape(bp, seq, MAIN_COLS)
        outs["p_gc"].append(proj3[:, seq - (CONV_W - 1):, OFF_QKV:OFF_QKV + GDN_CH].astype(F32))
        outs["p_sc"].append(proj3[:, seq - (CONV_W - 1):, OFF_XS:OFF_XS + SSM_CH].astype(F32))
        outs["p_g"].append(p_g)
        outs["p_s"].append(p_s)
        xp = _merge(o_a, y_b, proj, xp, wg, ws, wo, tm=pt["merge"])
        xp = _xattn_prompt(xp, g_x[l], wq, wco, kv, seq, tm=pt["xattn"])
        xp = _mlp(xp, g_ff[l], wu, wd, g_final, tm=pt["mlp"][0], tf=pt["mlp"][1], final_norm=last)

        proj, small = _in_proj(xs, g_mix[l], w_main, w_small, tm=st["in_proj"][0], tn=st["in_proj"][1],
                               out_dtype=F32)
        o_a, s_g = _gdn_sample(proj, small, gst_all, sg_all, l, gcw, gdn_par, gdn_nw)
        y_b, s_s = _ssd_sample(proj, small, sst_all, ss_all, l, scw, scb, ssm_par, d_row, ssm_nw)
        outs["s_gc"].append(jnp.concatenate(
            [state_gdn_conv[l, :, 1:].astype(F32), proj[:, None, OFF_QKV:OFF_QKV + GDN_CH]], axis=1))
        outs["s_sc"].append(jnp.concatenate(
            [state_ssm_conv[l, :, 1:].astype(F32), proj[:, None, OFF_XS:OFF_XS + SSM_CH]], axis=1))
        outs["s_g"].append(s_g)
        outs["s_s"].append(s_s)
        xs = _merge(o_a, y_b, proj, xs, wg, ws, wo, tm=st["merge"])
        xs = _xattn_sample(xs, g_x[l], wq, wco, ck_all, cv_all, l)
        xs = _mlp(xs, g_ff[l], wu, wd, g_final, tm=st["mlp"][0], tf=st["mlp"][1], final_norm=last)

    stack = lambda k: jnp.stack(outs[k])
    return (xp.reshape(bp, seq, D_MODEL), xs.reshape(bs, dec_seq, D_MODEL),
            stack("p_gc"), stack("p_g"), stack("p_sc"), stack("p_s"), stack("p_mk"), stack("p_mv"),
            stack("s_gc"), stack("s_g"), stack("s_sc"), stack("s_s"))
```
